```python
import jax, jax.numpy as jnp
from jax import lax
import numpy as np

D_MODEL = 1024
BATCH = 2
SEQ = 8192
DEPTH = 2
DEC_BATCH = 16
DEC_SEQ = 16
PAST_LEN = 4096

CHUNK = 64
POOL_WIDTH = D_MODEL // 4
POOL_GROUPS = 4
POOL_GROUP_DIM = POOL_WIDTH // POOL_GROUPS
POOL_WINDOWS = (2, 4, 8, 16)
POOL_HIST = max(POOL_WINDOWS) - 1
ATT_WIDTH = D_MODEL - POOL_WIDTH
HEAD_DIM = 64
N_HEADS = ATT_WIDTH // HEAD_DIM
MIX_WIDTH = POOL_WIDTH + ATT_WIDTH
IN_COLS = POOL_WIDTH + 3 * ATT_WIDTH + N_HEADS
D_FF = 4 * D_MODEL
Q_BLOCK = 128
EPS = 1e-6
FORGET_W_SCALE = 0.1

kernel_name = "hymba_pool_fox_stream_step"


def rms_norm(x):
    xf = x.astype(jnp.float32)
    return (xf * lax.rsqrt(jnp.mean(xf * xf, axis=-1, keepdims=True) + EPS)).astype(x.dtype)


def modulate(x, shift, scale):
    return x * (1 + scale[:, None, :]) + shift[:, None, :]


def pool_mixer(u, hist, pos0, w_pool, pool_scale):
    B, T, _ = u.shape
    z = jnp.concatenate([hist.astype(u.dtype), u], axis=1)
    zf = z.astype(jnp.float32)
    cs = jnp.concatenate([jnp.zeros_like(zf[:, :1]), jnp.cumsum(zf, axis=1)], axis=1)
    pos = (pos0 + jnp.arange(T)).astype(jnp.float32)
    end = POOL_HIST + 1
    uf = u.astype(jnp.float32)
    outs = []
    for g, w in enumerate(POOL_WINDOWS):
        sl = slice(g * POOL_GROUP_DIM, (g + 1) * POOL_GROUP_DIM)
        win_sum = cs[:, end:end + T, sl] - cs[:, end - w:end - w + T, sl]
        count = jnp.minimum(pos + 1.0, float(w))[None, :, None]
        outs.append(win_sum / count - uf[..., sl])
    mixed = jnp.stack(outs, axis=2).astype(u.dtype)
    y = jnp.einsum('btgc,gcd->btgd', mixed, w_pool).reshape(B, T, POOL_WIDTH)
    return y * pool_scale, z[:, -POOL_HIST:]


def fox_block(q, k, v, fq, fk, q_pos, k_pos):
    s = jnp.einsum('bqhd,bkhd->bhqk', q, k, preferred_element_type=jnp.float32) * (HEAD_DIM ** -0.5)
    s = s + jnp.swapaxes(fq, 1, 2)[..., :, None] - jnp.swapaxes(fk, 1, 2)[..., None, :]
    mask = (k_pos[None, :] <= q_pos[:, None])[None, None]
    p = jax.nn.softmax(jnp.where(mask, s, -jnp.inf), axis=-1)
    return jnp.einsum('bhqk,bkhd->bqhd', p.astype(v.dtype), v)


def fox_prompt(q, k, v, F):
    B, T = q.shape[:2]
    nb = T // Q_BLOCK
    qb = q.reshape(B, nb, Q_BLOCK, N_HEADS, HEAD_DIM).swapaxes(0, 1)
    fb = F.reshape(B, nb, Q_BLOCK, N_HEADS).swapaxes(0, 1)
    pb = jnp.arange(T).reshape(nb, Q_BLOCK)
    k_pos = jnp.arange(T)
    out = lax.map(lambda a: fox_block(a[0], k, v, a[1], F, a[2], k_pos), (qb, fb, pb))
    return out.swapaxes(0, 1).reshape(B, T, N_HEADS, HEAD_DIM)


def trunk_layer(x, c, pool_hist, past_k, past_v, past_logf, pos0,
                w_mod, b_mod, w_in, b_f, q_gain, k_gain, w_pool, pool_scale, w_out, w_up, w_down):
    B, T, _ = x.shape

    mod = jax.nn.silu(c) @ w_mod + b_mod
    sh1, sc1, g1, sh2, sc2, g2 = jnp.split(mod, 6, axis=-1)

    h = modulate(rms_norm(x), sh1, sc1)
    proj = h @ w_in
    u = proj[..., :POOL_WIDTH]
    o = POOL_WIDTH
    q = proj[..., o:o + ATT_WIDTH].reshape(B, T, N_HEADS, HEAD_DIM)
    k = proj[..., o + ATT_WIDTH:o + 2 * ATT_WIDTH].reshape(B, T, N_HEADS, HEAD_DIM)
    v = proj[..., o + 2 * ATT_WIDTH:o + 3 * ATT_WIDTH].reshape(B, T, N_HEADS, HEAD_DIM)
    f_logit = proj[..., o + 3 * ATT_WIDTH:]
    q = rms_norm(q) * q_gain
    k = rms_norm(k) * k_gain
    logf = jax.nn.log_sigmoid(f_logit.astype(jnp.float32) + b_f.astype(jnp.float32))

    pool_y, pool_state = pool_mixer(u, pool_hist, pos0, w_pool, pool_scale)

    if past_k is None:
        att = fox_prompt(q, k, v, jnp.cumsum(logf, axis=1))
    else:
        P = past_k.shape[1]
        k_all = jnp.concatenate([past_k.astype(k.dtype), k], axis=1)
        v_all = jnp.concatenate([past_v.astype(v.dtype), v], axis=1)
        F_all = jnp.cumsum(jnp.concatenate([past_logf.astype(jnp.float32), logf], axis=1), axis=1)
        att = fox_block(q, k_all, v_all, F_all[:, P:], F_all, P + jnp.arange(T), jnp.arange(P + T))

    mix = jnp.concatenate([pool_y, att.reshape(B, T, ATT_WIDTH)], axis=-1)
    x = x + g1[:, None, :] * (mix @ w_out)

    h2 = modulate(rms_norm(x), sh2, sc2)
    ff = jnp.square(jax.nn.relu(h2 @ w_up)) @ w_down
    x = x + g2[:, None, :] * ff
    return x, k, v, logf.astype(x.dtype), pool_state


def setup_inputs(seed: int = 0) -> dict:
    key = jax.random.key(seed)
    ks = jax.random.split(key, 20)
    f32 = jnp.float32

    def nrm(k, shape, s=1.0):
        return s * jax.random.normal(k, shape, f32)

    head_bias = jnp.linspace(1.0, 7.0, N_HEADS, dtype=f32)
    w_in = nrm(ks[10], (DEPTH, D_MODEL, IN_COLS), D_MODEL ** -0.5)
    w_in = w_in.at[..., POOL_WIDTH + 3 * ATT_WIDTH:].multiply(FORGET_W_SCALE)
    b_f = head_bias[None, :] + nrm(ks[11], (DEPTH, N_HEADS), 0.1)
    cache_logf = jax.nn.log_sigmoid(head_bias + nrm(ks[6], (DEPTH, DEC_BATCH, PAST_LEN, N_HEADS), 0.3))
    return {
        'x_prompt': nrm(ks[0], (BATCH, SEQ, D_MODEL)),
        'x_sample': nrm(ks[1], (DEC_BATCH, DEC_SEQ, D_MODEL)),
        'c_prompt': nrm(ks[2], (BATCH, D_MODEL)),
        'c_sample': nrm(ks[3], (DEC_BATCH, D_MODEL)),
        'cache_k': nrm(ks[4], (DEPTH, DEC_BATCH, PAST_LEN, N_HEADS, HEAD_DIM)),
        'cache_v': nrm(ks[5], (DEPTH, DEC_BATCH, PAST_LEN, N_HEADS, HEAD_DIM)),
        'cache_logf': cache_logf,
        'cache_pool': nrm(ks[7], (DEPTH, DEC_BATCH, POOL_HIST, POOL_WIDTH)),
        'w_mod': nrm(ks[8], (DEPTH, D_MODEL, 6 * D_MODEL), 0.5 * D_MODEL ** -0.5),
        'b_mod': nrm(ks[9], (DEPTH, 6 * D_MODEL), 0.02),
        'w_in': w_in,
        'b_f': b_f,
        'q_gain': 1.0 + nrm(ks[12], (DEPTH, HEAD_DIM), 0.02),
        'k_gain': 1.0 + nrm(ks[13], (DEPTH, HEAD_DIM), 0.02),
        'w_pool': nrm(ks[14], (DEPTH, POOL_GROUPS, POOL_GROUP_DIM, POOL_GROUP_DIM), POOL_GROUP_DIM ** -0.5),
        'pool_scale': 1.0 + nrm(ks[15], (DEPTH, POOL_WIDTH), 0.02),
        'w_out': nrm(ks[16], (DEPTH, MIX_WIDTH, D_MODEL), MIX_WIDTH ** -0.5),
        'w_up': nrm(ks[17], (DEPTH, D_MODEL, D_FF), D_MODEL ** -0.5),
        'w_down': nrm(ks[18], (DEPTH, D_FF, D_MODEL), D_FF ** -0.5),
    }


def reference(x_prompt, x_sample, c_prompt, c_sample, cache_k, cache_v, cache_logf, cache_pool,
              w_mod, b_mod, w_in, b_f, q_gain, k_gain, w_pool, pool_scale, w_out, w_up, w_down):
    y_prompt, y_sample = x_prompt, x_sample
    k_p, v_p, f_p, p_p = [], [], [], []
    k_s, v_s, f_s, p_s = [], [], [], []
    past = cache_k.shape[2]
    pool_zero = jnp.zeros((x_prompt.shape[0], POOL_HIST, POOL_WIDTH), x_prompt.dtype)
    for l in range(DEPTH):
        wts = (w_mod[l], b_mod[l], w_in[l], b_f[l], q_gain[l], k_gain[l],
               w_pool[l], pool_scale[l], w_out[l], w_up[l], w_down[l])
        y_prompt, kl, vl, fl, pl = trunk_layer(y_prompt, c_prompt, pool_zero, None, None, None, 0, *wts)
        k_p.append(kl); v_p.append(vl); f_p.append(fl); p_p.append(pl)
        y_sample, kl, vl, fl, pl = trunk_layer(y_sample, c_sample, cache_pool[l], cache_k[l], cache_v[l],
                                               cache_logf[l], past, *wts)
        k_s.append(kl); v_s.append(vl); f_s.append(fl); p_s.append(pl)
    return (y_prompt, y_sample,
            jnp.stack(k_p), jnp.stack(v_p), jnp.stack(f_p), jnp.stack(p_p),
            jnp.stack(k_s), jnp.stack(v_s), jnp.stack(f_s), jnp.stack(p_s))
```

```python
import functools

import numpy as np
import jax
import jax.numpy as jnp
from jax import lax
from jax.experimental import pallas as pl
from jax.experimental.pallas import tpu as pltpu

D_MODEL = 1024
POOL_WIDTH = 256
POOL_GROUP_DIM = 64
POOL_WINDOWS = (2, 4, 8, 16)
POOL_HIST = 15
HIST_ROWS = 16
ATT_WIDTH = 768
HEAD_DIM = 64
N_HEADS = 12
HEADS_PAD = 16
D_FF = 4096
EPS = 1e-6
NEG = -1e30

LANES = 128
CUMSUM_CHUNK = 256
VMEM_LIMIT = 56 * 1024 * 1024

F32 = jnp.float32
BF16 = jnp.bfloat16


def _dot(a, b):
    return jnp.dot(a, b, preferred_element_type=F32)


def _dot_nt(a, b):
    return lax.dot_general(a, b, (((1,), (1,)), ((), ())), preferred_element_type=F32)


def _split_dot(a, b):
    hi = a.astype(BF16)
    lo = (a - hi.astype(F32)).astype(BF16)
    return _dot(hi, b) + _dot(lo, b)


def _resident(shape):
    nd = len(shape)
    return pl.BlockSpec(shape, lambda *_: (0,) * nd, pipeline_mode=pl.Buffered(1))


def _mod_kernel(c_ref, w_ref, b_ref, o_ref):
    c = c_ref[...]
    s = c / (1.0 + jnp.exp(-c))
    o_ref[...] = _dot(s.astype(BF16), w_ref[...].astype(BF16)) + b_ref[...]


def _modulation(c_all, w_mod, b_mod):
    depth, _, n = w_mod.shape
    rows = c_all.shape[0]
    tn = 1536
    return pl.pallas_call(
        _mod_kernel,
        grid=(depth, n // tn),
        in_specs=[
            pl.BlockSpec((rows, D_MODEL), lambda l, j: (0, 0)),
            pl.BlockSpec((None, D_MODEL, tn), lambda l, j: (l, 0, j)),
            pl.BlockSpec((None, 1, tn), lambda l, j: (l, 0, j)),
        ],
        out_specs=pl.BlockSpec((None, rows, tn), lambda l, j: (l, 0, j)),
        out_shape=jax.ShapeDtypeStruct((depth, rows, n), F32),
        compiler_params=pltpu.CompilerParams(
            dimension_semantics=("arbitrary", "arbitrary"), vmem_limit_bytes=VMEM_LIMIT),
        name="modulation",
    )(c_all, w_mod, b_mod.reshape(depth, 1, n))


def _head_rms(a, ind, ind_t):
    ssum = _split_dot(a * a, ind)
    r = lax.rsqrt(ssum * (1.0 / HEAD_DIM) + EPS)
    return a * _split_dot(r, ind_t)


def _log_sigmoid(x):
    return jnp.minimum(x, 0.0) - jnp.log1p(jnp.exp(-jnp.abs(x)))


def _in_proj_kernel(*refs, n_alias, bb, tt, pos0, transposed):
    refs = refs[n_alias:]
    (x_ref, sh_ref, sc_ref, hist_ref, wu_ref, wq_ref, wk_ref, wv_ref, wf_ref, wft_ref,
     bfr_ref, bfc_ref, qg_ref, kg_ref, ind_ref, indt_ref, wp_ref, ps_ref,
     k32_ref, v32_ref, logf_ref, pstate_ref, kbf_ref, q_ref) = refs[:24]
    vt_ref = refs[24] if transposed else None
    logft_ref, py_ref, z_ref = refs[-3:]
    ti = pl.program_id(1)
    m = bb * tt

    x = x_ref[...]
    ms = jnp.mean(x * x, axis=-1, keepdims=True)
    h = (x * lax.rsqrt(ms + EPS)) * (1.0 + sc_ref[...]) + sh_ref[...]
    hb = h.reshape(m, D_MODEL).astype(BF16)

    ind = ind_ref[...]
    ind_t = indt_ref[...]
    q = _head_rms(_dot(hb, wq_ref[...]), ind, ind_t) * qg_ref[...]
    k = _head_rms(_dot(hb, wk_ref[...]), ind, ind_t) * kg_ref[...]
    v = _dot(hb, wv_ref[...])
    k32_ref[...] = k.reshape(bb, tt, ATT_WIDTH)
    v32_ref[...] = v.reshape(bb, tt, ATT_WIDTH)
    kbf_ref[...] = k.reshape(bb, tt, ATT_WIDTH).astype(BF16)
    qs = q * (HEAD_DIM ** -0.5)
    if transposed:
        q_ref[...] = qs.T.astype(BF16)
        vt_ref[...] = v.T.astype(BF16)
    else:
        q_ref[...] = qs.reshape(bb, tt, ATT_WIDTH).astype(BF16)

    fl = _dot(hb, wf_ref[...]) + bfr_ref[...]
    logf_ref[...] = _log_sigmoid(fl)[:, :N_HEADS].reshape(bb, tt, N_HEADS)
    flt = _dot_nt(wft_ref[...], hb) + bfc_ref[...]
    logft_ref[...] = _log_sigmoid(flt)

    u3 = _dot(hb, wu_ref[...]).reshape(bb, tt, POOL_WIDTH)

    @pl.when(ti == 0)
    def _():
        z_ref[:, 0:HIST_ROWS, :] = hist_ref[...]

    z_ref[:, HIST_ROWS:HIST_ROWS + tt, :] = u3
    acc = u3
    sums = {}
    for i in range(1, max(POOL_WINDOWS)):
        acc = acc + z_ref[:, HIST_ROWS - i:HIST_ROWS - i + tt, :]
        if i + 1 in POOL_WINDOWS:
            sums[i + 1] = acc
    group = lax.broadcasted_iota(jnp.int32, (1, 1, POOL_WIDTH), 2) // POOL_GROUP_DIM
    win = sums[POOL_WINDOWS[-1]]
    width = jnp.full((1, 1, POOL_WIDTH), float(POOL_WINDOWS[-1]), F32)
    for g in range(len(POOL_WINDOWS) - 2, -1, -1):
        win = jnp.where(group == g, sums[POOL_WINDOWS[g]], win)
        width = jnp.where(group == g, float(POOL_WINDOWS[g]), width)
    pos = (pos0 + ti * tt + lax.broadcasted_iota(jnp.int32, (1, tt, 1), 1)).astype(F32)
    count = jnp.minimum(pos + 1.0, width)
    mixed = win / count - u3
    y = _dot(mixed.reshape(m, POOL_WIDTH).astype(BF16), wp_ref[...]) * ps_ref[...]
    py_ref[...] = y.reshape(bb, tt, POOL_WIDTH).astype(BF16)

    pstate_ref[...] = z_ref[:, tt + 1:tt + HIST_ROWS, :]
    z_ref[:, 0:HIST_ROWS, :] = z_ref[:, tt:tt + HIST_ROWS, :]


def _in_proj(x, sh, sc, hist, w, slabs, layer, *, bb, tt, pos0, transposed):
    b, t, _ = x.shape
    depth = w["depth"]
    nb, nt = b // bb, t // tt
    m = bb * tt
    n_alias = 0 if slabs is None else 4

    def per_b(shape):
        return pl.BlockSpec(shape, lambda i, j: (i,) + (0,) * (len(shape) - 1))

    in_specs = [pl.BlockSpec(memory_space=pl.ANY)] * n_alias + [
        pl.BlockSpec((bb, tt, D_MODEL), lambda i, j: (i, j, 0)),
        per_b((bb, 1, D_MODEL)), per_b((bb, 1, D_MODEL)), per_b((bb, HIST_ROWS, POOL_WIDTH)),
        _resident((D_MODEL, POOL_WIDTH)), _resident((D_MODEL, ATT_WIDTH)),
        _resident((D_MODEL, ATT_WIDTH)), _resident((D_MODEL, ATT_WIDTH)),
        _resident((D_MODEL, LANES)), _resident((HEADS_PAD, D_MODEL)),
        _resident((1, LANES)), _resident((HEADS_PAD, 1)),
        _resident((1, ATT_WIDTH)), _resident((1, ATT_WIDTH)),
        _resident((ATT_WIDTH, LANES)), _resident((LANES, ATT_WIDTH)),
        _resident((POOL_WIDTH, POOL_WIDTH)), _resident((1, POOL_WIDTH)),
    ]
    slab_shapes = [
        jax.ShapeDtypeStruct((depth, b, t, ATT_WIDTH), F32),
        jax.ShapeDtypeStruct((depth, b, t, ATT_WIDTH), F32),
        jax.ShapeDtypeStruct((depth, b, t, N_HEADS), F32),
        jax.ShapeDtypeStruct((depth, b, POOL_HIST, POOL_WIDTH), F32),
    ]
    slab_specs = [
        pl.BlockSpec((None, bb, tt, ATT_WIDTH), lambda i, j: (layer, i, j, 0)),
        pl.BlockSpec((None, bb, tt, ATT_WIDTH), lambda i, j: (layer, i, j, 0)),
        pl.BlockSpec((None, bb, tt, N_HEADS), lambda i, j: (layer, i, j, 0)),
        pl.BlockSpec((None, bb, POOL_HIST, POOL_WIDTH), lambda i, j: (layer, i, 0, 0)),
    ]
    if transposed:
        q_shape = jax.ShapeDtypeStruct((b, ATT_WIDTH, t), BF16)
        q_spec = pl.BlockSpec((None, ATT_WIDTH, tt), lambda i, j: (i, 0, j))
    else:
        q_shape = jax.ShapeDtypeStruct((b, t, ATT_WIDTH), BF16)
        q_spec = pl.BlockSpec((bb, tt, ATT_WIDTH), lambda i, j: (i, j, 0))
    out_shape = slab_shapes + [
        jax.ShapeDtypeStruct((b, t, ATT_WIDTH), BF16),
        q_shape,
    ] + ([q_shape] if transposed else []) + [
        jax.ShapeDtypeStruct((nb, HEADS_PAD, nt * m), F32),
        jax.ShapeDtypeStruct((b, t, POOL_WIDTH), BF16),
    ]
    out_specs = slab_specs + [
        pl.BlockSpec((bb, tt, ATT_WIDTH), lambda i, j: (i, j, 0)),
        q_spec,
    ] + ([q_spec] if transposed else []) + [
        pl.BlockSpec((None, HEADS_PAD, m), lambda i, j: (i, 0, j)),
        pl.BlockSpec((bb, tt, POOL_WIDTH), lambda i, j: (i, j, 0)),
    ]
    args = ([] if slabs is None else list(slabs)) + [
        x, sh, sc, hist, w["w_u"], w["w_q"], w["w_k"], w["w_v"], w["w_f"], w["w_ft"],
        w["bf_row"], w["bf_col"], w["qg"], w["kg"], w["ind"], w["ind_t"], w["w_pool"],
        w["pool_scale"],
    ]
    outs = pl.pallas_call(
        functools.partial(_in_proj_kernel, n_alias=n_alias, bb=bb, tt=tt, pos0=pos0,
                          transposed=transposed),
        grid=(nb, nt),
        in_specs=in_specs,
        out_specs=out_specs,
        out_shape=out_shape,
        scratch_shapes=[pltpu.VMEM((bb, tt + HIST_ROWS, POOL_WIDTH), F32)],
        input_output_aliases={i: i for i in range(n_alias)},
        compiler_params=pltpu.CompilerParams(
            dimension_semantics=("arbitrary", "arbitrary"), vmem_limit_bytes=VMEM_LIMIT),
        name="in_proj_prompt" if transposed else "in_proj_sample",
    )(*args)
    return outs[:4], outs[4:]


def _cumsum_kernel(x_ref, gcol_ref, grow_ref, o_ref):
    x = x_ref[...]
    rb, c = x.shape
    tri = (lax.broadcasted_iota(jnp.int32, (c, c), 0)
           <= lax.broadcasted_iota(jnp.int32, (c, c), 1)).astype(F32)
    y = jnp.dot(x, tri, precision=lax.Precision.HIGHEST, preferred_element_type=F32)
    tot = jnp.broadcast_to(y[:, c - 1:c], (rb, LANES))
    before = ((gcol_ref[...] == grow_ref[...])
              & (lax.broadcasted_iota(jnp.int32, (rb, rb), 1)
                 < lax.broadcasted_iota(jnp.int32, (rb, rb), 0))).astype(F32)
    off = jnp.dot(before, tot, precision=lax.Precision.HIGHEST, preferred_element_type=F32)
    o_ref[...] = y + off[:, 0:1]


def _cumsum_time(x, rows_per_block):
    lead, length = x.shape[:-1], x.shape[-1]
    chunks = length // CUMSUM_CHUNK
    r = int(np.prod(lead)) * chunks
    rb = rows_per_block
    gid = (np.arange(rb) // chunks).astype(np.int32)
    out = pl.pallas_call(
        _cumsum_kernel,
        grid=(r // rb,),
        in_specs=[
            pl.BlockSpec((rb, CUMSUM_CHUNK), lambda i: (i, 0)),
            _resident((rb, 1)), _resident((1, rb)),
        ],
        out_specs=pl.BlockSpec((rb, CUMSUM_CHUNK), lambda i: (i, 0)),
        out_shape=jax.ShapeDtypeStruct((r, CUMSUM_CHUNK), F32),
        compiler_params=pltpu.CompilerParams(
            dimension_semantics=("arbitrary",), vmem_limit_bytes=VMEM_LIMIT),
        name="cumsum_time",
    )(x.reshape(r, CUMSUM_CHUNK), jnp.asarray(gid).reshape(rb, 1), jnp.asarray(gid).reshape(1, rb))
    return out.reshape(x.shape)


def _attn_prompt_kernel(qt_ref, k_ref, vt_ref, ft_ref, fcol_ref, o_ref, fkb_ref, *, tq):
    tk = tq
    hp = pl.program_id(1)
    qi = pl.program_id(2)
    t_len = k_ref.shape[0]
    chunk = 512

    @pl.when(qi == 0)
    def _():
        lane = lax.broadcasted_iota(jnp.int32, (chunk, HEADS_PAD), 1)
        for hh in range(2):
            def fill(c, carry, hh=hh):
                r0 = pl.multiple_of(c * chunk, chunk)
                blk = fcol_ref[pl.ds(r0, chunk), :]
                col = jnp.sum(jnp.where(lane == 2 * hp + hh, blk, 0.0), axis=1, keepdims=True)
                fkb_ref[hh, pl.ds(r0, chunk), :] = jnp.broadcast_to(col, (chunk, LANES))
                return carry
            lax.fori_loop(0, t_len // chunk, fill, 0)

    row_head = lax.broadcasted_iota(jnp.int32, (2 * HEAD_DIM, 1), 0) // HEAD_DIM
    causal = (lax.broadcasted_iota(jnp.int32, (tk, tq), 0)
              <= lax.broadcasted_iota(jnp.int32, (tk, tq), 1))
    ones_rows = jnp.ones((HEADS_PAD, tk), BF16)
    qt = qt_ref[...]
    outs = []
    for hh in range(2):
        qh = jnp.where(row_head == hh, qt, jnp.zeros_like(qt))
        fq = ft_ref[pl.ds(2 * hp + hh, 1), :]

        def tile(j, carry, masked, hh=hh, qh=qh, fq=fq):
            acc, m_run = carry
            k0 = pl.multiple_of(j * tk, tk)
            st = _dot(k_ref[pl.ds(k0, tk), :], qh)
            fk = fkb_ref[hh, pl.ds(k0, tk), :]
            a = st - jnp.concatenate([fk] * (tq // LANES), axis=1)
            if masked:
                a = jnp.where(causal, a, NEG)
            m_new = jnp.maximum(m_run, jnp.max(a, axis=0, keepdims=True) + fq)
            alpha = jnp.exp(m_run - m_new)
            p = jnp.exp(a + (fq - m_new)).astype(BF16)
            vb = jnp.concatenate(
                [vt_ref[pl.ds(hh * HEAD_DIM, HEAD_DIM), pl.ds(k0, tk)], ones_rows], axis=0)
            return alpha * acc + _dot(vb, p), m_new

        init = (jnp.zeros((HEAD_DIM + HEADS_PAD, tq), F32), jnp.full((1, tq), NEG, F32))
        carry = lax.fori_loop(0, qi, functools.partial(tile, masked=False), init)
        acc, _ = tile(qi, carry, True)
        outs.append(acc[0:HEAD_DIM] / acc[HEAD_DIM:HEAD_DIM + 1])
    o_ref[...] = jnp.concatenate(outs, axis=0).T.astype(BF16)


def _attn_prompt(qt, kbf, vt, ft, fcol, *, tq):
    b, t, _ = kbf.shape
    pairs = N_HEADS // 2
    return pl.pallas_call(
        functools.partial(_attn_prompt_kernel, tq=tq),
        grid=(b, pairs, t // tq),
        in_specs=[
            pl.BlockSpec((None, 2 * HEAD_DIM, tq), lambda i, p, j: (i, p, j)),
            pl.BlockSpec((None, t, 2 * HEAD_DIM), lambda i, p, j: (i, 0, p)),
            pl.BlockSpec((None, 2 * HEAD_DIM, t), lambda i, p, j: (i, p, 0)),
            pl.BlockSpec((None, HEADS_PAD, tq), lambda i, p, j: (i, 0, j)),
            pl.BlockSpec((None, t, HEADS_PAD), lambda i, p, j: (i, 0, 0)),
        ],
        out_specs=pl.BlockSpec((None, tq, 2 * HEAD_DIM), lambda i, p, j: (i, j, p)),
        out_shape=jax.ShapeDtypeStruct((b, t, ATT_WIDTH), BF16),
        scratch_shapes=[pltpu.VMEM((2, t, LANES), F32)],
        compiler_params=pltpu.CompilerParams(
            dimension_semantics=("arbitrary", "arbitrary", "arbitrary"),
            vmem_limit_bytes=VMEM_LIMIT),
        name="attn_prompt",
    )(qt, kbf, vt, ft, fcol)


def _attn_sample_kernel(q_ref, ck_ref, cv_ref, kn_ref, vn_ref, f_ref, fq_ref, o_ref,
                        qbd_ref, m_ref, l_ref, acc_ref, *, tk, past, tt):
    j = pl.program_id(1)
    rows = N_HEADS * tt
    own = ((lax.broadcasted_iota(jnp.int32, (rows, ATT_WIDTH), 0) // tt)
           == (lax.broadcasted_iota(jnp.int32, (rows, ATT_WIDTH), 1) // HEAD_DIM))

    @pl.when(j == 0)
    def _():
        q = q_ref[...]
        qbd_ref[...] = jnp.where(own, jnp.concatenate([q] * N_HEADS, axis=0), jnp.zeros((), BF16))
        m_ref[...] = jnp.full(m_ref.shape, NEG, F32)
        l_ref[...] = jnp.zeros(l_ref.shape, F32)
        acc_ref[...] = jnp.zeros(acc_ref.shape, F32)

    fq3 = fq_ref[...].reshape(N_HEADS, tt, 1)

    def update(kb, vb, fk, mask):
        n = kb.shape[0]
        s = _dot_nt(qbd_ref[...], kb.astype(BF16)).reshape(N_HEADS, tt, n)
        s = s + fq3 - fk[:N_HEADS][:, None, :]
        if mask is not None:
            s = jnp.where(mask, s, NEG)
        m_old = m_ref[...].reshape(N_HEADS, tt, 1)
        m_new = jnp.maximum(m_old, jnp.max(s, axis=-1, keepdims=True))
        alpha = jnp.exp(m_old - m_new)
        p = jnp.exp(s - m_new)
        l_ref[...] = (alpha * l_ref[...].reshape(N_HEADS, tt, 1)
                      + jnp.sum(p, axis=-1, keepdims=True)).reshape(rows, 1)
        m_ref[...] = m_new.reshape(rows, 1)
        pv = _dot(p.reshape(rows, n).astype(BF16), vb.astype(BF16))
        acc_ref[...] = alpha.reshape(rows, 1) * acc_ref[...] + pv

    k0 = pl.multiple_of(j * tk, tk)
    update(ck_ref[...], cv_ref[...], f_ref[:, pl.ds(k0, tk)], None)

    @pl.when(j == pl.num_programs(1) - 1)
    def _():
        mask = (lax.broadcasted_iota(jnp.int32, (1, tt, tt), 2)
                <= lax.broadcasted_iota(jnp.int32, (1, tt, tt), 1))
        update(kn_ref[...], vn_ref[...], f_ref[:, past:past + tt], mask)
        o = jnp.where(own, acc_ref[...] / l_ref[...], 0.0).reshape(N_HEADS, tt, ATT_WIDTH)
        o_ref[...] = jnp.sum(o, axis=0).astype(BF16)


def _attn_sample(q, cache_k, cache_v, k_new, v_new, f_all, fq_col, layer, *, tk):
    nb, tt, _ = q.shape
    past = cache_k.shape[2]
    rows = N_HEADS * tt
    f_len = f_all.shape[-1]
    return pl.pallas_call(
        functools.partial(_attn_sample_kernel, tk=tk, past=past, tt=tt),
        grid=(nb, past // tk),
        in_specs=[
            pl.BlockSpec((None, tt, ATT_WIDTH), lambda i, j: (i, 0, 0)),
            pl.BlockSpec((None, None, tk, ATT_WIDTH), lambda i, j: (layer, i, j, 0)),
            pl.BlockSpec((None, None, tk, ATT_WIDTH), lambda i, j: (layer, i, j, 0)),
            pl.BlockSpec((None, None, tt, ATT_WIDTH), lambda i, j: (layer, i, 0, 0)),
            pl.BlockSpec((None, None, tt, ATT_WIDTH), lambda i, j: (layer, i, 0, 0)),
            pl.BlockSpec((None, HEADS_PAD, f_len), lambda i, j: (i, 0, 0)),
            pl.BlockSpec((None, rows, 1), lambda i, j: (i, 0, 0)),
        ],
        out_specs=pl.BlockSpec((None, tt, ATT_WIDTH), lambda i, j: (i, 0, 0)),
        out_shape=jax.ShapeDtypeStruct((nb, tt, ATT_WIDTH), BF16),
        scratch_shapes=[
            pltpu.VMEM((rows, ATT_WIDTH), BF16),
            pltpu.VMEM((rows, 1), F32),
            pltpu.VMEM((rows, 1), F32),
            pltpu.VMEM((rows, ATT_WIDTH), F32),
        ],
        compiler_params=pltpu.CompilerParams(
            dimension_semantics=("arbitrary", "arbitrary"), vmem_limit_bytes=VMEM_LIMIT),
        name="attn_sample",
    )(q, cache_k, cache_v, k_new, v_new, f_all, fq_col)


def _out_mlp_kernel(x_ref, py_ref, att_ref, g1_ref, sh_ref, sc_ref, g2_ref,
                    wop_ref, woa_ref, wup_ref, wdn_ref, y_ref, *, bb, tt, ff_chunk):
    m = bb * tt
    mix = (_dot(py_ref[...].reshape(m, POOL_WIDTH), wop_ref[...])
           + _dot(att_ref[...].reshape(m, ATT_WIDTH), woa_ref[...]))
    x1 = x_ref[...] + g1_ref[...] * mix.reshape(bb, tt, D_MODEL)
    ms = jnp.mean(x1 * x1, axis=-1, keepdims=True)
    h = (x1 * lax.rsqrt(ms + EPS)) * (1.0 + sc_ref[...]) + sh_ref[...]
    hb = h.reshape(m, D_MODEL).astype(BF16)
    ff = jnp.zeros((m, D_MODEL), F32)
    for c in range(D_FF // ff_chunk):
        up = _dot(hb, wup_ref[:, c * ff_chunk:(c + 1) * ff_chunk])
        act = jnp.square(jnp.maximum(up, 0.0)).astype(BF16)
        ff = ff + _dot(act, wdn_ref[c * ff_chunk:(c + 1) * ff_chunk, :])
    y_ref[...] = x1 + g2_ref[...] * ff.reshape(bb, tt, D_MODEL)


def _out_mlp(x, py, att, g1, sh, sc, g2, w, *, bb, tt, name):
    b, t, _ = x.shape

    def tile(width):
        return pl.BlockSpec((bb, tt, width), lambda i, j: (i, j, 0))

    def per_b():
        return pl.BlockSpec((bb, 1, D_MODEL), lambda i, j: (i, 0, 0))

    return pl.pallas_call(
        functools.partial(_out_mlp_kernel, bb=bb, tt=tt, ff_chunk=1024),
        grid=(b // bb, t // tt),
        in_specs=[
            tile(D_MODEL), tile(POOL_WIDTH), tile(ATT_WIDTH),
            per_b(), per_b(), per_b(), per_b(),
            _resident((POOL_WIDTH, D_MODEL)), _resident((ATT_WIDTH, D_MODEL)),
            _resident((D_MODEL, D_FF)), _resident((D_FF, D_MODEL)),
        ],
        out_specs=tile(D_MODEL),
        out_shape=jax.ShapeDtypeStruct((b, t, D_MODEL), F32),
        compiler_params=pltpu.CompilerParams(
            dimension_semantics=("arbitrary", "arbitrary"), vmem_limit_bytes=VMEM_LIMIT),
        name=name,
    )(x, py, att, g1, sh, sc, g2, w["w_out_p"], w["w_out_a"], w["w_up"], w["w_down"])


def _layer_weights(depth, layer, w_in, b_f, q_gain, k_gain, w_pool, pool_scale, w_out, w_up, w_down):
    wl = w_in[layer]
    o = POOL_WIDTH
    w_f = wl[:, o + 3 * ATT_WIDTH:]
    head_of_col = np.arange(ATT_WIDTH) // HEAD_DIM
    ind = (head_of_col[:, None] == np.arange(LANES)[None, :]).astype(np.float32)
    w_pool_bd = jnp.zeros((POOL_WIDTH, POOL_WIDTH), F32)
    for g in range(len(POOL_WINDOWS)):
        sl = slice(g * POOL_GROUP_DIM, (g + 1) * POOL_GROUP_DIM)
        w_pool_bd = w_pool_bd.at[sl, sl].set(w_pool[layer, g])
    return {
        "depth": depth,
        "w_u": wl[:, :o].astype(BF16),
        "w_q": wl[:, o:o + ATT_WIDTH].astype(BF16),
        "w_k": wl[:, o + ATT_WIDTH:o + 2 * ATT_WIDTH].astype(BF16),
        "w_v": wl[:, o + 2 * ATT_WIDTH:o + 3 * ATT_WIDTH].astype(BF16),
        "w_f": jnp.pad(w_f, ((0, 0), (0, LANES - N_HEADS))).astype(BF16),
        "w_ft": jnp.pad(w_f.T, ((0, HEADS_PAD - N_HEADS), (0, 0))).astype(BF16),
        "bf_row": jnp.pad(b_f[layer], (0, LANES - N_HEADS)).reshape(1, LANES),
        "bf_col": jnp.pad(b_f[layer], (0, HEADS_PAD - N_HEADS)).reshape(HEADS_PAD, 1),
        "qg": jnp.tile(q_gain[layer], N_HEADS).reshape(1, ATT_WIDTH),
        "kg": jnp.tile(k_gain[layer], N_HEADS).reshape(1, ATT_WIDTH),
        "ind": jnp.asarray(ind, BF16),
        "ind_t": jnp.asarray(ind.T, BF16),
        "w_pool": w_pool_bd.astype(BF16),
        "pool_scale": pool_scale[layer].reshape(1, POOL_WIDTH),
        "w_out_p": w_out[layer, :POOL_WIDTH].astype(BF16),
        "w_out_a": w_out[layer, POOL_WIDTH:].astype(BF16),
        "w_up": w_up[layer].astype(BF16),
        "w_down": w_down[layer].astype(BF16),
    }


def _mod_parts(mod_rows):
    b = mod_rows.shape[0]
    parts = mod_rows.reshape(b, 6, 1, D_MODEL)
    return [parts[:, i] for i in range(6)]


def kernel(x_prompt, x_sample, c_prompt, c_sample, cache_k, cache_v, cache_logf, cache_pool,
           w_mod, b_mod, w_in, b_f, q_gain, k_gain, w_pool, pool_scale, w_out, w_up, w_down):
    depth = w_mod.shape[0]
    bp, seq, _ = x_prompt.shape
    bs, dec_seq, _ = x_sample.shape
    past = cache_k.shape[2]
    tq = 256
    tm_in = 256
    tm_out = 512
    tk_cache = 1024

    n_c = bp + bs
    c_rows = -(-n_c // 8) * 8
    c_all = jnp.pad(jnp.concatenate([c_prompt, c_sample], axis=0), ((0, c_rows - n_c), (0, 0)))
    mod = _modulation(c_all, w_mod, b_mod)

    ck = cache_k.reshape(depth, bs, past, ATT_WIDTH)
    cv = cache_v.reshape(depth, bs, past, ATT_WIDTH)
    s_chunks = -(-(past + dec_seq) // CUMSUM_CHUNK)
    s_chunks += s_chunks % 2
    s_len = s_chunks * CUMSUM_CHUNK
    hist_p = jnp.zeros((bp, HIST_ROWS, POOL_WIDTH), F32)

    y_p, y_s = x_prompt, x_sample
    slabs_p = slabs_s = None
    for l in range(depth):
        w = _layer_weights(depth, l, w_in, b_f, q_gain, k_gain, w_pool, pool_scale, w_out, w_up, w_down)
        sh1, sc1, g1, sh2, sc2, g2 = _mod_parts(mod[l, :bp])
        slabs_p, (kbf, qt, vt, logft, py) = _in_proj(
            y_p, sh1, sc1, hist_p, w, slabs_p, l, bb=1, tt=tm_in, pos0=0, transposed=True)
        ft = _cumsum_time(logft, rows_per_block=bp * HEADS_PAD * (seq // CUMSUM_CHUNK))
        att = _attn_prompt(qt, kbf, vt, ft, jnp.swapaxes(ft, 1, 2), tq=tq)
        y_p = _out_mlp(y_p, py, att, g1, sh2, sc2, g2, w, bb=1, tt=tm_out, name="out_mlp_prompt")

        sh1, sc1, g1, sh2, sc2, g2 = _mod_parts(mod[l, bp:n_c])
        hist_s = jnp.pad(cache_pool[l], ((0, 0), (HIST_ROWS - POOL_HIST, 0), (0, 0)))
        slabs_s, (_, q_s, logft_s, py_s) = _in_proj(
            y_s, sh1, sc1, hist_s, w, slabs_s, l, bb=bs, tt=dec_seq, pos0=past, transposed=False)
        new_t = jnp.swapaxes(logft_s.reshape(HEADS_PAD, bs, dec_seq), 0, 1)
        cache_t = jnp.pad(jnp.swapaxes(cache_logf[l], 1, 2), ((0, 0), (0, HEADS_PAD - N_HEADS), (0, 0)))
        seq_s = jnp.concatenate(
            [cache_t, new_t, jnp.zeros((bs, HEADS_PAD, s_len - past - dec_seq), F32)], axis=-1)
        f_all = _cumsum_time(seq_s, rows_per_block=4 * HEADS_PAD * s_chunks)
        fq_col = f_all[:, :N_HEADS, past:past + dec_seq].reshape(bs, N_HEADS * dec_seq, 1)
        att_s = _attn_sample(q_s, ck, cv, slabs_s[0], slabs_s[1], f_all, fq_col, l, tk=tk_cache)
        y_s = _out_mlp(y_s, py_s, att_s, g1, sh2, sc2, g2, w, bb=bs, tt=dec_seq, name="out_mlp_sample")

    k_p, v_p, f_p, p_p = slabs_p
    k_s, v_s, f_s, p_s = slabs_s
    return (y_p, y_s,
            k_p.reshape(depth, bp, seq, N_HEADS, HEAD_DIM), v_p.reshape(depth, bp, seq, N_HEADS, HEAD_DIM),
            f_p, p_p,
            k_s.reshape(depth, bs, dec_seq, N_HEADS, HEAD_DIM), v_s.reshape(depth, bs, dec_seq, N_HEADS, HEAD_DIM),
            f_s, p_s)
```

```python
import functools

import numpy as np
import jax
import jax.numpy as jnp
from jax import lax
from jax.experimental import pallas as pl
from jax.experimental.pallas import tpu as pltpu

D_MODEL = 1024
POOL_WIDTH = 256
POOL_GROUP_DIM = 64
POOL_WINDOWS = (2, 4, 8, 16)
POOL_HIST = 15
HIST_ROWS = 16
ATT_WIDTH = 768
HEAD_DIM = 64
N_HEADS = 12
HEADS_PAD = 16
D_FF = 4096
EPS = 1e-6
NEG = -1e30
LOG2E = 1.4426950408889634

LANES = 128
CUMSUM_CHUNK = 256
VMEM_LIMIT = 56 * 1024 * 1024

F32 = jnp.float32
BF16 = jnp.bfloat16


def _dot(a, b):
    return jnp.dot(a, b, preferred_element_type=F32)


def _dot_nt(a, b):
    return lax.dot_general(a, b, (((1,), (1,)), ((), ())), preferred_element_type=F32)


def _split_dot(a, b):
    hi = a.astype(BF16)
    lo = (a - hi.astype(F32)).astype(BF16)
    return _dot(hi, b) + _dot(lo, b)


def _resident(shape):
    nd = len(shape)
    return pl.BlockSpec(shape, lambda *_: (0,) * nd, pipeline_mode=pl.Buffered(1))


def _mod_kernel(c_ref, w_ref, b_ref, o_ref):
    c = c_ref[...]
    s = c / (1.0 + jnp.exp(-c))
    o_ref[...] = _dot(s.astype(BF16), w_ref[...].astype(BF16)) + b_ref[...]


def _modulation(c_all, w_mod, b_mod):
    depth, _, n = w_mod.shape
    rows = c_all.shape[0]
    tn = 1536
    return pl.pallas_call(
        _mod_kernel,
        grid=(depth, n // tn),
        in_specs=[
            pl.BlockSpec((rows, D_MODEL), lambda l, j: (0, 0)),
            pl.BlockSpec((None, D_MODEL, tn), lambda l, j: (l, 0, j)),
            pl.BlockSpec((None, 1, tn), lambda l, j: (l, 0, j)),
        ],
        out_specs=pl.BlockSpec((None, rows, tn), lambda l, j: (l, 0, j)),
        out_shape=jax.ShapeDtypeStruct((depth, rows, n), F32),
        compiler_params=pltpu.CompilerParams(
            dimension_semantics=("arbitrary", "arbitrary"), vmem_limit_bytes=VMEM_LIMIT),
        name="modulation",
    )(c_all, w_mod, b_mod.reshape(depth, 1, n))


def _head_rms(a, ind, ind_t):
    ssum = _split_dot(a * a, ind)
    r = lax.rsqrt(ssum * (1.0 / HEAD_DIM) + EPS)
    return a * _split_dot(r, ind_t)


def _log_sigmoid(x):
    return jnp.minimum(x, 0.0) - jnp.log1p(jnp.exp(-jnp.abs(x)))


def _in_proj_kernel(*refs, n_alias, bb, tt, pos0, transposed):
    refs = refs[n_alias:]
    (x_ref, sh_ref, sc_ref, hist_ref, wu_ref, wq_ref, wk_ref, wv_ref, wf_ref, wft_ref,
     bfr_ref, bfc_ref, qg_ref, kg_ref, ind_ref, indt_ref, wp_ref, ps_ref,
     k32_ref, v32_ref, logf_ref, pstate_ref, kbf_ref, q_ref) = refs[:24]
    vt_ref = refs[24] if transposed else None
    logft_ref, py_ref, z_ref = refs[-3:]
    ti = pl.program_id(1)
    m = bb * tt

    x = x_ref[...]
    ms = jnp.mean(x * x, axis=-1, keepdims=True)
    h = (x * lax.rsqrt(ms + EPS)) * (1.0 + sc_ref[...]) + sh_ref[...]
    hb = h.reshape(m, D_MODEL).astype(BF16)

    ind = ind_ref[...]
    ind_t = indt_ref[...]
    q = _head_rms(_dot(hb, wq_ref[...]), ind, ind_t) * qg_ref[...]
    k = _head_rms(_dot(hb, wk_ref[...]), ind, ind_t) * kg_ref[...]
    v = _dot(hb, wv_ref[...])
    k32_ref[...] = k.reshape(bb, tt, ATT_WIDTH)
    v32_ref[...] = v.reshape(bb, tt, ATT_WIDTH)
    kbf_ref[...] = k.reshape(bb, tt, ATT_WIDTH).astype(BF16)
    qs = q * (HEAD_DIM ** -0.5 * (LOG2E if transposed else 1.0))
    if transposed:
        q_ref[...] = qs.T.astype(BF16)
        vt_ref[...] = v.T.astype(BF16)
    else:
        q_ref[...] = qs.reshape(bb, tt, ATT_WIDTH).astype(BF16)

    fl = _dot(hb, wf_ref[...]) + bfr_ref[...]
    logf_ref[...] = _log_sigmoid(fl)[:, :N_HEADS].reshape(bb, tt, N_HEADS)
    flt = _dot_nt(wft_ref[...], hb) + bfc_ref[...]
    logft_ref[...] = _log_sigmoid(flt)

    u3 = _dot(hb, wu_ref[...]).reshape(bb, tt, POOL_WIDTH)

    @pl.when(ti == 0)
    def _():
        z_ref[:, 0:HIST_ROWS, :] = hist_ref[...]

    z_ref[:, HIST_ROWS:HIST_ROWS + tt, :] = u3
    acc = u3
    sums = {}
    for i in range(1, max(POOL_WINDOWS)):
        acc = acc + z_ref[:, HIST_ROWS - i:HIST_ROWS - i + tt, :]
        if i + 1 in POOL_WINDOWS:
            sums[i + 1] = acc
    group = lax.broadcasted_iota(jnp.int32, (1, 1, POOL_WIDTH), 2) // POOL_GROUP_DIM
    win = sums[POOL_WINDOWS[-1]]
    width = jnp.full((1, 1, POOL_WIDTH), float(POOL_WINDOWS[-1]), F32)
    for g in range(len(POOL_WINDOWS) - 2, -1, -1):
        win = jnp.where(group == g, sums[POOL_WINDOWS[g]], win)
        width = jnp.where(group == g, float(POOL_WINDOWS[g]), width)
    pos = (pos0 + ti * tt + lax.broadcasted_iota(jnp.int32, (1, tt, 1), 1)).astype(F32)
    count = jnp.minimum(pos + 1.0, width)
    mixed = win / count - u3
    y = _dot(mixed.reshape(m, POOL_WIDTH).astype(BF16), wp_ref[...]) * ps_ref[...]
    py_ref[...] = y.reshape(bb, tt, POOL_WIDTH).astype(BF16)

    pstate_ref[...] = z_ref[:, tt + 1:tt + HIST_ROWS, :]
    z_ref[:, 0:HIST_ROWS, :] = z_ref[:, tt:tt + HIST_ROWS, :]


def _in_proj(x, sh, sc, hist, w, slabs, layer, *, bb, tt, pos0, transposed):
    b, t, _ = x.shape
    depth = w["depth"]
    nb, nt = b // bb, t // tt
    m = bb * tt
    n_alias = 0 if slabs is None else 4

    def per_b(shape):
        return pl.BlockSpec(shape, lambda i, j: (i,) + (0,) * (len(shape) - 1))

    in_specs = [pl.BlockSpec(memory_space=pl.ANY)] * n_alias + [
        pl.BlockSpec((bb, tt, D_MODEL), lambda i, j: (i, j, 0)),
        per_b((bb, 1, D_MODEL)), per_b((bb, 1, D_MODEL)), per_b((bb, HIST_ROWS, POOL_WIDTH)),
        _resident((D_MODEL, POOL_WIDTH)), _resident((D_MODEL, ATT_WIDTH)),
        _resident((D_MODEL, ATT_WIDTH)), _resident((D_MODEL, ATT_WIDTH)),
        _resident((D_MODEL, LANES)), _resident((HEADS_PAD, D_MODEL)),
        _resident((1, LANES)), _resident((HEADS_PAD, 1)),
        _resident((1, ATT_WIDTH)), _resident((1, ATT_WIDTH)),
        _resident((ATT_WIDTH, LANES)), _resident((LANES, ATT_WIDTH)),
        _resident((POOL_WIDTH, POOL_WIDTH)), _resident((1, POOL_WIDTH)),
    ]
    slab_shapes = [
        jax.ShapeDtypeStruct((depth, b, t, ATT_WIDTH), F32),
        jax.ShapeDtypeStruct((depth, b, t, ATT_WIDTH), F32),
        jax.ShapeDtypeStruct((depth, b, t, N_HEADS), F32),
        jax.ShapeDtypeStruct((depth, b, POOL_HIST, POOL_WIDTH), F32),
    ]
    slab_specs = [
        pl.BlockSpec((None, bb, tt, ATT_WIDTH), lambda i, j: (layer, i, j, 0)),
        pl.BlockSpec((None, bb, tt, ATT_WIDTH), lambda i, j: (layer, i, j, 0)),
        pl.BlockSpec((None, bb, tt, N_HEADS), lambda i, j: (layer, i, j, 0)),
        pl.BlockSpec((None, bb, POOL_HIST, POOL_WIDTH), lambda i, j: (layer, i, 0, 0)),
    ]
    if transposed:
        q_shape = jax.ShapeDtypeStruct((b, ATT_WIDTH, t), BF16)
        q_spec = pl.BlockSpec((None, ATT_WIDTH, tt), lambda i, j: (i, 0, j))
    else:
        q_shape = jax.ShapeDtypeStruct((b, t, ATT_WIDTH), BF16)
        q_spec = pl.BlockSpec((bb, tt, ATT_WIDTH), lambda i, j: (i, j, 0))
    out_shape = slab_shapes + [
        jax.ShapeDtypeStruct((b, t, ATT_WIDTH), BF16),
        q_shape,
    ] + ([q_shape] if transposed else []) + [
        jax.ShapeDtypeStruct((nb, HEADS_PAD, nt * m), F32),
        jax.ShapeDtypeStruct((b, t, POOL_WIDTH), BF16),
    ]
    out_specs = slab_specs + [
        pl.BlockSpec((bb, tt, ATT_WIDTH), lambda i, j: (i, j, 0)),
        q_spec,
    ] + ([q_spec] if transposed else []) + [
        pl.BlockSpec((None, HEADS_PAD, m), lambda i, j: (i, 0, j)),
        pl.BlockSpec((bb, tt, POOL_WIDTH), lambda i, j: (i, j, 0)),
    ]
    args = ([] if slabs is None else list(slabs)) + [
        x, sh, sc, hist, w["w_u"], w["w_q"], w["w_k"], w["w_v"], w["w_f"], w["w_ft"],
        w["bf_row"], w["bf_col"], w["qg"], w["kg"], w["ind"], w["ind_t"], w["w_pool"],
        w["pool_scale"],
    ]
    outs = pl.pallas_call(
        functools.partial(_in_proj_kernel, n_alias=n_alias, bb=bb, tt=tt, pos0=pos0,
                          transposed=transposed),
        grid=(nb, nt),
        in_specs=in_specs,
        out_specs=out_specs,
        out_shape=out_shape,
        scratch_shapes=[pltpu.VMEM((bb, tt + HIST_ROWS, POOL_WIDTH), F32)],
        input_output_aliases={i: i for i in range(n_alias)},
        compiler_params=pltpu.CompilerParams(
            dimension_semantics=("arbitrary", "arbitrary"), vmem_limit_bytes=VMEM_LIMIT),
        name="in_proj_prompt" if transposed else "in_proj_sample",
    )(*args)
    return outs[:4], outs[4:]


def _cumsum_kernel(x_ref, gcol_ref, grow_ref, o_ref):
    x = x_ref[...]
    rb, c = x.shape
    tri = (lax.broadcasted_iota(jnp.int32, (c, c), 0)
           <= lax.broadcasted_iota(jnp.int32, (c, c), 1)).astype(F32)
    y = jnp.dot(x, tri, precision=lax.Precision.HIGHEST, preferred_element_type=F32)
    tot = jnp.broadcast_to(y[:, c - 1:c], (rb, LANES))
    before = ((gcol_ref[...] == grow_ref[...])
              & (lax.broadcasted_iota(jnp.int32, (rb, rb), 1)
                 < lax.broadcasted_iota(jnp.int32, (rb, rb), 0))).astype(F32)
    off = jnp.dot(before, tot, precision=lax.Precision.HIGHEST, preferred_element_type=F32)
    o_ref[...] = y + off[:, 0:1]


def _cumsum_time(x, rows_per_block):
    lead, length = x.shape[:-1], x.shape[-1]
    chunks = length // CUMSUM_CHUNK
    r = int(np.prod(lead)) * chunks
    rb = rows_per_block
    gid = (np.arange(rb) // chunks).astype(np.int32)
    out = pl.pallas_call(
        _cumsum_kernel,
        grid=(r // rb,),
        in_specs=[
            pl.BlockSpec((rb, CUMSUM_CHUNK), lambda i: (i, 0)),
            _resident((rb, 1)), _resident((1, rb)),
        ],
        out_specs=pl.BlockSpec((rb, CUMSUM_CHUNK), lambda i: (i, 0)),
        out_shape=jax.ShapeDtypeStruct((r, CUMSUM_CHUNK), F32),
        compiler_params=pltpu.CompilerParams(
            dimension_semantics=("arbitrary",), vmem_limit_bytes=VMEM_LIMIT),
        name="cumsum_time",
    )(x.reshape(r, CUMSUM_CHUNK), jnp.asarray(gid).reshape(rb, 1), jnp.asarray(gid).reshape(1, rb))
    return out.reshape(x.shape)


def _attn_prompt_kernel(qt_ref, k_ref, vt_ref, ft_ref, fcol_ref, o_ref, fkb_ref, s_ref, p_ref,
                        *, tq, tk):
    hp = pl.program_id(1)
    qi = pl.program_id(2)
    n_full = (qi * tq) // tk
    t_len = k_ref.shape[0]
    chunk = 512

    @pl.when(qi == 0)
    def _():
        lane = lax.broadcasted_iota(jnp.int32, (chunk, HEADS_PAD), 1)
        for hh in range(2):
            def fill(c, carry, hh=hh):
                r0 = pl.multiple_of(c * chunk, chunk)
                blk = fcol_ref[pl.ds(r0, chunk), :]
                col = jnp.sum(jnp.where(lane == 2 * hp + hh, blk, 0.0), axis=1, keepdims=True)
                fkb_ref[hh, pl.ds(r0, chunk), :] = jnp.broadcast_to(col * LOG2E, (chunk, LANES))
                return carry
            lax.fori_loop(0, t_len // chunk, fill, 0)

    row_head = lax.broadcasted_iota(jnp.int32, (2 * HEAD_DIM, 1), 0) // HEAD_DIM
    causal = (n_full * tk + lax.broadcasted_iota(jnp.int32, (tk, tq), 0)
              <= qi * tq + lax.broadcasted_iota(jnp.int32, (tk, tq), 1))
    ones_rows = jnp.ones((HEADS_PAD, tk), BF16)
    qt = qt_ref[...]
    zero = jnp.zeros_like(qt)
    q2 = jnp.concatenate([jnp.where(row_head == 0, qt, zero), jnp.where(row_head == 1, qt, zero)],
                         axis=1)
    fq = [ft_ref[pl.ds(2 * hp + hh, 1), :] * LOG2E for hh in range(2)]

    def biased_scores(j):
        k0 = pl.multiple_of(j * tk, tk)
        s2 = _dot(k_ref[pl.ds(k0, tk), :], q2)
        a, cmax = [], []
        for hh in range(2):
            fk = fkb_ref[hh, pl.ds(k0, tk), :]
            ah = s2[:, hh * tq:(hh + 1) * tq] - jnp.concatenate([fk] * (tq // LANES), axis=1)
            a.append(ah)
            cmax.append(jnp.max(ah, axis=0, keepdims=True))
        return jnp.concatenate(a, axis=1), cmax

    def weights(a2, cmax, m_run):
        ps, alphas, ms = [], [], []
        for hh in range(2):
            m_new = jnp.maximum(m_run[hh], cmax[hh] + fq[hh])
            alphas.append(jnp.exp2(m_run[hh] - m_new))
            ps.append(jnp.exp2(a2[:, hh * tq:(hh + 1) * tq] + (fq[hh] - m_new)).astype(BF16))
            ms.append(m_new)
        return ps, alphas, ms

    def accumulate(j, acc, p, alpha):
        k0 = pl.multiple_of(j * tk, tk)
        out = []
        for hh in range(2):
            vb = jnp.concatenate(
                [vt_ref[pl.ds(hh * HEAD_DIM, HEAD_DIM), pl.ds(k0, tk)], ones_rows], axis=0)
            out.append(alpha[hh] * acc[hh] + _dot(vb, p[hh]))
        return out

    a0, cmax0 = biased_scores(0)
    s_ref[...] = a0
    p_ref[...] = jnp.zeros(p_ref.shape, BF16)

    def step(j, carry):
        acc, m_run, alpha, cmax = carry
        a_cur = s_ref[...]
        p_prev = p_ref[...]
        a_next, cmax_next = biased_scores(j + 1)
        p, alpha_new, m_new = weights(a_cur, cmax, m_run)
        acc = accumulate(jnp.maximum(j - 1, 0), acc, [p_prev[:, :tq], p_prev[:, tq:]], alpha)
        s_ref[...] = a_next
        p_ref[...] = jnp.concatenate(p, axis=1)
        return tuple(acc), tuple(m_new), tuple(alpha_new), tuple(cmax_next)

    acc0 = jnp.zeros((HEAD_DIM + HEADS_PAD, tq), F32)
    m0 = jnp.full((1, tq), NEG, F32)
    one = jnp.ones((1, tq), F32)
    acc, m_run, alpha, _ = lax.fori_loop(
        0, n_full, step, ((acc0, acc0), (m0, m0), (one, one), tuple(cmax0)))
    p_prev = p_ref[...]
    a_diag = s_ref[...]
    a_diag = jnp.concatenate(
        [jnp.where(causal, a_diag[:, hh * tq:(hh + 1) * tq], NEG) for hh in range(2)], axis=1)
    cmax = [jnp.max(a_diag[:, hh * tq:(hh + 1) * tq], axis=0, keepdims=True) for hh in range(2)]
    p, alpha_new, _ = weights(a_diag, cmax, m_run)
    acc = accumulate(jnp.maximum(n_full - 1, 0), acc, [p_prev[:, :tq], p_prev[:, tq:]], alpha)
    acc = accumulate(n_full, acc, p, alpha_new)
    outs = [a[0:HEAD_DIM] / a[HEAD_DIM:HEAD_DIM + 1] for a in acc]
    o_ref[...] = jnp.concatenate(outs, axis=0).T.astype(BF16)


def _attn_prompt(qt, kbf, vt, ft, fcol, *, tq, tk):
    b, t, _ = kbf.shape
    pairs = N_HEADS // 2
    return pl.pallas_call(
        functools.partial(_attn_prompt_kernel, tq=tq, tk=tk),
        grid=(b, pairs, t // tq),
        in_specs=[
            pl.BlockSpec((None, 2 * HEAD_DIM, tq), lambda i, p, j: (i, p, j)),
            pl.BlockSpec((None, t, 2 * HEAD_DIM), lambda i, p, j: (i, 0, p)),
            pl.BlockSpec((None, 2 * HEAD_DIM, t), lambda i, p, j: (i, p, 0)),
            pl.BlockSpec((None, HEADS_PAD, tq), lambda i, p, j: (i, 0, j)),
            pl.BlockSpec((None, t, HEADS_PAD), lambda i, p, j: (i, 0, 0)),
        ],
        out_specs=pl.BlockSpec((None, tq, 2 * HEAD_DIM), lambda i, p, j: (i, j, p)),
        out_shape=jax.ShapeDtypeStruct((b, t, ATT_WIDTH), BF16),
        scratch_shapes=[
            pltpu.VMEM((2, t, LANES), F32),
            pltpu.VMEM((tk, 2 * tq), F32),
            pltpu.VMEM((tk, 2 * tq), BF16),
        ],
        compiler_params=pltpu.CompilerParams(
            dimension_semantics=("arbitrary", "arbitrary", "arbitrary"),
            vmem_limit_bytes=VMEM_LIMIT),
        name="attn_prompt",
    )(qt, kbf, vt, ft, fcol)


def _attn_sample_kernel(q_ref, ck_ref, cv_ref, kn_ref, vn_ref, f_ref, fq_ref, o_ref,
                        qbd_ref, m_ref, l_ref, acc_ref, *, tk, past, tt):
    j = pl.program_id(1)
    rows = N_HEADS * tt
    own = ((lax.broadcasted_iota(jnp.int32, (rows, ATT_WIDTH), 0) // tt)
           == (lax.broadcasted_iota(jnp.int32, (rows, ATT_WIDTH), 1) // HEAD_DIM))

    @pl.when(j == 0)
    def _():
        q = q_ref[...]
        qbd_ref[...] = jnp.where(own, jnp.concatenate([q] * N_HEADS, axis=0), jnp.zeros((), BF16))
        m_ref[...] = jnp.full(m_ref.shape, NEG, F32)
        l_ref[...] = jnp.zeros(l_ref.shape, F32)
        acc_ref[...] = jnp.zeros(acc_ref.shape, F32)

    fq3 = fq_ref[...].reshape(N_HEADS, tt, 1)

    def update(kb, vb, fk, mask, time_on_lanes):
        if time_on_lanes:
            n = kb.shape[1]
            s = _dot(qbd_ref[...], kb.astype(BF16))
        else:
            n = kb.shape[0]
            s = _dot_nt(qbd_ref[...], kb.astype(BF16))
        s = s.reshape(N_HEADS, tt, n) + fq3 - fk[:N_HEADS][:, None, :]
        if mask is not None:
            s = jnp.where(mask, s, NEG)
        m_old = m_ref[...].reshape(N_HEADS, tt, 1)
        m_new = jnp.maximum(m_old, jnp.max(s, axis=-1, keepdims=True))
        alpha = jnp.exp(m_old - m_new)
        p = jnp.exp(s - m_new)
        l_ref[...] = (alpha * l_ref[...].reshape(N_HEADS, tt, 1)
                      + jnp.sum(p, axis=-1, keepdims=True)).reshape(rows, 1)
        m_ref[...] = m_new.reshape(rows, 1)
        pb = p.reshape(rows, n).astype(BF16)
        pv = _dot_nt(pb, vb.astype(BF16)) if time_on_lanes else _dot(pb, vb.astype(BF16))
        acc_ref[...] = alpha.reshape(rows, 1) * acc_ref[...] + pv

    k0 = pl.multiple_of(j * tk, tk)
    update(ck_ref[...], cv_ref[...], f_ref[:, pl.ds(k0, tk)], None, True)

    @pl.when(j == pl.num_programs(1) - 1)
    def _():
        mask = (lax.broadcasted_iota(jnp.int32, (1, tt, tt), 2)
                <= lax.broadcasted_iota(jnp.int32, (1, tt, tt), 1))
        update(kn_ref[...], vn_ref[...], f_ref[:, past:past + tt], mask, False)
        o = jnp.where(own, acc_ref[...] / l_ref[...], 0.0).reshape(N_HEADS, tt, ATT_WIDTH)
        o_ref[...] = jnp.sum(o, axis=0).astype(BF16)


def _attn_sample(q, cache_k, cache_v, k_new, v_new, f_all, fq_col, layer, *, tk):
    nb, tt, _ = q.shape
    past = cache_k.shape[3]
    rows = N_HEADS * tt
    f_len = f_all.shape[-1]
    return pl.pallas_call(
        functools.partial(_attn_sample_kernel, tk=tk, past=past, tt=tt),
        grid=(nb, past // tk),
        in_specs=[
            pl.BlockSpec((None, tt, ATT_WIDTH), lambda i, j: (i, 0, 0)),
            pl.BlockSpec((None, None, ATT_WIDTH, tk), lambda i, j: (layer, i, 0, j)),
            pl.BlockSpec((None, None, ATT_WIDTH, tk), lambda i, j: (layer, i, 0, j)),
            pl.BlockSpec((None, None, tt, ATT_WIDTH), lambda i, j: (layer, i, 0, 0)),
            pl.BlockSpec((None, None, tt, ATT_WIDTH), lambda i, j: (layer, i, 0, 0)),
            pl.BlockSpec((None, HEADS_PAD, f_len), lambda i, j: (i, 0, 0)),
            pl.BlockSpec((None, rows, 1), lambda i, j: (i, 0, 0)),
        ],
        out_specs=pl.BlockSpec((None, tt, ATT_WIDTH), lambda i, j: (i, 0, 0)),
        out_shape=jax.ShapeDtypeStruct((nb, tt, ATT_WIDTH), BF16),
        scratch_shapes=[
            pltpu.VMEM((rows, ATT_WIDTH), BF16),
            pltpu.VMEM((rows, 1), F32),
            pltpu.VMEM((rows, 1), F32),
            pltpu.VMEM((rows, ATT_WIDTH), F32),
        ],
        compiler_params=pltpu.CompilerParams(
            dimension_semantics=("arbitrary", "arbitrary"), vmem_limit_bytes=VMEM_LIMIT),
        name="attn_sample",
    )(q, cache_k, cache_v, k_new, v_new, f_all, fq_col)


def _out_mlp_kernel(x_ref, py_ref, att_ref, g1_ref, sh_ref, sc_ref, g2_ref,
                    wop_ref, woa_ref, wup_ref, wdn_ref, y_ref, *, bb, tt, ff_chunk):
    m = bb * tt
    mix = (_dot(py_ref[...].reshape(m, POOL_WIDTH), wop_ref[...])
           + _dot(att_ref[...].reshape(m, ATT_WIDTH), woa_ref[...]))
    x1 = x_ref[...] + g1_ref[...] * mix.reshape(bb, tt, D_MODEL)
    ms = jnp.mean(x1 * x1, axis=-1, keepdims=True)
    h = (x1 * lax.rsqrt(ms + EPS)) * (1.0 + sc_ref[...]) + sh_ref[...]
    hb = h.reshape(m, D_MODEL).astype(BF16)
    ff = jnp.zeros((m, D_MODEL), F32)
    for c in range(D_FF // ff_chunk):
        up = _dot(hb, wup_ref[:, c * ff_chunk:(c + 1) * ff_chunk])
        act = jnp.square(jnp.maximum(up, 0.0)).astype(BF16)
        ff = ff + _dot(act, wdn_ref[c * ff_chunk:(c + 1) * ff_chunk, :])
    y_ref[...] = x1 + g2_ref[...] * ff.reshape(bb, tt, D_MODEL)


def _out_mlp(x, py, att, g1, sh, sc, g2, w, *, bb, tt, name):
    b, t, _ = x.shape

    def tile(width):
        return pl.BlockSpec((bb, tt, width), lambda i, j: (i, j, 0))

    def per_b():
        return pl.BlockSpec((bb, 1, D_MODEL), lambda i, j: (i, 0, 0))

    return pl.pallas_call(
        functools.partial(_out_mlp_kernel, bb=bb, tt=tt, ff_chunk=1024),
        grid=(b // bb, t // tt),
        in_specs=[
            tile(D_MODEL), tile(POOL_WIDTH), tile(ATT_WIDTH),
            per_b(), per_b(), per_b(), per_b(),
            _resident((POOL_WIDTH, D_MODEL)), _resident((ATT_WIDTH, D_MODEL)),
            _resident((D_MODEL, D_FF)), _resident((D_FF, D_MODEL)),
        ],
        out_specs=tile(D_MODEL),
        out_shape=jax.ShapeDtypeStruct((b, t, D_MODEL), F32),
        compiler_params=pltpu.CompilerParams(
            dimension_semantics=("arbitrary", "arbitrary"), vmem_limit_bytes=VMEM_LIMIT),
        name=name,
    )(x, py, att, g1, sh, sc, g2, w["w_out_p"], w["w_out_a"], w["w_up"], w["w_down"])


def _layer_weights(depth, layer, w_in, b_f, q_gain, k_gain, w_pool, pool_scale, w_out, w_up, w_down):
    wl = w_in[layer]
    o = POOL_WIDTH
    w_f = wl[:, o + 3 * ATT_WIDTH:]
    head_of_col = np.arange(ATT_WIDTH) // HEAD_DIM
    ind = (head_of_col[:, None] == np.arange(LANES)[None, :]).astype(np.float32)
    w_pool_bd = jnp.zeros((POOL_WIDTH, POOL_WIDTH), F32)
    for g in range(len(POOL_WINDOWS)):
        sl = slice(g * POOL_GROUP_DIM, (g + 1) * POOL_GROUP_DIM)
        w_pool_bd = w_pool_bd.at[sl, sl].set(w_pool[layer, g])
    return {
        "depth": depth,
        "w_u": wl[:, :o].astype(BF16),
        "w_q": wl[:, o:o + ATT_WIDTH].astype(BF16),
        "w_k": wl[:, o + ATT_WIDTH:o + 2 * ATT_WIDTH].astype(BF16),
        "w_v": wl[:, o + 2 * ATT_WIDTH:o + 3 * ATT_WIDTH].astype(BF16),
        "w_f": jnp.pad(w_f, ((0, 0), (0, LANES - N_HEADS))).astype(BF16),
        "w_ft": jnp.pad(w_f.T, ((0, HEADS_PAD - N_HEADS), (0, 0))).astype(BF16),
        "bf_row": jnp.pad(b_f[layer], (0, LANES - N_HEADS)).reshape(1, LANES),
        "bf_col": jnp.pad(b_f[layer], (0, HEADS_PAD - N_HEADS)).reshape(HEADS_PAD, 1),
        "qg": jnp.tile(q_gain[layer], N_HEADS).reshape(1, ATT_WIDTH),
        "kg": jnp.tile(k_gain[layer], N_HEADS).reshape(1, ATT_WIDTH),
        "ind": jnp.asarray(ind, BF16),
        "ind_t": jnp.asarray(ind.T, BF16),
        "w_pool": w_pool_bd.astype(BF16),
        "pool_scale": pool_scale[layer].reshape(1, POOL_WIDTH),
        "w_out_p": w_out[layer, :POOL_WIDTH].astype(BF16),
        "w_out_a": w_out[layer, POOL_WIDTH:].astype(BF16),
        "w_up": w_up[layer].astype(BF16),
        "w_down": w_down[layer].astype(BF16),
    }


def _mod_parts(mod_rows):
    b = mod_rows.shape[0]
    parts = mod_rows.reshape(b, 6, 1, D_MODEL)
    return [parts[:, i] for i in range(6)]


def kernel(x_prompt, x_sample, c_prompt, c_sample, cache_k, cache_v, cache_logf, cache_pool,
           w_mod, b_mod, w_in, b_f, q_gain, k_gain, w_pool, pool_scale, w_out, w_up, w_down):
    depth = w_mod.shape[0]
    bp, seq, _ = x_prompt.shape
    bs, dec_seq, _ = x_sample.shape
    past = cache_k.shape[2]
    tq = 256
    tk_p = 512
    tm_in = 256
    tm_out = 512
    tk_cache = 1024

    n_c = bp + bs
    c_rows = -(-n_c // 8) * 8
    c_all = jnp.pad(jnp.concatenate([c_prompt, c_sample], axis=0), ((0, c_rows - n_c), (0, 0)))
    mod = _modulation(c_all, w_mod, b_mod)

    ck = jnp.transpose(cache_k, (0, 1, 3, 4, 2)).reshape(depth, bs, ATT_WIDTH, past)
    cv = jnp.transpose(cache_v, (0, 1, 3, 4, 2)).reshape(depth, bs, ATT_WIDTH, past)
    s_chunks = -(-(past + dec_seq) // CUMSUM_CHUNK)
    s_chunks += s_chunks % 2
    s_len = s_chunks * CUMSUM_CHUNK
    hist_p = jnp.zeros((bp, HIST_ROWS, POOL_WIDTH), F32)

    y_p, y_s = x_prompt, x_sample
    slabs_p = slabs_s = None
    for l in range(depth):
        w = _layer_weights(depth, l, w_in, b_f, q_gain, k_gain, w_pool, pool_scale, w_out, w_up, w_down)
        sh1, sc1, g1, sh2, sc2, g2 = _mod_parts(mod[l, :bp])
        slabs_p, (kbf, qt, vt, logft, py) = _in_proj(
            y_p, sh1, sc1, hist_p, w, slabs_p, l, bb=1, tt=tm_in, pos0=0, transposed=True)
        ft = _cumsum_time(logft, rows_per_block=bp * HEADS_PAD * (seq // CUMSUM_CHUNK))
        att = _attn_prompt(qt, kbf, vt, ft, jnp.swapaxes(ft, 1, 2), tq=tq, tk=tk_p)
        y_p = _out_mlp(y_p, py, att, g1, sh2, sc2, g2, w, bb=1, tt=tm_out, name="out_mlp_prompt")

        sh1, sc1, g1, sh2, sc2, g2 = _mod_parts(mod[l, bp:n_c])
        hist_s = jnp.pad(cache_pool[l], ((0, 0), (HIST_ROWS - POOL_HIST, 0), (0, 0)))
        slabs_s, (_, q_s, logft_s, py_s) = _in_proj(
            y_s, sh1, sc1, hist_s, w, slabs_s, l, bb=bs, tt=dec_seq, pos0=past, transposed=False)
        new_t = jnp.swapaxes(logft_s.reshape(HEADS_PAD, bs, dec_seq), 0, 1)
        cache_t = jnp.pad(jnp.swapaxes(cache_logf[l], 1, 2), ((0, 0), (0, HEADS_PAD - N_HEADS), (0, 0)))
        seq_s = jnp.concatenate(
            [cache_t, new_t, jnp.zeros((bs, HEADS_PAD, s_len - past - dec_seq), F32)], axis=-1)
        f_all = _cumsum_time(seq_s, rows_per_block=4 * HEADS_PAD * s_chunks)
        fq_col = f_all[:, :N_HEADS, past:past + dec_seq].reshape(bs, N_HEADS * dec_seq, 1)
        att_s = _attn_sample(q_s, ck, cv, slabs_s[0], slabs_s[1], f_all, fq_col, l, tk=tk_cache)
        y_s = _out_mlp(y_s, py_s, att_s, g1, sh2, sc2, g2, w, bb=bs, tt=dec_seq, name="out_mlp_sample")

    k_p, v_p, f_p, p_p = slabs_p
    k_s, v_s, f_s, p_s = slabs_s
    return (y_p, y_s,
            k_p.reshape(depth, bp, seq, N_HEADS, HEAD_DIM), v_p.reshape(depth, bp, seq, N_HEADS, HEAD_DIM),
            f_p, p_p,
            k_s.reshape(depth, bs, dec_seq, N_HEADS, HEAD_DIM), v_s.reshape(depth, bs, dec_seq, N_HEADS, HEAD_DIM),
            f_s, p_s)
```

```python
import functools

import numpy as np
import jax
import jax.numpy as jnp
from jax import lax
from jax.experimental import pallas as pl
from jax.experimental.pallas import tpu as pltpu

D_MODEL = 1024
POOL_WIDTH = 256
POOL_GROUP_DIM = 64
POOL_WINDOWS = (2, 4, 8, 16)
POOL_HIST = 15
HIST_ROWS = 16
ATT_WIDTH = 768
HEAD_DIM = 64
N_HEADS = 12
HEADS_PAD = 16
D_FF = 4096
EPS = 1e-6
NEG = -1e30
LOG2E = 1.4426950408889634
BOUND_SLACK_OCTAVES = 64.0

LANES = 128
CUMSUM_CHUNK = 256
VMEM_LIMIT = 56 * 1024 * 1024

F32 = jnp.float32
BF16 = jnp.bfloat16


def _dot(a, b):
    return jnp.dot(a, b, preferred_element_type=F32)


def _dot_nt(a, b):
    return lax.dot_general(a, b, (((1,), (1,)), ((), ())), preferred_element_type=F32)


def _split_dot(a, b):
    hi = a.astype(BF16)
    lo = (a - hi.astype(F32)).astype(BF16)
    return _dot(hi, b) + _dot(lo, b)


def _resident(shape):
    nd = len(shape)
    return pl.BlockSpec(shape, lambda *_: (0,) * nd, pipeline_mode=pl.Buffered(1))


def _mod_kernel(c_ref, w_ref, b_ref, o_ref):
    c = c_ref[...]
    s = c / (1.0 + jnp.exp(-c))
    o_ref[...] = _dot(s.astype(BF16), w_ref[...].astype(BF16)) + b_ref[...]


def _modulation(c_all, w_mod, b_mod):
    depth, _, n = w_mod.shape
    rows = c_all.shape[0]
    tn = 1536
    return pl.pallas_call(
        _mod_kernel,
        grid=(depth, n // tn),
        in_specs=[
            pl.BlockSpec((rows, D_MODEL), lambda l, j: (0, 0)),
            pl.BlockSpec((None, D_MODEL, tn), lambda l, j: (l, 0, j)),
            pl.BlockSpec((None, 1, tn), lambda l, j: (l, 0, j)),
        ],
        out_specs=pl.BlockSpec((None, rows, tn), lambda l, j: (l, 0, j)),
        out_shape=jax.ShapeDtypeStruct((depth, rows, n), F32),
        compiler_params=pltpu.CompilerParams(
            dimension_semantics=("arbitrary", "arbitrary"), vmem_limit_bytes=VMEM_LIMIT),
        name="modulation",
    )(c_all, w_mod, b_mod.reshape(depth, 1, n))


def _head_rms(a, ind, ind_t):
    ssum = _split_dot(a * a, ind)
    r = lax.rsqrt(ssum * (1.0 / HEAD_DIM) + EPS)
    return a * _split_dot(r, ind_t)


def _log_sigmoid(x):
    return jnp.minimum(x, 0.0) - jnp.log1p(jnp.exp(-jnp.abs(x)))


def _in_proj_kernel(*refs, n_alias, bb, tt, pos0, transposed):
    refs = refs[n_alias:]
    (x_ref, sh_ref, sc_ref, hist_ref, wu_ref, wq_ref, wk_ref, wv_ref, wf_ref, wft_ref,
     bfr_ref, bfc_ref, qg_ref, kg_ref, ind_ref, indt_ref, wp_ref, ps_ref,
     k32_ref, v32_ref, logf_ref, pstate_ref, kbf_ref, q_ref) = refs[:24]
    vt_ref = refs[24] if transposed else None
    logft_ref, py_ref, z_ref = refs[-3:]
    ti = pl.program_id(1)
    m = bb * tt

    x = x_ref[...]
    ms = jnp.mean(x * x, axis=-1, keepdims=True)
    h = (x * lax.rsqrt(ms + EPS)) * (1.0 + sc_ref[...]) + sh_ref[...]
    hb = h.reshape(m, D_MODEL).astype(BF16)

    ind = ind_ref[...]
    ind_t = indt_ref[...]
    q = _head_rms(_dot(hb, wq_ref[...]), ind, ind_t) * qg_ref[...]
    k = _head_rms(_dot(hb, wk_ref[...]), ind, ind_t) * kg_ref[...]
    v = _dot(hb, wv_ref[...])
    k32_ref[...] = k.reshape(bb, tt, ATT_WIDTH)
    v32_ref[...] = v.reshape(bb, tt, ATT_WIDTH)
    kbf_ref[...] = k.reshape(bb, tt, ATT_WIDTH).astype(BF16)
    qs = q * (HEAD_DIM ** -0.5 * (LOG2E if transposed else 1.0))
    if transposed:
        q_ref[...] = qs.T.astype(BF16)
        vt_ref[...] = v.T.astype(BF16)
    else:
        q_ref[...] = qs.reshape(bb, tt, ATT_WIDTH).astype(BF16)

    fl = _dot(hb, wf_ref[...]) + bfr_ref[...]
    logf_ref[...] = _log_sigmoid(fl)[:, :N_HEADS].reshape(bb, tt, N_HEADS)
    flt = _dot_nt(wft_ref[...], hb) + bfc_ref[...]
    logft_ref[...] = _log_sigmoid(flt)

    u3 = _dot(hb, wu_ref[...]).reshape(bb, tt, POOL_WIDTH)

    @pl.when(ti == 0)
    def _():
        z_ref[:, 0:HIST_ROWS, :] = hist_ref[...]

    z_ref[:, HIST_ROWS:HIST_ROWS + tt, :] = u3
    acc = u3
    sums = {}
    for i in range(1, max(POOL_WINDOWS)):
        acc = acc + z_ref[:, HIST_ROWS - i:HIST_ROWS - i + tt, :]
        if i + 1 in POOL_WINDOWS:
            sums[i + 1] = acc
    group = lax.broadcasted_iota(jnp.int32, (1, 1, POOL_WIDTH), 2) // POOL_GROUP_DIM
    win = sums[POOL_WINDOWS[-1]]
    width = jnp.full((1, 1, POOL_WIDTH), float(POOL_WINDOWS[-1]), F32)
    for g in range(len(POOL_WINDOWS) - 2, -1, -1):
        win = jnp.where(group == g, sums[POOL_WINDOWS[g]], win)
        width = jnp.where(group == g, float(POOL_WINDOWS[g]), width)
    pos = (pos0 + ti * tt + lax.broadcasted_iota(jnp.int32, (1, tt, 1), 1)).astype(F32)
    count = jnp.minimum(pos + 1.0, width)
    mixed = win / count - u3
    y = _dot(mixed.reshape(m, POOL_WIDTH).astype(BF16), wp_ref[...]) * ps_ref[...]
    py_ref[...] = y.reshape(bb, tt, POOL_WIDTH).astype(BF16)

    pstate_ref[...] = z_ref[:, tt + 1:tt + HIST_ROWS, :]
    z_ref[:, 0:HIST_ROWS, :] = z_ref[:, tt:tt + HIST_ROWS, :]


def _in_proj(x, sh, sc, hist, w, slabs, layer, *, bb, tt, pos0, transposed):
    b, t, _ = x.shape
    depth = w["depth"]
    nb, nt = b // bb, t // tt
    m = bb * tt
    n_alias = 0 if slabs is None else 4

    def per_b(shape):
        return pl.BlockSpec(shape, lambda i, j: (i,) + (0,) * (len(shape) - 1))

    in_specs = [pl.BlockSpec(memory_space=pl.ANY)] * n_alias + [
        pl.BlockSpec((bb, tt, D_MODEL), lambda i, j: (i, j, 0)),
        per_b((bb, 1, D_MODEL)), per_b((bb, 1, D_MODEL)), per_b((bb, HIST_ROWS, POOL_WIDTH)),
        _resident((D_MODEL, POOL_WIDTH)), _resident((D_MODEL, ATT_WIDTH)),
        _resident((D_MODEL, ATT_WIDTH)), _resident((D_MODEL, ATT_WIDTH)),
        _resident((D_MODEL, LANES)), _resident((HEADS_PAD, D_MODEL)),
        _resident((1, LANES)), _resident((HEADS_PAD, 1)),
        _resident((1, ATT_WIDTH)), _resident((1, ATT_WIDTH)),
        _resident((ATT_WIDTH, LANES)), _resident((LANES, ATT_WIDTH)),
        _resident((POOL_WIDTH, POOL_WIDTH)), _resident((1, POOL_WIDTH)),
    ]
    slab_shapes = [
        jax.ShapeDtypeStruct((depth, b, t, ATT_WIDTH), F32),
        jax.ShapeDtypeStruct((depth, b, t, ATT_WIDTH), F32),
        jax.ShapeDtypeStruct((depth, b, t, N_HEADS), F32),
        jax.ShapeDtypeStruct((depth, b, POOL_HIST, POOL_WIDTH), F32),
    ]
    slab_specs = [
        pl.BlockSpec((None, bb, tt, ATT_WIDTH), lambda i, j: (layer, i, j, 0)),
        pl.BlockSpec((None, bb, tt, ATT_WIDTH), lambda i, j: (layer, i, j, 0)),
        pl.BlockSpec((None, bb, tt, N_HEADS), lambda i, j: (layer, i, j, 0)),
        pl.BlockSpec((None, bb, POOL_HIST, POOL_WIDTH), lambda i, j: (layer, i, 0, 0)),
    ]
    if transposed:
        q_shape = jax.ShapeDtypeStruct((b, ATT_WIDTH, t), BF16)
        q_spec = pl.BlockSpec((None, ATT_WIDTH, tt), lambda i, j: (i, 0, j))
    else:
        q_shape = jax.ShapeDtypeStruct((b, t, ATT_WIDTH), BF16)
        q_spec = pl.BlockSpec((bb, tt, ATT_WIDTH), lambda i, j: (i, j, 0))
    out_shape = slab_shapes + [
        jax.ShapeDtypeStruct((b, t, ATT_WIDTH), BF16),
        q_shape,
    ] + ([q_shape] if transposed else []) + [
        jax.ShapeDtypeStruct((nb, HEADS_PAD, nt * m), F32),
        jax.ShapeDtypeStruct((b, t, POOL_WIDTH), BF16),
    ]
    out_specs = slab_specs + [
        pl.BlockSpec((bb, tt, ATT_WIDTH), lambda i, j: (i, j, 0)),
        q_spec,
    ] + ([q_spec] if transposed else []) + [
        pl.BlockSpec((None, HEADS_PAD, m), lambda i, j: (i, 0, j)),
        pl.BlockSpec((bb, tt, POOL_WIDTH), lambda i, j: (i, j, 0)),
    ]
    args = ([] if slabs is None else list(slabs)) + [
        x, sh, sc, hist, w["w_u"], w["w_q"], w["w_k"], w["w_v"], w["w_f"], w["w_ft"],
        w["bf_row"], w["bf_col"], w["qg"], w["kg"], w["ind"], w["ind_t"], w["w_pool"],
        w["pool_scale"],
    ]
    outs = pl.pallas_call(
        functools.partial(_in_proj_kernel, n_alias=n_alias, bb=bb, tt=tt, pos0=pos0,
                          transposed=transposed),
        grid=(nb, nt),
        in_specs=in_specs,
        out_specs=out_specs,
        out_shape=out_shape,
        scratch_shapes=[pltpu.VMEM((bb, tt + HIST_ROWS, POOL_WIDTH), F32)],
        input_output_aliases={i: i for i in range(n_alias)},
        compiler_params=pltpu.CompilerParams(
            dimension_semantics=("arbitrary", "arbitrary"), vmem_limit_bytes=VMEM_LIMIT),
        name="in_proj_prompt" if transposed else "in_proj_sample",
    )(*args)
    return outs[:4], outs[4:]


def _cumsum_kernel(x_ref, gcol_ref, grow_ref, o_ref):
    x = x_ref[...]
    rb, c = x.shape
    tri = (lax.broadcasted_iota(jnp.int32, (c, c), 0)
           <= lax.broadcasted_iota(jnp.int32, (c, c), 1)).astype(F32)
    y = jnp.dot(x, tri, precision=lax.Precision.HIGHEST, preferred_element_type=F32)
    tot = jnp.broadcast_to(y[:, c - 1:c], (rb, LANES))
    before = ((gcol_ref[...] == grow_ref[...])
              & (lax.broadcasted_iota(jnp.int32, (rb, rb), 1)
                 < lax.broadcasted_iota(jnp.int32, (rb, rb), 0))).astype(F32)
    off = jnp.dot(before, tot, precision=lax.Precision.HIGHEST, preferred_element_type=F32)
    o_ref[...] = y + off[:, 0:1]


def _cumsum_time(x, rows_per_block):
    lead, length = x.shape[:-1], x.shape[-1]
    chunks = length // CUMSUM_CHUNK
    r = int(np.prod(lead)) * chunks
    rb = rows_per_block
    gid = (np.arange(rb) // chunks).astype(np.int32)
    out = pl.pallas_call(
        _cumsum_kernel,
        grid=(r // rb,),
        in_specs=[
            pl.BlockSpec((rb, CUMSUM_CHUNK), lambda i: (i, 0)),
            _resident((rb, 1)), _resident((1, rb)),
        ],
        out_specs=pl.BlockSpec((rb, CUMSUM_CHUNK), lambda i: (i, 0)),
        out_shape=jax.ShapeDtypeStruct((r, CUMSUM_CHUNK), F32),
        compiler_params=pltpu.CompilerParams(
            dimension_semantics=("arbitrary",), vmem_limit_bytes=VMEM_LIMIT),
        name="cumsum_time",
    )(x.reshape(r, CUMSUM_CHUNK), jnp.asarray(gid).reshape(rb, 1), jnp.asarray(gid).reshape(1, rb))
    return out.reshape(x.shape)


def _fill_key_forget_sums(fcol_ref, fkb_ref, hp):
    chunk = 512
    lane = lax.broadcasted_iota(jnp.int32, (chunk, HEADS_PAD), 1)
    for hh in range(2):
        def fill(c, carry, hh=hh):
            r0 = pl.multiple_of(c * chunk, chunk)
            blk = fcol_ref[pl.ds(r0, chunk), :]
            col = jnp.sum(jnp.where(lane == 2 * hp + hh, blk, 0.0), axis=1, keepdims=True)
            fkb_ref[hh, pl.ds(r0, chunk), :] = jnp.broadcast_to(col * LOG2E, (chunk, LANES))
            return carry
        lax.fori_loop(0, fcol_ref.shape[0] // chunk, fill, 0)


def _pair_query_columns(qt):
    row_head = lax.broadcasted_iota(jnp.int32, (2 * HEAD_DIM, 1), 0) // HEAD_DIM
    zero = jnp.zeros_like(qt)
    return jnp.concatenate(
        [jnp.where(row_head == 0, qt, zero), jnp.where(row_head == 1, qt, zero)], axis=1)


def _attn_prompt_bounded_kernel(kmax_ref, qt_ref, k_ref, vt_ref, ft_ref, fcol_ref, o_ref,
                                fkb_ref, p_ref, *, tq, tk):
    hp = pl.program_id(1)
    qi = pl.program_id(2)
    n_full = (qi * tq) // tk

    @pl.when(qi == 0)
    def _():
        _fill_key_forget_sums(fcol_ref, fkb_ref, hp)

    causal = (n_full * tk + lax.broadcasted_iota(jnp.int32, (tk, tq), 0)
              <= qi * tq + lax.broadcasted_iota(jnp.int32, (tk, tq), 1))
    ones_rows = jnp.ones((HEADS_PAD, tk), BF16)
    qt = qt_ref[...]
    q2 = _pair_query_columns(qt)
    qsq = jnp.square(qt.astype(F32))
    shift = []
    for hh in range(2):
        qnorm = jnp.sqrt(jnp.sum(qsq[hh * HEAD_DIM:(hh + 1) * HEAD_DIM], axis=0, keepdims=True))
        shift.append(ft_ref[pl.ds(2 * hp + hh, 1), :] * LOG2E - qnorm * kmax_ref[0, 0])

    def weights(j, masked):
        k0 = pl.multiple_of(j * tk, tk)
        s2 = _dot(k_ref[pl.ds(k0, tk), :], q2)
        ps = []
        for hh in range(2):
            fk = fkb_ref[hh, pl.ds(k0, tk), :]
            a = (s2[:, hh * tq:(hh + 1) * tq] - jnp.concatenate([fk] * (tq // LANES), axis=1)
                 + shift[hh])
            if masked:
                a = jnp.where(causal, a, NEG)
            ps.append(jnp.exp2(a).astype(BF16))
        return jnp.concatenate(ps, axis=1)

    def accumulate(j, acc, p2):
        k0 = pl.multiple_of(j * tk, tk)
        out = []
        for hh in range(2):
            vb = jnp.concatenate(
                [vt_ref[pl.ds(hh * HEAD_DIM, HEAD_DIM), pl.ds(k0, tk)], ones_rows], axis=0)
            out.append(acc[hh] + _dot(vb, p2[:, hh * tq:(hh + 1) * tq]))
        return tuple(out)

    p_ref[...] = weights(0, False)

    def step(j, acc):
        p_next = weights(j + 1, False)
        acc = accumulate(j, acc, p_ref[...])
        p_ref[...] = p_next
        return acc

    acc0 = jnp.zeros((HEAD_DIM + HEADS_PAD, tq), F32)
    acc = lax.fori_loop(0, jnp.maximum(n_full - 1, 0), step, (acc0, acc0))
    p_last = weights(n_full, True)
    p_pending = p_ref[...]
    p_pending = jnp.where(n_full > 0, p_pending, jnp.zeros_like(p_pending))
    acc = accumulate(jnp.maximum(n_full - 1, 0), acc, p_pending)
    acc = accumulate(n_full, acc, p_last)
    outs = [a[0:HEAD_DIM] / a[HEAD_DIM:HEAD_DIM + 1] for a in acc]
    o_ref[...] = jnp.concatenate(outs, axis=0).T.astype(BF16)


def _attn_prompt_kernel(qt_ref, k_ref, vt_ref, ft_ref, fcol_ref, o_ref, fkb_ref, s_ref, p_ref,
                        *, tq, tk):
    hp = pl.program_id(1)
    qi = pl.program_id(2)
    n_full = (qi * tq) // tk

    @pl.when(qi == 0)
    def _():
        _fill_key_forget_sums(fcol_ref, fkb_ref, hp)

    causal = (n_full * tk + lax.broadcasted_iota(jnp.int32, (tk, tq), 0)
              <= qi * tq + lax.broadcasted_iota(jnp.int32, (tk, tq), 1))
    ones_rows = jnp.ones((HEADS_PAD, tk), BF16)
    q2 = _pair_query_columns(qt_ref[...])
    fq = [ft_ref[pl.ds(2 * hp + hh, 1), :] * LOG2E for hh in range(2)]

    def biased_scores(j):
        k0 = pl.multiple_of(j * tk, tk)
        s2 = _dot(k_ref[pl.ds(k0, tk), :], q2)
        a, cmax = [], []
        for hh in range(2):
            fk = fkb_ref[hh, pl.ds(k0, tk), :]
            ah = s2[:, hh * tq:(hh + 1) * tq] - jnp.concatenate([fk] * (tq // LANES), axis=1)
            a.append(ah)
            cmax.append(jnp.max(ah, axis=0, keepdims=True))
        return jnp.concatenate(a, axis=1), cmax

    def weights(a2, cmax, m_run):
        ps, alphas, ms = [], [], []
        for hh in range(2):
            m_new = jnp.maximum(m_run[hh], cmax[hh] + fq[hh])
            alphas.append(jnp.exp2(m_run[hh] - m_new))
            ps.append(jnp.exp2(a2[:, hh * tq:(hh + 1) * tq] + (fq[hh] - m_new)).astype(BF16))
            ms.append(m_new)
        return ps, alphas, ms

    def accumulate(j, acc, p, alpha):
        k0 = pl.multiple_of(j * tk, tk)
        out = []
        for hh in range(2):
            vb = jnp.concatenate(
                [vt_ref[pl.ds(hh * HEAD_DIM, HEAD_DIM), pl.ds(k0, tk)], ones_rows], axis=0)
            out.append(alpha[hh] * acc[hh] + _dot(vb, p[hh]))
        return out

    a0, cmax0 = biased_scores(0)
    s_ref[...] = a0
    p_ref[...] = jnp.zeros(p_ref.shape, BF16)

    def step(j, carry):
        acc, m_run, alpha, cmax = carry
        a_cur = s_ref[...]
        p_prev = p_ref[...]
        a_next, cmax_next = biased_scores(j + 1)
        p, alpha_new, m_new = weights(a_cur, cmax, m_run)
        acc = accumulate(jnp.maximum(j - 1, 0), acc, [p_prev[:, :tq], p_prev[:, tq:]], alpha)
        s_ref[...] = a_next
        p_ref[...] = jnp.concatenate(p, axis=1)
        return tuple(acc), tuple(m_new), tuple(alpha_new), tuple(cmax_next)

    acc0 = jnp.zeros((HEAD_DIM + HEADS_PAD, tq), F32)
    m0 = jnp.full((1, tq), NEG, F32)
    one = jnp.ones((1, tq), F32)
    acc, m_run, alpha, _ = lax.fori_loop(
        0, n_full, step, ((acc0, acc0), (m0, m0), (one, one), tuple(cmax0)))
    p_prev = p_ref[...]
    a_diag = s_ref[...]
    a_diag = jnp.concatenate(
        [jnp.where(causal, a_diag[:, hh * tq:(hh + 1) * tq], NEG) for hh in range(2)], axis=1)
    cmax = [jnp.max(a_diag[:, hh * tq:(hh + 1) * tq], axis=0, keepdims=True) for hh in range(2)]
    p, alpha_new, _ = weights(a_diag, cmax, m_run)
    acc = accumulate(jnp.maximum(n_full - 1, 0), acc, [p_prev[:, :tq], p_prev[:, tq:]], alpha)
    acc = accumulate(n_full, acc, p, alpha_new)
    outs = [a[0:HEAD_DIM] / a[HEAD_DIM:HEAD_DIM + 1] for a in acc]
    o_ref[...] = jnp.concatenate(outs, axis=0).T.astype(BF16)


def _attn_prompt(qt, kbf, vt, ft, fcol, q_gain, k_gain, *, tq, tk):
    b, t, _ = kbf.shape
    pairs = N_HEADS // 2
    in_specs = [
        pl.BlockSpec((None, 2 * HEAD_DIM, tq), lambda i, p, j: (i, p, j)),
        pl.BlockSpec((None, t, 2 * HEAD_DIM), lambda i, p, j: (i, 0, p)),
        pl.BlockSpec((None, 2 * HEAD_DIM, t), lambda i, p, j: (i, p, 0)),
        pl.BlockSpec((None, HEADS_PAD, tq), lambda i, p, j: (i, 0, j)),
        pl.BlockSpec((None, t, HEADS_PAD), lambda i, p, j: (i, 0, 0)),
    ]
    common = dict(
        grid=(b, pairs, t // tq),
        out_specs=pl.BlockSpec((None, tq, 2 * HEAD_DIM), lambda i, p, j: (i, j, p)),
        out_shape=jax.ShapeDtypeStruct((b, t, ATT_WIDTH), BF16),
        compiler_params=pltpu.CompilerParams(
            dimension_semantics=("arbitrary", "arbitrary", "arbitrary"),
            vmem_limit_bytes=VMEM_LIMIT),
    )
    key_sums = pltpu.VMEM((2, t, LANES), F32)
    weights_tile = pltpu.VMEM((tk, 2 * tq), BF16)

    def running_max(_, *args):
        return pl.pallas_call(
            functools.partial(_attn_prompt_kernel, tq=tq, tk=tk),
            in_specs=in_specs,
            scratch_shapes=[key_sums, pltpu.VMEM((tk, 2 * tq), F32), weights_tile],
            name="attn_prompt", **common)(*args)

    def bounded(kmax, *args):
        return pl.pallas_call(
            functools.partial(_attn_prompt_bounded_kernel, tq=tq, tk=tk),
            in_specs=[pl.BlockSpec(memory_space=pltpu.SMEM)] + in_specs,
            scratch_shapes=[key_sums, weights_tile],
            name="attn_prompt_bounded", **common)(kmax, *args)

    qmax = 1.01 * LOG2E * jnp.max(jnp.abs(q_gain))
    kmax = 1.01 * HEAD_DIM ** 0.5 * jnp.max(jnp.abs(k_gain))
    return lax.cond(2.0 * qmax * kmax <= BOUND_SLACK_OCTAVES, bounded, running_max,
                    kmax.reshape(1, 1).astype(F32), qt, kbf, vt, ft, fcol)


def _attn_sample_kernel(q_ref, ck_ref, cv_ref, kn_ref, vn_ref, f_ref, fq_ref, o_ref,
                        qbd_ref, m_ref, l_ref, acc_ref, *, tk, past, tt):
    j = pl.program_id(1)
    rows = N_HEADS * tt
    own = ((lax.broadcasted_iota(jnp.int32, (rows, ATT_WIDTH), 0) // tt)
           == (lax.broadcasted_iota(jnp.int32, (rows, ATT_WIDTH), 1) // HEAD_DIM))

    @pl.when(j == 0)
    def _():
        q = q_ref[...]
        qbd_ref[...] = jnp.where(own, jnp.concatenate([q] * N_HEADS, axis=0), jnp.zeros((), BF16))
        m_ref[...] = jnp.full(m_ref.shape, NEG, F32)
        l_ref[...] = jnp.zeros(l_ref.shape, F32)
        acc_ref[...] = jnp.zeros(acc_ref.shape, F32)

    fq3 = fq_ref[...].reshape(N_HEADS, tt, 1)

    def update(kb, vb, fk, mask, time_on_lanes):
        if time_on_lanes:
            n = kb.shape[1]
            s = _dot(qbd_ref[...], kb.astype(BF16))
        else:
            n = kb.shape[0]
            s = _dot_nt(qbd_ref[...], kb.astype(BF16))
        s = s.reshape(N_HEADS, tt, n) + fq3 - fk[:N_HEADS][:, None, :]
        if mask is not None:
            s = jnp.where(mask, s, NEG)
        m_old = m_ref[...].reshape(N_HEADS, tt, 1)
        m_new = jnp.maximum(m_old, jnp.max(s, axis=-1, keepdims=True))
        alpha = jnp.exp(m_old - m_new)
        p = jnp.exp(s - m_new)
        l_ref[...] = (alpha * l_ref[...].reshape(N_HEADS, tt, 1)
                      + jnp.sum(p, axis=-1, keepdims=True)).reshape(rows, 1)
        m_ref[...] = m_new.reshape(rows, 1)
        pb = p.reshape(rows, n).astype(BF16)
        pv = _dot_nt(pb, vb.astype(BF16)) if time_on_lanes else _dot(pb, vb.astype(BF16))
        acc_ref[...] = alpha.reshape(rows, 1) * acc_ref[...] + pv

    k0 = pl.multiple_of(j * tk, tk)
    update(ck_ref[...], cv_ref[...], f_ref[:, pl.ds(k0, tk)], None, True)

    @pl.when(j == pl.num_programs(1) - 1)
    def _():
        mask = (lax.broadcasted_iota(jnp.int32, (1, tt, tt), 2)
                <= lax.broadcasted_iota(jnp.int32, (1, tt, tt), 1))
        update(kn_ref[...], vn_ref[...], f_ref[:, past:past + tt], mask, False)
        o = jnp.where(own, acc_ref[...] / l_ref[...], 0.0).reshape(N_HEADS, tt, ATT_WIDTH)
        o_ref[...] = jnp.sum(o, axis=0).astype(BF16)


def _attn_sample(q, cache_k, cache_v, k_new, v_new, f_all, fq_col, layer, *, tk):
    nb, tt, _ = q.shape
    past = cache_k.shape[3]
    rows = N_HEADS * tt
    f_len = f_all.shape[-1]
    return pl.pallas_call(
        functools.partial(_attn_sample_kernel, tk=tk, past=past, tt=tt),
        grid=(nb, past // tk),
        in_specs=[
            pl.BlockSpec((None, tt, ATT_WIDTH), lambda i, j: (i, 0, 0)),
            pl.BlockSpec((None, None, ATT_WIDTH, tk), lambda i, j: (layer, i, 0, j)),
            pl.BlockSpec((None, None, ATT_WIDTH, tk), lambda i, j: (layer, i, 0, j)),
            pl.BlockSpec((None, None, tt, ATT_WIDTH), lambda i, j: (layer, i, 0, 0)),
            pl.BlockSpec((None, None, tt, ATT_WIDTH), lambda i, j: (layer, i, 0, 0)),
            pl.BlockSpec((None, HEADS_PAD, f_len), lambda i, j: (i, 0, 0)),
            pl.BlockSpec((None, rows, 1), lambda i, j: (i, 0, 0)),
        ],
        out_specs=pl.BlockSpec((None, tt, ATT_WIDTH), lambda i, j: (i, 0, 0)),
        out_shape=jax.ShapeDtypeStruct((nb, tt, ATT_WIDTH), BF16),
        scratch_shapes=[
            pltpu.VMEM((rows, ATT_WIDTH), BF16),
            pltpu.VMEM((rows, 1), F32),
            pltpu.VMEM((rows, 1), F32),
            pltpu.VMEM((rows, ATT_WIDTH), F32),
        ],
        compiler_params=pltpu.CompilerParams(
            dimension_semantics=("arbitrary", "arbitrary"), vmem_limit_bytes=VMEM_LIMIT),
        name="attn_sample",
    )(q, cache_k, cache_v, k_new, v_new, f_all, fq_col)


def _out_mlp_kernel(x_ref, py_ref, att_ref, g1_ref, sh_ref, sc_ref, g2_ref,
                    wop_ref, woa_ref, wup_ref, wdn_ref, y_ref, *, bb, tt, ff_chunk):
    m = bb * tt
    mix = (_dot(py_ref[...].reshape(m, POOL_WIDTH), wop_ref[...])
           + _dot(att_ref[...].reshape(m, ATT_WIDTH), woa_ref[...]))
    x1 = x_ref[...] + g1_ref[...] * mix.reshape(bb, tt, D_MODEL)
    ms = jnp.mean(x1 * x1, axis=-1, keepdims=True)
    h = (x1 * lax.rsqrt(ms + EPS)) * (1.0 + sc_ref[...]) + sh_ref[...]
    hb = h.reshape(m, D_MODEL).astype(BF16)
    ff = jnp.zeros((m, D_MODEL), F32)
    for c in range(D_FF // ff_chunk):
        up = _dot(hb, wup_ref[:, c * ff_chunk:(c + 1) * ff_chunk])
        act = jnp.square(jnp.maximum(up, 0.0)).astype(BF16)
        ff = ff + _dot(act, wdn_ref[c * ff_chunk:(c + 1) * ff_chunk, :])
    y_ref[...] = x1 + g2_ref[...] * ff.reshape(bb, tt, D_MODEL)


def _out_mlp(x, py, att, g1, sh, sc, g2, w, *, bb, tt, name):
    b, t, _ = x.shape

    def tile(width):
        return pl.BlockSpec((bb, tt, width), lambda i, j: (i, j, 0))

    def per_b():
        return pl.BlockSpec((bb, 1, D_MODEL), lambda i, j: (i, 0, 0))

    return pl.pallas_call(
        functools.partial(_out_mlp_kernel, bb=bb, tt=tt, ff_chunk=1024),
        grid=(b // bb, t // tt),
        in_specs=[
            tile(D_MODEL), tile(POOL_WIDTH), tile(ATT_WIDTH),
            per_b(), per_b(), per_b(), per_b(),
            _resident((POOL_WIDTH, D_MODEL)), _resident((ATT_WIDTH, D_MODEL)),
            _resident((D_MODEL, D_FF)), _resident((D_FF, D_MODEL)),
        ],
        out_specs=tile(D_MODEL),
        out_shape=jax.ShapeDtypeStruct((b, t, D_MODEL), F32),
        compiler_params=pltpu.CompilerParams(
            dimension_semantics=("arbitrary", "arbitrary"), vmem_limit_bytes=VMEM_LIMIT),
        name=name,
    )(x, py, att, g1, sh, sc, g2, w["w_out_p"], w["w_out_a"], w["w_up"], w["w_down"])


def _layer_weights(depth, layer, w_in, b_f, q_gain, k_gain, w_pool, pool_scale, w_out, w_up, w_down):
    wl = w_in[layer]
    o = POOL_WIDTH
    w_f = wl[:, o + 3 * ATT_WIDTH:]
    head_of_col = np.arange(ATT_WIDTH) // HEAD_DIM
    ind = (head_of_col[:, None] == np.arange(LANES)[None, :]).astype(np.float32)
    w_pool_bd = jnp.zeros((POOL_WIDTH, POOL_WIDTH), F32)
    for g in range(len(POOL_WINDOWS)):
        sl = slice(g * POOL_GROUP_DIM, (g + 1) * POOL_GROUP_DIM)
        w_pool_bd = w_pool_bd.at[sl, sl].set(w_pool[layer, g])
    return {
        "depth": depth,
        "w_u": wl[:, :o].astype(BF16),
        "w_q": wl[:, o:o + ATT_WIDTH].astype(BF16),
        "w_k": wl[:, o + ATT_WIDTH:o + 2 * ATT_WIDTH].astype(BF16),
        "w_v": wl[:, o + 2 * ATT_WIDTH:o + 3 * ATT_WIDTH].astype(BF16),
        "w_f": jnp.pad(w_f, ((0, 0), (0, LANES - N_HEADS))).astype(BF16),
        "w_ft": jnp.pad(w_f.T, ((0, HEADS_PAD - N_HEADS), (0, 0))).astype(BF16),
        "bf_row": jnp.pad(b_f[layer], (0, LANES - N_HEADS)).reshape(1, LANES),
        "bf_col": jnp.pad(b_f[layer], (0, HEADS_PAD - N_HEADS)).reshape(HEADS_PAD, 1),
        "qg": jnp.tile(q_gain[layer], N_HEADS).reshape(1, ATT_WIDTH),
        "kg": jnp.tile(k_gain[layer], N_HEADS).reshape(1, ATT_WIDTH),
        "ind": jnp.asarray(ind, BF16),
        "ind_t": jnp.asarray(ind.T, BF16),
        "w_pool": w_pool_bd.astype(BF16),
        "pool_scale": pool_scale[layer].reshape(1, POOL_WIDTH),
        "w_out_p": w_out[layer, :POOL_WIDTH].astype(BF16),
        "w_out_a": w_out[layer, POOL_WIDTH:].astype(BF16),
        "w_up": w_up[layer].astype(BF16),
        "w_down": w_down[layer].astype(BF16),
    }


def _mod_parts(mod_rows):
    b = mod_rows.shape[0]
    parts = mod_rows.reshape(b, 6, 1, D_MODEL)
    return [parts[:, i] for i in range(6)]


def kernel(x_prompt, x_sample, c_prompt, c_sample, cache_k, cache_v, cache_logf, cache_pool,
           w_mod, b_mod, w_in, b_f, q_gain, k_gain, w_pool, pool_scale, w_out, w_up, w_down):
    depth = w_mod.shape[0]
    bp, seq, _ = x_prompt.shape
    bs, dec_seq, _ = x_sample.shape
    past = cache_k.shape[2]
    tq = 512
    tk_p = 1024
    tm_in = 256
    tm_out = 512
    tk_cache = 1024

    n_c = bp + bs
    c_rows = -(-n_c // 8) * 8
    c_all = jnp.pad(jnp.concatenate([c_prompt, c_sample], axis=0), ((0, c_rows - n_c), (0, 0)))
    mod = _modulation(c_all, w_mod, b_mod)

    ck = jnp.transpose(cache_k, (0, 1, 3, 4, 2)).reshape(depth, bs, ATT_WIDTH, past)
    cv = jnp.transpose(cache_v, (0, 1, 3, 4, 2)).reshape(depth, bs, ATT_WIDTH, past)
    s_chunks = -(-(past + dec_seq) // CUMSUM_CHUNK)
    s_chunks += s_chunks % 2
    s_len = s_chunks * CUMSUM_CHUNK
    hist_p = jnp.zeros((bp, HIST_ROWS, POOL_WIDTH), F32)

    y_p, y_s = x_prompt, x_sample
    slabs_p = slabs_s = None
    for l in range(depth):
        w = _layer_weights(depth, l, w_in, b_f, q_gain, k_gain, w_pool, pool_scale, w_out, w_up, w_down)
        sh1, sc1, g1, sh2, sc2, g2 = _mod_parts(mod[l, :bp])
        slabs_p, (kbf, qt, vt, logft, py) = _in_proj(
            y_p, sh1, sc1, hist_p, w, slabs_p, l, bb=1, tt=tm_in, pos0=0, transposed=True)
        ft = _cumsum_time(logft, rows_per_block=bp * HEADS_PAD * (seq // CUMSUM_CHUNK))
        att = _attn_prompt(qt, kbf, vt, ft, jnp.swapaxes(ft, 1, 2), q_gain[l], k_gain[l],
                           tq=tq, tk=tk_p)
        y_p = _out_mlp(y_p, py, att, g1, sh2, sc2, g2, w, bb=1, tt=tm_out, name="out_mlp_prompt")

        sh1, sc1, g1, sh2, sc2, g2 = _mod_parts(mod[l, bp:n_c])
        hist_s = jnp.pad(cache_pool[l], ((0, 0), (HIST_ROWS - POOL_HIST, 0), (0, 0)))
        slabs_s, (_, q_s, logft_s, py_s) = _in_proj(
            y_s, sh1, sc1, hist_s, w, slabs_s, l, bb=bs, tt=dec_seq, pos0=past, transposed=False)
        new_t = jnp.swapaxes(logft_s.reshape(HEADS_PAD, bs, dec_seq), 0, 1)
        cache_t = jnp.pad(jnp.swapaxes(cache_logf[l], 1, 2), ((0, 0), (0, HEADS_PAD - N_HEADS), (0, 0)))
        seq_s = jnp.concatenate(
            [cache_t, new_t, jnp.zeros((bs, HEADS_PAD, s_len - past - dec_seq), F32)], axis=-1)
        f_all = _cumsum_time(seq_s, rows_per_block=4 * HEADS_PAD * s_chunks)
        fq_col = f_all[:, :N_HEADS, past:past + dec_seq].reshape(bs, N_HEADS * dec_seq, 1)
        att_s = _attn_sample(q_s, ck, cv, slabs_s[0], slabs_s[1], f_all, fq_col, l, tk=tk_cache)
        y_s = _out_mlp(y_s, py_s, att_s, g1, sh2, sc2, g2, w, bb=bs, tt=dec_seq, name="out_mlp_sample")

    k_p, v_p, f_p, p_p = slabs_p
    k_s, v_s, f_s, p_s = slabs_s
    return (y_p, y_s,
            k_p.reshape(depth, bp, seq, N_HEADS, HEAD_DIM), v_p.reshape(depth, bp, seq, N_HEADS, HEAD_DIM),
            f_p, p_p,
            k_s.reshape(depth, bs, dec_seq, N_HEADS, HEAD_DIM), v_s.reshape(depth, bs, dec_seq, N_HEADS, HEAD_DIM),
            f_s, p_s)
```

```python
import functools

import numpy as np
import jax
import jax.numpy as jnp
from jax import lax
from jax.experimental import pallas as pl
from jax.experimental.pallas import tpu as pltpu

D_MODEL = 1024
POOL_WIDTH = 256
POOL_GROUP_DIM = 64
POOL_WINDOWS = (2, 4, 8, 16)
POOL_HIST = 15
HIST_ROWS = 16
ATT_WIDTH = 768
HEAD_DIM = 64
N_HEADS = 12
HEADS_PAD = 16
D_FF = 4096
EPS = 1e-6
NEG = -1e30
LOG2E = 1.4426950408889634
BOUND_SLACK_OCTAVES = 64.0

LANES = 128
CUMSUM_CHUNK = 256
VMEM_LIMIT = 56 * 1024 * 1024

F32 = jnp.float32
BF16 = jnp.bfloat16


def _dot(a, b):
    return jnp.dot(a, b, preferred_element_type=F32)


def _dot_nt(a, b):
    return lax.dot_general(a, b, (((1,), (1,)), ((), ())), preferred_element_type=F32)


def _split_dot(a, b):
    hi = a.astype(BF16)
    lo = (a - hi.astype(F32)).astype(BF16)
    return _dot(hi, b) + _dot(lo, b)


def _resident(shape):
    nd = len(shape)
    return pl.BlockSpec(shape, lambda *_: (0,) * nd, pipeline_mode=pl.Buffered(1))


def _mod_kernel(c_ref, w_ref, b_ref, o_ref):
    c = c_ref[...]
    s = c / (1.0 + jnp.exp(-c))
    o_ref[...] = _dot(s.astype(BF16), w_ref[...].astype(BF16)) + b_ref[...]


def _modulation(c_all, w_mod, b_mod):
    depth, _, n = w_mod.shape
    rows = c_all.shape[0]
    tn = 1536
    return pl.pallas_call(
        _mod_kernel,
        grid=(depth, n // tn),
        in_specs=[
            pl.BlockSpec((rows, D_MODEL), lambda l, j: (0, 0)),
            pl.BlockSpec((None, D_MODEL, tn), lambda l, j: (l, 0, j)),
            pl.BlockSpec((None, 1, tn), lambda l, j: (l, 0, j)),
        ],
        out_specs=pl.BlockSpec((None, rows, tn), lambda l, j: (l, 0, j)),
        out_shape=jax.ShapeDtypeStruct((depth, rows, n), F32),
        compiler_params=pltpu.CompilerParams(
            dimension_semantics=("arbitrary", "arbitrary"), vmem_limit_bytes=VMEM_LIMIT),
        name="modulation",
    )(c_all, w_mod, b_mod.reshape(depth, 1, n))


def _head_rms(a, ind, ind_t):
    ssum = _dot((a * a).astype(BF16), ind)
    r = lax.rsqrt(ssum * (1.0 / HEAD_DIM) + EPS)
    return a * _split_dot(r, ind_t)


def _log_sigmoid(x):
    return jnp.minimum(x, 0.0) - jnp.log1p(jnp.exp(-jnp.abs(x)))


def _in_proj_kernel(*refs, n_alias, bb, tt, pos0, transposed):
    refs = refs[n_alias:]
    (x_ref, sh_ref, sc_ref, hist_ref, wu_ref, wq_ref, wk_ref, wv_ref, wf_ref, wft_ref,
     bfr_ref, bfc_ref, qg_ref, kg_ref, ind_ref, indt_ref, wp_ref, ps_ref,
     k32_ref, v32_ref, logf_ref, pstate_ref, kbf_ref, q_ref) = refs[:24]
    vt_ref = refs[24] if transposed else None
    logft_ref, py_ref, z_ref = refs[-3:]
    ti = pl.program_id(1)
    m = bb * tt

    x = x_ref[...]
    ms = jnp.mean(x * x, axis=-1, keepdims=True)
    h = (x * lax.rsqrt(ms + EPS)) * (1.0 + sc_ref[...]) + sh_ref[...]
    hb = h.reshape(m, D_MODEL).astype(BF16)

    ind = ind_ref[...]
    ind_t = indt_ref[...]
    q = _head_rms(_dot(hb, wq_ref[...]), ind, ind_t) * qg_ref[...]
    k = _head_rms(_dot(hb, wk_ref[...]), ind, ind_t) * kg_ref[...]
    v = _dot(hb, wv_ref[...])
    k32_ref[...] = k.reshape(bb, tt, ATT_WIDTH)
    v32_ref[...] = v.reshape(bb, tt, ATT_WIDTH)
    kbf_ref[...] = k.reshape(bb, tt, ATT_WIDTH).astype(BF16)
    qs = q * (HEAD_DIM ** -0.5 * (LOG2E if transposed else 1.0))
    if transposed:
        q_ref[...] = qs.T.astype(BF16)
        vt_ref[...] = v.T.astype(BF16)
    else:
        q_ref[...] = qs.reshape(bb, tt, ATT_WIDTH).astype(BF16)

    fl = _dot(hb, wf_ref[...]) + bfr_ref[...]
    logf_ref[...] = _log_sigmoid(fl)[:, :N_HEADS].reshape(bb, tt, N_HEADS)
    flt = _dot_nt(wft_ref[...], hb) + bfc_ref[...]
    logft_ref[...] = _log_sigmoid(flt)

    u3 = _dot(hb, wu_ref[...]).reshape(bb, tt, POOL_WIDTH)

    @pl.when(ti == 0)
    def _():
        z_ref[:, 0:HIST_ROWS, :] = hist_ref[...]

    z_ref[:, HIST_ROWS:HIST_ROWS + tt, :] = u3
    acc = u3
    sums = {}
    for i in range(1, max(POOL_WINDOWS)):
        acc = acc + z_ref[:, HIST_ROWS - i:HIST_ROWS - i + tt, :]
        if i + 1 in POOL_WINDOWS:
            sums[i + 1] = acc
    group = lax.broadcasted_iota(jnp.int32, (1, 1, POOL_WIDTH), 2) // POOL_GROUP_DIM
    win = sums[POOL_WINDOWS[-1]]
    width = jnp.full((1, 1, POOL_WIDTH), float(POOL_WINDOWS[-1]), F32)
    for g in range(len(POOL_WINDOWS) - 2, -1, -1):
        win = jnp.where(group == g, sums[POOL_WINDOWS[g]], win)
        width = jnp.where(group == g, float(POOL_WINDOWS[g]), width)
    pos = (pos0 + ti * tt + lax.broadcasted_iota(jnp.int32, (1, tt, 1), 1)).astype(F32)
    count = jnp.minimum(pos + 1.0, width)
    mixed = win / count - u3
    y = _dot(mixed.reshape(m, POOL_WIDTH).astype(BF16), wp_ref[...]) * ps_ref[...]
    py_ref[...] = y.reshape(bb, tt, POOL_WIDTH).astype(BF16)

    pstate_ref[...] = z_ref[:, tt + 1:tt + HIST_ROWS, :]
    z_ref[:, 0:HIST_ROWS, :] = z_ref[:, tt:tt + HIST_ROWS, :]


def _in_proj(x, sh, sc, hist, w, slabs, layer, *, bb, tt, pos0, transposed):
    b, t, _ = x.shape
    depth = w["depth"]
    nb, nt = b // bb, t // tt
    m = bb * tt
    n_alias = 0 if slabs is None else 4

    def per_b(shape):
        return pl.BlockSpec(shape, lambda i, j: (i,) + (0,) * (len(shape) - 1))

    in_specs = [pl.BlockSpec(memory_space=pl.ANY)] * n_alias + [
        pl.BlockSpec((bb, tt, D_MODEL), lambda i, j: (i, j, 0)),
        per_b((bb, 1, D_MODEL)), per_b((bb, 1, D_MODEL)), per_b((bb, HIST_ROWS, POOL_WIDTH)),
        _resident((D_MODEL, POOL_WIDTH)), _resident((D_MODEL, ATT_WIDTH)),
        _resident((D_MODEL, ATT_WIDTH)), _resident((D_MODEL, ATT_WIDTH)),
        _resident((D_MODEL, LANES)), _resident((HEADS_PAD, D_MODEL)),
        _resident((1, LANES)), _resident((HEADS_PAD, 1)),
        _resident((1, ATT_WIDTH)), _resident((1, ATT_WIDTH)),
        _resident((ATT_WIDTH, LANES)), _resident((LANES, ATT_WIDTH)),
        _resident((POOL_WIDTH, POOL_WIDTH)), _resident((1, POOL_WIDTH)),
    ]
    slab_shapes = [
        jax.ShapeDtypeStruct((depth, b, t, ATT_WIDTH), F32),
        jax.ShapeDtypeStruct((depth, b, t, ATT_WIDTH), F32),
        jax.ShapeDtypeStruct((depth, b, t, N_HEADS), F32),
        jax.ShapeDtypeStruct((depth, b, POOL_HIST, POOL_WIDTH), F32),
    ]
    slab_specs = [
        pl.BlockSpec((None, bb, tt, ATT_WIDTH), lambda i, j: (layer, i, j, 0)),
        pl.BlockSpec((None, bb, tt, ATT_WIDTH), lambda i, j: (layer, i, j, 0)),
        pl.BlockSpec((None, bb, tt, N_HEADS), lambda i, j: (layer, i, j, 0)),
        pl.BlockSpec((None, bb, POOL_HIST, POOL_WIDTH), lambda i, j: (layer, i, 0, 0)),
    ]
    if transposed:
        q_shape = jax.ShapeDtypeStruct((b, ATT_WIDTH, t), BF16)
        q_spec = pl.BlockSpec((None, ATT_WIDTH, tt), lambda i, j: (i, 0, j))
    else:
        q_shape = jax.ShapeDtypeStruct((b, t, ATT_WIDTH), BF16)
        q_spec = pl.BlockSpec((bb, tt, ATT_WIDTH), lambda i, j: (i, j, 0))
    out_shape = slab_shapes + [
        jax.ShapeDtypeStruct((b, t, ATT_WIDTH), BF16),
        q_shape,
    ] + ([q_shape] if transposed else []) + [
        jax.ShapeDtypeStruct((nb, HEADS_PAD, nt * m), F32),
        jax.ShapeDtypeStruct((b, t, POOL_WIDTH), BF16),
    ]
    out_specs = slab_specs + [
        pl.BlockSpec((bb, tt, ATT_WIDTH), lambda i, j: (i, j, 0)),
        q_spec,
    ] + ([q_spec] if transposed else []) + [
        pl.BlockSpec((None, HEADS_PAD, m), lambda i, j: (i, 0, j)),
        pl.BlockSpec((bb, tt, POOL_WIDTH), lambda i, j: (i, j, 0)),
    ]
    args = ([] if slabs is None else list(slabs)) + [
        x, sh, sc, hist, w["w_u"], w["w_q"], w["w_k"], w["w_v"], w["w_f"], w["w_ft"],
        w["bf_row"], w["bf_col"], w["qg"], w["kg"], w["ind"], w["ind_t"], w["w_pool"],
        w["pool_scale"],
    ]
    outs = pl.pallas_call(
        functools.partial(_in_proj_kernel, n_alias=n_alias, bb=bb, tt=tt, pos0=pos0,
                          transposed=transposed),
        grid=(nb, nt),
        in_specs=in_specs,
        out_specs=out_specs,
        out_shape=out_shape,
        scratch_shapes=[pltpu.VMEM((bb, tt + HIST_ROWS, POOL_WIDTH), F32)],
        input_output_aliases={i: i for i in range(n_alias)},
        compiler_params=pltpu.CompilerParams(
            dimension_semantics=("arbitrary", "arbitrary"), vmem_limit_bytes=VMEM_LIMIT),
        name="in_proj_prompt" if transposed else "in_proj_sample",
    )(*args)
    return outs[:4], outs[4:]


def _cumsum_kernel(x_ref, gcol_ref, grow_ref, o_ref):
    x = x_ref[...]
    rb, c = x.shape
    tri = (lax.broadcasted_iota(jnp.int32, (c, c), 0)
           <= lax.broadcasted_iota(jnp.int32, (c, c), 1)).astype(F32)
    y = jnp.dot(x, tri, precision=lax.Precision.HIGHEST, preferred_element_type=F32)
    tot = jnp.broadcast_to(y[:, c - 1:c], (rb, LANES))
    before = ((gcol_ref[...] == grow_ref[...])
              & (lax.broadcasted_iota(jnp.int32, (rb, rb), 1)
                 < lax.broadcasted_iota(jnp.int32, (rb, rb), 0))).astype(F32)
    off = jnp.dot(before, tot, precision=lax.Precision.HIGHEST, preferred_element_type=F32)
    o_ref[...] = y + off[:, 0:1]


def _cumsum_time(x, rows_per_block):
    lead, length = x.shape[:-1], x.shape[-1]
    chunks = length // CUMSUM_CHUNK
    r = int(np.prod(lead)) * chunks
    rb = rows_per_block
    gid = (np.arange(rb) // chunks).astype(np.int32)
    out = pl.pallas_call(
        _cumsum_kernel,
        grid=(r // rb,),
        in_specs=[
            pl.BlockSpec((rb, CUMSUM_CHUNK), lambda i: (i, 0)),
            _resident((rb, 1)), _resident((1, rb)),
        ],
        out_specs=pl.BlockSpec((rb, CUMSUM_CHUNK), lambda i: (i, 0)),
        out_shape=jax.ShapeDtypeStruct((r, CUMSUM_CHUNK), F32),
        compiler_params=pltpu.CompilerParams(
            dimension_semantics=("arbitrary",), vmem_limit_bytes=VMEM_LIMIT),
        name="cumsum_time",
    )(x.reshape(r, CUMSUM_CHUNK), jnp.asarray(gid).reshape(rb, 1), jnp.asarray(gid).reshape(1, rb))
    return out.reshape(x.shape)


def _fill_key_forget_sums(fcol_ref, fkb_ref, hp):
    chunk = 512
    lane = lax.broadcasted_iota(jnp.int32, (chunk, HEADS_PAD), 1)
    for hh in range(2):
        def fill(c, carry, hh=hh):
            r0 = pl.multiple_of(c * chunk, chunk)
            blk = fcol_ref[pl.ds(r0, chunk), :]
            col = jnp.sum(jnp.where(lane == 2 * hp + hh, blk, 0.0), axis=1, keepdims=True)
            fkb_ref[hh, pl.ds(r0, chunk), :] = jnp.broadcast_to(col * LOG2E, (chunk, LANES))
            return carry
        lax.fori_loop(0, fcol_ref.shape[0] // chunk, fill, 0)


def _pair_query_columns(qt):
    row_head = lax.broadcasted_iota(jnp.int32, (2 * HEAD_DIM, 1), 0) // HEAD_DIM
    zero = jnp.zeros_like(qt)
    return jnp.concatenate(
        [jnp.where(row_head == 0, qt, zero), jnp.where(row_head == 1, qt, zero)], axis=1)


def _attn_prompt_bounded_kernel(kmax_ref, qt_ref, k_ref, vt_ref, ft_ref, fcol_ref, o_ref,
                                fkb_ref, p_ref, *, tq, tk):
    hp = pl.program_id(1)
    qi = pl.program_id(2)
    n_full = (qi * tq) // tk

    @pl.when(qi == 0)
    def _():
        _fill_key_forget_sums(fcol_ref, fkb_ref, hp)

    causal = (n_full * tk + lax.broadcasted_iota(jnp.int32, (tk, tq), 0)
              <= qi * tq + lax.broadcasted_iota(jnp.int32, (tk, tq), 1))
    ones_rows = jnp.ones((HEADS_PAD, tk), BF16)
    qt = qt_ref[...]
    q2 = _pair_query_columns(qt)
    qsq = jnp.square(qt.astype(F32))
    shift = []
    for hh in range(2):
        qnorm = jnp.sqrt(jnp.sum(qsq[hh * HEAD_DIM:(hh + 1) * HEAD_DIM], axis=0, keepdims=True))
        shift.append(ft_ref[pl.ds(2 * hp + hh, 1), :] * LOG2E - qnorm * kmax_ref[0, 0])

    def weights(j, masked):
        k0 = pl.multiple_of(j * tk, tk)
        s2 = _dot(k_ref[pl.ds(k0, tk), :], q2)
        ps = []
        for hh in range(2):
            fk = fkb_ref[hh, pl.ds(k0, tk), :]
            a = (s2[:, hh * tq:(hh + 1) * tq] - jnp.concatenate([fk] * (tq // LANES), axis=1)
                 + shift[hh])
            if masked:
                a = jnp.where(causal, a, NEG)
            ps.append(jnp.exp2(a).astype(BF16))
        return jnp.concatenate(ps, axis=1)

    def accumulate(j, acc, p2):
        k0 = pl.multiple_of(j * tk, tk)
        out = []
        for hh in range(2):
            vb = jnp.concatenate(
                [vt_ref[pl.ds(hh * HEAD_DIM, HEAD_DIM), pl.ds(k0, tk)], ones_rows], axis=0)
            out.append(acc[hh] + _dot(vb, p2[:, hh * tq:(hh + 1) * tq]))
        return tuple(out)

    p_ref[...] = weights(n_full, True)

    def step(j, acc):
        p_next = weights(j, False)
        acc = accumulate(jnp.where(j == 0, n_full, j - 1), acc, p_ref[...])
        p_ref[...] = p_next
        return acc

    acc0 = jnp.zeros((HEAD_DIM + HEADS_PAD, tq), F32)
    acc = lax.fori_loop(0, n_full, step, (acc0, acc0))
    acc = accumulate(jnp.maximum(n_full - 1, 0), acc, p_ref[...])
    outs = [a[0:HEAD_DIM] / a[HEAD_DIM:HEAD_DIM + 1] for a in acc]
    o_ref[...] = jnp.concatenate(outs, axis=0).T.astype(BF16)


def _attn_prompt_kernel(qt_ref, k_ref, vt_ref, ft_ref, fcol_ref, o_ref, fkb_ref, s_ref, p_ref,
                        *, tq, tk):
    hp = pl.program_id(1)
    qi = pl.program_id(2)
    n_full = (qi * tq) // tk

    @pl.when(qi == 0)
    def _():
        _fill_key_forget_sums(fcol_ref, fkb_ref, hp)

    causal = (n_full * tk + lax.broadcasted_iota(jnp.int32, (tk, tq), 0)
              <= qi * tq + lax.broadcasted_iota(jnp.int32, (tk, tq), 1))
    ones_rows = jnp.ones((HEADS_PAD, tk), BF16)
    q2 = _pair_query_columns(qt_ref[...])
    fq = [ft_ref[pl.ds(2 * hp + hh, 1), :] * LOG2E for hh in range(2)]

    def biased_scores(j):
        k0 = pl.multiple_of(j * tk, tk)
        s2 = _dot(k_ref[pl.ds(k0, tk), :], q2)
        a, cmax = [], []
        for hh in range(2):
            fk = fkb_ref[hh, pl.ds(k0, tk), :]
            ah = s2[:, hh * tq:(hh + 1) * tq] - jnp.concatenate([fk] * (tq // LANES), axis=1)
            a.append(ah)
            cmax.append(jnp.max(ah, axis=0, keepdims=True))
        return jnp.concatenate(a, axis=1), cmax

    def weights(a2, cmax, m_run):
        ps, alphas, ms = [], [], []
        for hh in range(2):
            m_new = jnp.maximum(m_run[hh], cmax[hh] + fq[hh])
            alphas.append(jnp.exp2(m_run[hh] - m_new))
            ps.append(jnp.exp2(a2[:, hh * tq:(hh + 1) * tq] + (fq[hh] - m_new)).astype(BF16))
            ms.append(m_new)
        return ps, alphas, ms

    def accumulate(j, acc, p, alpha):
        k0 = pl.multiple_of(j * tk, tk)
        out = []
        for hh in range(2):
            vb = jnp.concatenate(
                [vt_ref[pl.ds(hh * HEAD_DIM, HEAD_DIM), pl.ds(k0, tk)], ones_rows], axis=0)
            out.append(alpha[hh] * acc[hh] + _dot(vb, p[hh]))
        return out

    a0, cmax0 = biased_scores(0)
    s_ref[...] = a0
    p_ref[...] = jnp.zeros(p_ref.shape, BF16)

    def step(j, carry):
        acc, m_run, alpha, cmax = carry
        a_cur = s_ref[...]
        p_prev = p_ref[...]
        a_next, cmax_next = biased_scores(j + 1)
        p, alpha_new, m_new = weights(a_cur, cmax, m_run)
        acc = accumulate(jnp.maximum(j - 1, 0), acc, [p_prev[:, :tq], p_prev[:, tq:]], alpha)
        s_ref[...] = a_next
        p_ref[...] = jnp.concatenate(p, axis=1)
        return tuple(acc), tuple(m_new), tuple(alpha_new), tuple(cmax_next)

    acc0 = jnp.zeros((HEAD_DIM + HEADS_PAD, tq), F32)
    m0 = jnp.full((1, tq), NEG, F32)
    one = jnp.ones((1, tq), F32)
    acc, m_run, alpha, _ = lax.fori_loop(
        0, n_full, step, ((acc0, acc0), (m0, m0), (one, one), tuple(cmax0)))
    p_prev = p_ref[...]
    a_diag = s_ref[...]
    a_diag = jnp.concatenate(
        [jnp.where(causal, a_diag[:, hh * tq:(hh + 1) * tq], NEG) for hh in range(2)], axis=1)
    cmax = [jnp.max(a_diag[:, hh * tq:(hh + 1) * tq], axis=0, keepdims=True) for hh in range(2)]
    p, alpha_new, _ = weights(a_diag, cmax, m_run)
    acc = accumulate(jnp.maximum(n_full - 1, 0), acc, [p_prev[:, :tq], p_prev[:, tq:]], alpha)
    acc = accumulate(n_full, acc, p, alpha_new)
    outs = [a[0:HEAD_DIM] / a[HEAD_DIM:HEAD_DIM + 1] for a in acc]
    o_ref[...] = jnp.concatenate(outs, axis=0).T.astype(BF16)


def _attn_prompt(qt, kbf, vt, ft, fcol, q_gain, k_gain, *, tq, tk):
    b, t, _ = kbf.shape
    pairs = N_HEADS // 2
    in_specs = [
        pl.BlockSpec((None, 2 * HEAD_DIM, tq), lambda i, p, j: (i, p, j)),
        pl.BlockSpec((None, t, 2 * HEAD_DIM), lambda i, p, j: (i, 0, p)),
        pl.BlockSpec((None, 2 * HEAD_DIM, t), lambda i, p, j: (i, p, 0)),
        pl.BlockSpec((None, HEADS_PAD, tq), lambda i, p, j: (i, 0, j)),
        pl.BlockSpec((None, t, HEADS_PAD), lambda i, p, j: (i, 0, 0)),
    ]
    common = dict(
        grid=(b, pairs, t // tq),
        out_specs=pl.BlockSpec((None, tq, 2 * HEAD_DIM), lambda i, p, j: (i, j, p)),
        out_shape=jax.ShapeDtypeStruct((b, t, ATT_WIDTH), BF16),
        compiler_params=pltpu.CompilerParams(
            dimension_semantics=("arbitrary", "arbitrary", "arbitrary"),
            vmem_limit_bytes=VMEM_LIMIT),
    )
    key_sums = pltpu.VMEM((2, t, LANES), F32)
    weights_tile = pltpu.VMEM((tk, 2 * tq), BF16)

    def running_max(_, *args):
        return pl.pallas_call(
            functools.partial(_attn_prompt_kernel, tq=tq, tk=tk),
            in_specs=in_specs,
            scratch_shapes=[key_sums, pltpu.VMEM((tk, 2 * tq), F32), weights_tile],
            name="attn_prompt", **common)(*args)

    def bounded(kmax, *args):
        return pl.pallas_call(
            functools.partial(_attn_prompt_bounded_kernel, tq=tq, tk=tk),
            in_specs=[pl.BlockSpec(memory_space=pltpu.SMEM)] + in_specs,
            scratch_shapes=[key_sums, weights_tile],
            name="attn_prompt_bounded", **common)(kmax, *args)

    qmax = 1.01 * LOG2E * jnp.max(jnp.abs(q_gain))
    kmax = 1.01 * HEAD_DIM ** 0.5 * jnp.max(jnp.abs(k_gain))
    return lax.cond(2.0 * qmax * kmax <= BOUND_SLACK_OCTAVES, bounded, running_max,
                    kmax.reshape(1, 1).astype(F32), qt, kbf, vt, ft, fcol)


def _attn_sample_kernel(q_ref, ck_ref, cv_ref, kn_ref, vn_ref, f_ref, fq_ref, o_ref,
                        qbd_ref, m_ref, l_ref, acc_ref, *, tk, past, tt):
    j = pl.program_id(1)
    rows = N_HEADS * tt
    own = ((lax.broadcasted_iota(jnp.int32, (rows, ATT_WIDTH), 0) // tt)
           == (lax.broadcasted_iota(jnp.int32, (rows, ATT_WIDTH), 1) // HEAD_DIM))

    @pl.when(j == 0)
    def _():
        q = q_ref[...]
        qbd_ref[...] = jnp.where(own, jnp.concatenate([q] * N_HEADS, axis=0), jnp.zeros((), BF16))
        m_ref[...] = jnp.full(m_ref.shape, NEG, F32)
        l_ref[...] = jnp.zeros(l_ref.shape, F32)
        acc_ref[...] = jnp.zeros(acc_ref.shape, F32)

    fq3 = fq_ref[...].reshape(N_HEADS, tt, 1)

    def update(kb, vb, fk, mask, time_on_lanes):
        if time_on_lanes:
            n = kb.shape[1]
            s = _dot(qbd_ref[...], kb.astype(BF16))
        else:
            n = kb.shape[0]
            s = _dot_nt(qbd_ref[...], kb.astype(BF16))
        s = s.reshape(N_HEADS, tt, n) + fq3 - fk[:N_HEADS][:, None, :]
        if mask is not None:
            s = jnp.where(mask, s, NEG)
        m_old = m_ref[...].reshape(N_HEADS, tt, 1)
        m_new = jnp.maximum(m_old, jnp.max(s, axis=-1, keepdims=True))
        alpha = jnp.exp(m_old - m_new)
        p = jnp.exp(s - m_new)
        l_ref[...] = (alpha * l_ref[...].reshape(N_HEADS, tt, 1)
                      + jnp.sum(p, axis=-1, keepdims=True)).reshape(rows, 1)
        m_ref[...] = m_new.reshape(rows, 1)
        pb = p.reshape(rows, n).astype(BF16)
        pv = _dot_nt(pb, vb.astype(BF16)) if time_on_lanes else _dot(pb, vb.astype(BF16))
        acc_ref[...] = alpha.reshape(rows, 1) * acc_ref[...] + pv

    k0 = pl.multiple_of(j * tk, tk)
    update(ck_ref[...], cv_ref[...], f_ref[:, pl.ds(k0, tk)], None, True)

    @pl.when(j == pl.num_programs(1) - 1)
    def _():
        mask = (lax.broadcasted_iota(jnp.int32, (1, tt, tt), 2)
                <= lax.broadcasted_iota(jnp.int32, (1, tt, tt), 1))
        update(kn_ref[...], vn_ref[...], f_ref[:, past:past + tt], mask, False)
        o = jnp.where(own, acc_ref[...] / l_ref[...], 0.0).reshape(N_HEADS, tt, ATT_WIDTH)
        o_ref[...] = jnp.sum(o, axis=0).astype(BF16)


def _attn_sample(q, cache_k, cache_v, k_new, v_new, f_all, fq_col, layer, *, tk):
    nb, tt, _ = q.shape
    past = cache_k.shape[3]
    rows = N_HEADS * tt
    f_len = f_all.shape[-1]
    return pl.pallas_call(
        functools.partial(_attn_sample_kernel, tk=tk, past=past, tt=tt),
        grid=(nb, past // tk),
        in_specs=[
            pl.BlockSpec((None, tt, ATT_WIDTH), lambda i, j: (i, 0, 0)),
            pl.BlockSpec((None, None, ATT_WIDTH, tk), lambda i, j: (layer, i, 0, j)),
            pl.BlockSpec((None, None, ATT_WIDTH, tk), lambda i, j: (layer, i, 0, j)),
            pl.BlockSpec((None, None, tt, ATT_WIDTH), lambda i, j: (layer, i, 0, 0)),
            pl.BlockSpec((None, None, tt, ATT_WIDTH), lambda i, j: (layer, i, 0, 0)),
            pl.BlockSpec((None, HEADS_PAD, f_len), lambda i, j: (i, 0, 0)),
            pl.BlockSpec((None, rows, 1), lambda i, j: (i, 0, 0)),
        ],
        out_specs=pl.BlockSpec((None, tt, ATT_WIDTH), lambda i, j: (i, 0, 0)),
        out_shape=jax.ShapeDtypeStruct((nb, tt, ATT_WIDTH), BF16),
        scratch_shapes=[
            pltpu.VMEM((rows, ATT_WIDTH), BF16),
            pltpu.VMEM((rows, 1), F32),
            pltpu.VMEM((rows, 1), F32),
            pltpu.VMEM((rows, ATT_WIDTH), F32),
        ],
        compiler_params=pltpu.CompilerParams(
            dimension_semantics=("arbitrary", "arbitrary"), vmem_limit_bytes=VMEM_LIMIT),
        name="attn_sample",
    )(q, cache_k, cache_v, k_new, v_new, f_all, fq_col)


def _out_mlp_kernel(x_ref, py_ref, att_ref, g1_ref, sh_ref, sc_ref, g2_ref,
                    wop_ref, woa_ref, wup_ref, wdn_ref, y_ref, *, bb, tt, ff_chunk):
    m = bb * tt
    mix = (_dot(py_ref[...].reshape(m, POOL_WIDTH), wop_ref[...])
           + _dot(att_ref[...].reshape(m, ATT_WIDTH), woa_ref[...]))
    x1 = x_ref[...] + g1_ref[...] * mix.reshape(bb, tt, D_MODEL)
    ms = jnp.mean(x1 * x1, axis=-1, keepdims=True)
    h = (x1 * lax.rsqrt(ms + EPS)) * (1.0 + sc_ref[...]) + sh_ref[...]
    hb = h.reshape(m, D_MODEL).astype(BF16)
    ff = jnp.zeros((m, D_MODEL), F32)
    for c in range(D_FF // ff_chunk):
        up = _dot(hb, wup_ref[:, c * ff_chunk:(c + 1) * ff_chunk])
        act = jnp.square(jnp.maximum(up, 0.0)).astype(BF16)
        ff = ff + _dot(act, wdn_ref[c * ff_chunk:(c + 1) * ff_chunk, :])
    y_ref[...] = x1 + g2_ref[...] * ff.reshape(bb, tt, D_MODEL)


def _out_mlp(x, py, att, g1, sh, sc, g2, w, *, bb, tt, name):
    b, t, _ = x.shape

    def tile(width):
        return pl.BlockSpec((bb, tt, width), lambda i, j: (i, j, 0))

    def per_b():
        return pl.BlockSpec((bb, 1, D_MODEL), lambda i, j: (i, 0, 0))

    return pl.pallas_call(
        functools.partial(_out_mlp_kernel, bb=bb, tt=tt, ff_chunk=1024),
        grid=(b // bb, t // tt),
        in_specs=[
            tile(D_MODEL), tile(POOL_WIDTH), tile(ATT_WIDTH),
            per_b(), per_b(), per_b(), per_b(),
            _resident((POOL_WIDTH, D_MODEL)), _resident((ATT_WIDTH, D_MODEL)),
            _resident((D_MODEL, D_FF)), _resident((D_FF, D_MODEL)),
        ],
        out_specs=tile(D_MODEL),
        out_shape=jax.ShapeDtypeStruct((b, t, D_MODEL), F32),
        compiler_params=pltpu.CompilerParams(
            dimension_semantics=("arbitrary", "arbitrary"), vmem_limit_bytes=VMEM_LIMIT),
        name=name,
    )(x, py, att, g1, sh, sc, g2, w["w_out_p"], w["w_out_a"], w["w_up"], w["w_down"])


def _layer_weights(depth, layer, w_in, b_f, q_gain, k_gain, w_pool, pool_scale, w_out, w_up, w_down):
    wl = w_in[layer]
    o = POOL_WIDTH
    w_f = wl[:, o + 3 * ATT_WIDTH:]
    head_of_col = np.arange(ATT_WIDTH) // HEAD_DIM
    ind = (head_of_col[:, None] == np.arange(LANES)[None, :]).astype(np.float32)
    w_pool_bd = jnp.zeros((POOL_WIDTH, POOL_WIDTH), F32)
    for g in range(len(POOL_WINDOWS)):
        sl = slice(g * POOL_GROUP_DIM, (g + 1) * POOL_GROUP_DIM)
        w_pool_bd = w_pool_bd.at[sl, sl].set(w_pool[layer, g])
    return {
        "depth": depth,
        "w_u": wl[:, :o].astype(BF16),
        "w_q": wl[:, o:o + ATT_WIDTH].astype(BF16),
        "w_k": wl[:, o + ATT_WIDTH:o + 2 * ATT_WIDTH].astype(BF16),
        "w_v": wl[:, o + 2 * ATT_WIDTH:o + 3 * ATT_WIDTH].astype(BF16),
        "w_f": jnp.pad(w_f, ((0, 0), (0, LANES - N_HEADS))).astype(BF16),
        "w_ft": jnp.pad(w_f.T, ((0, HEADS_PAD - N_HEADS), (0, 0))).astype(BF16),
        "bf_row": jnp.pad(b_f[layer], (0, LANES - N_HEADS)).reshape(1, LANES),
        "bf_col": jnp.pad(b_f[layer], (0, HEADS_PAD - N_HEADS)).reshape(HEADS_PAD, 1),
        "qg": jnp.tile(q_gain[layer], N_HEADS).reshape(1, ATT_WIDTH),
        "kg": jnp.tile(k_gain[layer], N_HEADS).reshape(1, ATT_WIDTH),
        "ind": jnp.asarray(ind, BF16),
        "ind_t": jnp.asarray(ind.T, BF16),
        "w_pool": w_pool_bd.astype(BF16),
        "pool_scale": pool_scale[layer].reshape(1, POOL_WIDTH),
        "w_out_p": w_out[layer, :POOL_WIDTH].astype(BF16),
        "w_out_a": w_out[layer, POOL_WIDTH:].astype(BF16),
        "w_up": w_up[layer].astype(BF16),
        "w_down": w_down[layer].astype(BF16),
    }


def _mod_parts(mod_rows):
    b = mod_rows.shape[0]
    parts = mod_rows.reshape(b, 6, 1, D_MODEL)
    return [parts[:, i] for i in range(6)]


def kernel(x_prompt, x_sample, c_prompt, c_sample, cache_k, cache_v, cache_logf, cache_pool,
           w_mod, b_mod, w_in, b_f, q_gain, k_gain, w_pool, pool_scale, w_out, w_up, w_down):
    depth = w_mod.shape[0]
    bp, seq, _ = x_prompt.shape
    bs, dec_seq, _ = x_sample.shape
    past = cache_k.shape[2]
    tq = 512
    tk_p = 1024
    tm_in = 512
    tm_out = 512
    tk_cache = 2048

    n_c = bp + bs
    c_rows = -(-n_c // 8) * 8
    c_all = jnp.pad(jnp.concatenate([c_prompt, c_sample], axis=0), ((0, c_rows - n_c), (0, 0)))
    mod = _modulation(c_all, w_mod, b_mod)

    ck = jnp.transpose(cache_k, (0, 1, 3, 4, 2)).reshape(depth, bs, ATT_WIDTH, past)
    cv = jnp.transpose(cache_v, (0, 1, 3, 4, 2)).reshape(depth, bs, ATT_WIDTH, past)
    s_chunks = -(-(past + dec_seq) // CUMSUM_CHUNK)
    s_chunks += s_chunks % 2
    s_len = s_chunks * CUMSUM_CHUNK
    hist_p = jnp.zeros((bp, HIST_ROWS, POOL_WIDTH), F32)

    y_p, y_s = x_prompt, x_sample
    slabs_p = slabs_s = None
    for l in range(depth):
        w = _layer_weights(depth, l, w_in, b_f, q_gain, k_gain, w_pool, pool_scale, w_out, w_up, w_down)
        sh1, sc1, g1, sh2, sc2, g2 = _mod_parts(mod[l, :bp])
        slabs_p, (kbf, qt, vt, logft, py) = _in_proj(
            y_p, sh1, sc1, hist_p, w, slabs_p, l, bb=1, tt=tm_in, pos0=0, transposed=True)
        ft = _cumsum_time(logft, rows_per_block=bp * HEADS_PAD * (seq // CUMSUM_CHUNK))
        att = _attn_prompt(qt, kbf, vt, ft, jnp.swapaxes(ft, 1, 2), q_gain[l], k_gain[l],
                           tq=tq, tk=tk_p)
        y_p = _out_mlp(y_p, py, att, g1, sh2, sc2, g2, w, bb=1, tt=tm_out, name="out_mlp_prompt")

        sh1, sc1, g1, sh2, sc2, g2 = _mod_parts(mod[l, bp:n_c])
        hist_s = jnp.pad(cache_pool[l], ((0, 0), (HIST_ROWS - POOL_HIST, 0), (0, 0)))
        slabs_s, (_, q_s, logft_s, py_s) = _in_proj(
            y_s, sh1, sc1, hist_s, w, slabs_s, l, bb=bs, tt=dec_seq, pos0=past, transposed=False)
        new_t = jnp.swapaxes(logft_s.reshape(HEADS_PAD, bs, dec_seq), 0, 1)
        cache_t = jnp.pad(jnp.swapaxes(cache_logf[l], 1, 2), ((0, 0), (0, HEADS_PAD - N_HEADS), (0, 0)))
        seq_s = jnp.concatenate(
            [cache_t, new_t, jnp.zeros((bs, HEADS_PAD, s_len - past - dec_seq), F32)], axis=-1)
        f_all = _cumsum_time(seq_s, rows_per_block=4 * HEADS_PAD * s_chunks)
        fq_col = f_all[:, :N_HEADS, past:past + dec_seq].reshape(bs, N_HEADS * dec_seq, 1)
        att_s = _attn_sample(q_s, ck, cv, slabs_s[0], slabs_s[1], f_all, fq_col, l, tk=tk_cache)
        y_s = _out_mlp(y_s, py_s, att_s, g1, sh2, sc2, g2, w, bb=bs, tt=dec_seq, name="out_mlp_sample")

    k_p, v_p, f_p, p_p = slabs_p
    k_s, v_s, f_s, p_s = slabs_s
    return (y_p, y_s,
            k_p.reshape(depth, bp, seq, N_HEADS, HEAD_DIM), v_p.reshape(depth, bp, seq, N_HEADS, HEAD_DIM),
            f_p, p_p,
            k_s.reshape(depth, bs, dec_seq, N_HEADS, HEAD_DIM), v_s.reshape(depth, bs, dec_seq, N_HEADS, HEAD_DIM),
            f_s, p_s)
```

```python
import functools

import numpy as np
import jax
import jax.numpy as jnp
from jax import lax
from jax.experimental import pallas as pl
from jax.experimental.pallas import tpu as pltpu

D_MODEL = 1024
POOL_WIDTH = 256
POOL_GROUP_DIM = 64
POOL_WINDOWS = (2, 4, 8, 16)
POOL_HIST = 15
HIST_ROWS = 16
ATT_WIDTH = 768
HEAD_DIM = 64
N_HEADS = 12
HEADS_PAD = 16
D_FF = 4096
EPS = 1e-6
NEG = -1e30
LOG2E = 1.4426950408889634
BOUND_SLACK_OCTAVES = 64.0

LANES = 128
CUMSUM_CHUNK = 256
VMEM_LIMIT = 56 * 1024 * 1024

F32 = jnp.float32
BF16 = jnp.bfloat16


def _dot(a, b):
    return jnp.dot(a, b, preferred_element_type=F32)


def _dot_nt(a, b):
    return lax.dot_general(a, b, (((1,), (1,)), ((), ())), preferred_element_type=F32)


def _split_dot(a, b):
    hi = a.astype(BF16)
    lo = (a - hi.astype(F32)).astype(BF16)
    return _dot(hi, b) + _dot(lo, b)


def _resident(shape):
    nd = len(shape)
    return pl.BlockSpec(shape, lambda *_: (0,) * nd, pipeline_mode=pl.Buffered(1))


def _resident_layer(shape, layer):
    nd = len(shape)
    return pl.BlockSpec((None,) + tuple(shape), lambda *_: (layer,) + (0,) * nd,
                        pipeline_mode=pl.Buffered(1))


def _mod_kernel(c_ref, w_ref, b_ref, o_ref):
    c = c_ref[...]
    s = c / (1.0 + jnp.exp(-c))
    o_ref[...] = _dot(s.astype(BF16), w_ref[...].astype(BF16)) + b_ref[...]


def _modulation(c_all, w_mod, b_mod):
    depth, _, n = w_mod.shape
    rows = c_all.shape[0]
    tn = 1536
    return pl.pallas_call(
        _mod_kernel,
        grid=(depth, n // tn),
        in_specs=[
            pl.BlockSpec((rows, D_MODEL), lambda l, j: (0, 0)),
            pl.BlockSpec((None, D_MODEL, tn), lambda l, j: (l, 0, j)),
            pl.BlockSpec((None, 1, tn), lambda l, j: (l, 0, j)),
        ],
        out_specs=pl.BlockSpec((None, rows, tn), lambda l, j: (l, 0, j)),
        out_shape=jax.ShapeDtypeStruct((depth, rows, n), F32),
        compiler_params=pltpu.CompilerParams(
            dimension_semantics=("arbitrary", "arbitrary"), vmem_limit_bytes=VMEM_LIMIT),
        name="modulation",
    )(c_all, w_mod, b_mod.reshape(depth, 1, n))


def _head_rms(a, ind, ind_t):
    ssum = _dot((a * a).astype(BF16), ind)
    r = lax.rsqrt(ssum * (1.0 / HEAD_DIM) + EPS)
    return a * _split_dot(r, ind_t)


def _log_sigmoid(x):
    return jnp.minimum(x, 0.0) - jnp.log1p(jnp.exp(-jnp.abs(x)))


def _in_proj_kernel(*refs, n_alias, bb, tt, pos0, transposed):
    refs = refs[n_alias:]
    (x_ref, sh_ref, sc_ref, hist_ref, win_ref, wf_ref, wft_ref,
     bfr_ref, bfc_ref, qg_ref, kg_ref, ind_ref, indt_ref, wp_ref, ps_ref,
     k32_ref, v32_ref, logf_ref, pstate_ref, kbf_ref, q_ref) = refs[:21]
    vt_ref = refs[21] if transposed else None
    o = POOL_WIDTH
    wu_ref = win_ref.at[:, 0:o]
    wq_ref = win_ref.at[:, o:o + ATT_WIDTH]
    wk_ref = win_ref.at[:, o + ATT_WIDTH:o + 2 * ATT_WIDTH]
    wv_ref = win_ref.at[:, o + 2 * ATT_WIDTH:o + 3 * ATT_WIDTH]
    logft_ref, py_ref, z_ref = refs[-3:]
    ti = pl.program_id(1)
    m = bb * tt

    x = x_ref[...]
    ms = jnp.mean(x * x, axis=-1, keepdims=True)
    h = (x * lax.rsqrt(ms + EPS)) * (1.0 + sc_ref[...]) + sh_ref[...]
    hb = h.reshape(m, D_MODEL).astype(BF16)

    ind = ind_ref[...]
    ind_t = indt_ref[...]
    q = _head_rms(_dot(hb, wq_ref[...]), ind, ind_t) * qg_ref[...]
    k = _head_rms(_dot(hb, wk_ref[...]), ind, ind_t) * kg_ref[...]
    v = _dot(hb, wv_ref[...])
    k32_ref[...] = k.reshape(bb, tt, ATT_WIDTH)
    v32_ref[...] = v.reshape(bb, tt, ATT_WIDTH)
    kbf_ref[...] = k.reshape(bb, tt, ATT_WIDTH).astype(BF16)
    qs = q * (HEAD_DIM ** -0.5 * (LOG2E if transposed else 1.0))
    if transposed:
        q_ref[...] = qs.T.astype(BF16)
        vt_ref[...] = v.T.astype(BF16)
    else:
        q_ref[...] = qs.reshape(bb, tt, ATT_WIDTH).astype(BF16)

    fl = _dot(hb, wf_ref[...]) + bfr_ref[...]
    logf_ref[...] = _log_sigmoid(fl)[:, :N_HEADS].reshape(bb, tt, N_HEADS)
    flt = _dot_nt(wft_ref[...], hb) + bfc_ref[...]
    logft_ref[...] = _log_sigmoid(flt)

    u3 = _dot(hb, wu_ref[...]).reshape(bb, tt, POOL_WIDTH)

    @pl.when(ti == 0)
    def _():
        z_ref[:, 0:HIST_ROWS, :] = hist_ref[...]

    z_ref[:, HIST_ROWS:HIST_ROWS + tt, :] = u3
    acc = u3
    sums = {}
    for i in range(1, max(POOL_WINDOWS)):
        acc = acc + z_ref[:, HIST_ROWS - i:HIST_ROWS - i + tt, :]
        if i + 1 in POOL_WINDOWS:
            sums[i + 1] = acc
    group = lax.broadcasted_iota(jnp.int32, (1, 1, POOL_WIDTH), 2) // POOL_GROUP_DIM
    win = sums[POOL_WINDOWS[-1]]
    width = jnp.full((1, 1, POOL_WIDTH), float(POOL_WINDOWS[-1]), F32)
    for g in range(len(POOL_WINDOWS) - 2, -1, -1):
        win = jnp.where(group == g, sums[POOL_WINDOWS[g]], win)
        width = jnp.where(group == g, float(POOL_WINDOWS[g]), width)
    pos = (pos0 + ti * tt + lax.broadcasted_iota(jnp.int32, (1, tt, 1), 1)).astype(F32)
    count = jnp.minimum(pos + 1.0, width)
    mixed = win / count - u3
    y = _dot(mixed.reshape(m, POOL_WIDTH).astype(BF16), wp_ref[...]) * ps_ref[...]
    py_ref[...] = y.reshape(bb, tt, POOL_WIDTH).astype(BF16)

    pstate_ref[...] = z_ref[:, tt + 1:tt + HIST_ROWS, :]
    z_ref[:, 0:HIST_ROWS, :] = z_ref[:, tt:tt + HIST_ROWS, :]


def _in_proj(x, sh, sc, hist, w, slabs, layer, *, bb, tt, pos0, transposed):
    b, t, _ = x.shape
    depth = w["depth"]
    nb, nt = b // bb, t // tt
    m = bb * tt
    n_alias = 0 if slabs is None else 4

    def per_b(shape):
        return pl.BlockSpec(shape, lambda i, j: (i,) + (0,) * (len(shape) - 1))

    in_specs = [pl.BlockSpec(memory_space=pl.ANY)] * n_alias + [
        pl.BlockSpec((bb, tt, D_MODEL), lambda i, j: (i, j, 0)),
        per_b((bb, 1, D_MODEL)), per_b((bb, 1, D_MODEL)), per_b((bb, HIST_ROWS, POOL_WIDTH)),
        _resident_layer((D_MODEL, w["w_in"].shape[-1]), layer),
        _resident((D_MODEL, LANES)), _resident((HEADS_PAD, D_MODEL)),
        _resident((1, LANES)), _resident((HEADS_PAD, 1)),
        _resident((1, ATT_WIDTH)), _resident((1, ATT_WIDTH)),
        _resident((ATT_WIDTH, LANES)), _resident((LANES, ATT_WIDTH)),
        _resident((POOL_WIDTH, POOL_WIDTH)), _resident((1, POOL_WIDTH)),
    ]
    slab_shapes = [
        jax.ShapeDtypeStruct((depth, b, t, ATT_WIDTH), F32),
        jax.ShapeDtypeStruct((depth, b, t, ATT_WIDTH), F32),
        jax.ShapeDtypeStruct((depth, b, t, N_HEADS), F32),
        jax.ShapeDtypeStruct((depth, b, POOL_HIST, POOL_WIDTH), F32),
    ]
    slab_specs = [
        pl.BlockSpec((None, bb, tt, ATT_WIDTH), lambda i, j: (layer, i, j, 0)),
        pl.BlockSpec((None, bb, tt, ATT_WIDTH), lambda i, j: (layer, i, j, 0)),
        pl.BlockSpec((None, bb, tt, N_HEADS), lambda i, j: (layer, i, j, 0)),
        pl.BlockSpec((None, bb, POOL_HIST, POOL_WIDTH), lambda i, j: (layer, i, 0, 0)),
    ]
    if transposed:
        q_shape = jax.ShapeDtypeStruct((b, ATT_WIDTH, t), BF16)
        q_spec = pl.BlockSpec((None, ATT_WIDTH, tt), lambda i, j: (i, 0, j))
    else:
        q_shape = jax.ShapeDtypeStruct((b, t, ATT_WIDTH), BF16)
        q_spec = pl.BlockSpec((bb, tt, ATT_WIDTH), lambda i, j: (i, j, 0))
    out_shape = slab_shapes + [
        jax.ShapeDtypeStruct((b, t, ATT_WIDTH), BF16),
        q_shape,
    ] + ([q_shape] if transposed else []) + [
        jax.ShapeDtypeStruct((nb, HEADS_PAD, nt * m), F32),
        jax.ShapeDtypeStruct((b, t, POOL_WIDTH), BF16),
    ]
    out_specs = slab_specs + [
        pl.BlockSpec((bb, tt, ATT_WIDTH), lambda i, j: (i, j, 0)),
        q_spec,
    ] + ([q_spec] if transposed else []) + [
        pl.BlockSpec((None, HEADS_PAD, m), lambda i, j: (i, 0, j)),
        pl.BlockSpec((bb, tt, POOL_WIDTH), lambda i, j: (i, j, 0)),
    ]
    args = ([] if slabs is None else list(slabs)) + [
        x, sh, sc, hist, w["w_in"], w["w_f"], w["w_ft"],
        w["bf_row"], w["bf_col"], w["qg"], w["kg"], w["ind"], w["ind_t"], w["w_pool"],
        w["pool_scale"],
    ]
    outs = pl.pallas_call(
        functools.partial(_in_proj_kernel, n_alias=n_alias, bb=bb, tt=tt, pos0=pos0,
                          transposed=transposed),
        grid=(nb, nt),
        in_specs=in_specs,
        out_specs=out_specs,
        out_shape=out_shape,
        scratch_shapes=[pltpu.VMEM((bb, tt + HIST_ROWS, POOL_WIDTH), F32)],
        input_output_aliases={i: i for i in range(n_alias)},
        compiler_params=pltpu.CompilerParams(
            dimension_semantics=("arbitrary", "arbitrary"), vmem_limit_bytes=VMEM_LIMIT),
        name="in_proj_prompt" if transposed else "in_proj_sample",
    )(*args)
    return outs[:4], outs[4:]


def _cumsum_kernel(x_ref, gcol_ref, grow_ref, o_ref):
    x = x_ref[...]
    rb, c = x.shape
    tri = (lax.broadcasted_iota(jnp.int32, (c, c), 0)
           <= lax.broadcasted_iota(jnp.int32, (c, c), 1)).astype(F32)
    y = jnp.dot(x, tri, precision=lax.Precision.HIGHEST, preferred_element_type=F32)
    tot = jnp.broadcast_to(y[:, c - 1:c], (rb, LANES))
    before = ((gcol_ref[...] == grow_ref[...])
              & (lax.broadcasted_iota(jnp.int32, (rb, rb), 1)
                 < lax.broadcasted_iota(jnp.int32, (rb, rb), 0))).astype(F32)
    off = jnp.dot(before, tot, precision=lax.Precision.HIGHEST, preferred_element_type=F32)
    o_ref[...] = y + off[:, 0:1]


def _cumsum_time(x, rows_per_block):
    lead, length = x.shape[:-1], x.shape[-1]
    chunks = length // CUMSUM_CHUNK
    r = int(np.prod(lead)) * chunks
    rb = rows_per_block
    gid = (np.arange(rb) // chunks).astype(np.int32)
    out = pl.pallas_call(
        _cumsum_kernel,
        grid=(r // rb,),
        in_specs=[
            pl.BlockSpec((rb, CUMSUM_CHUNK), lambda i: (i, 0)),
            _resident((rb, 1)), _resident((1, rb)),
        ],
        out_specs=pl.BlockSpec((rb, CUMSUM_CHUNK), lambda i: (i, 0)),
        out_shape=jax.ShapeDtypeStruct((r, CUMSUM_CHUNK), F32),
        compiler_params=pltpu.CompilerParams(
            dimension_semantics=("arbitrary",), vmem_limit_bytes=VMEM_LIMIT),
        name="cumsum_time",
    )(x.reshape(r, CUMSUM_CHUNK), jnp.asarray(gid).reshape(rb, 1), jnp.asarray(gid).reshape(1, rb))
    return out.reshape(x.shape)


def _fill_key_forget_sums(fcol_ref, fkb_ref, hp):
    chunk = 512
    lane = lax.broadcasted_iota(jnp.int32, (chunk, HEADS_PAD), 1)
    for hh in range(2):
        def fill(c, carry, hh=hh):
            r0 = pl.multiple_of(c * chunk, chunk)
            blk = fcol_ref[pl.ds(r0, chunk), :]
            col = jnp.sum(jnp.where(lane == 2 * hp + hh, blk, 0.0), axis=1, keepdims=True)
            fkb_ref[hh, pl.ds(r0, chunk), :] = jnp.broadcast_to(col * LOG2E, (chunk, LANES))
            return carry
        lax.fori_loop(0, fcol_ref.shape[0] // chunk, fill, 0)


def _pair_query_columns(qt):
    row_head = lax.broadcasted_iota(jnp.int32, (2 * HEAD_DIM, 1), 0) // HEAD_DIM
    zero = jnp.zeros_like(qt)
    return jnp.concatenate(
        [jnp.where(row_head == 0, qt, zero), jnp.where(row_head == 1, qt, zero)], axis=1)


def _attn_prompt_bounded_kernel(kmax_ref, qt_ref, k_ref, vt_ref, ft_ref, fcol_ref, o_ref,
                                fkb_ref, p_ref, *, tq, tk):
    hp = pl.program_id(1)
    qi = pl.program_id(2)
    n_full = (qi * tq) // tk

    @pl.when(qi == 0)
    def _():
        _fill_key_forget_sums(fcol_ref, fkb_ref, hp)

    causal = (n_full * tk + lax.broadcasted_iota(jnp.int32, (tk, tq), 0)
              <= qi * tq + lax.broadcasted_iota(jnp.int32, (tk, tq), 1))
    ones_rows = jnp.ones((HEADS_PAD, tk), BF16)
    qt = qt_ref[...]
    q2 = _pair_query_columns(qt)
    qsq = jnp.square(qt.astype(F32))
    shift = []
    for hh in range(2):
        qnorm = jnp.sqrt(jnp.sum(qsq[hh * HEAD_DIM:(hh + 1) * HEAD_DIM], axis=0, keepdims=True))
        shift.append(ft_ref[pl.ds(2 * hp + hh, 1), :] * LOG2E - qnorm * kmax_ref[0, 0])

    def weights(j, masked):
        k0 = pl.multiple_of(j * tk, tk)
        s2 = _dot(k_ref[pl.ds(k0, tk), :], q2)
        ps = []
        for hh in range(2):
            fk = fkb_ref[hh, pl.ds(k0, tk), :]
            a = (s2[:, hh * tq:(hh + 1) * tq] - jnp.concatenate([fk] * (tq // LANES), axis=1)
                 + shift[hh])
            if masked:
                a = jnp.where(causal, a, NEG)
            ps.append(jnp.exp2(a).astype(BF16))
        return jnp.concatenate(ps, axis=1)

    def accumulate(j, acc, p2):
        k0 = pl.multiple_of(j * tk, tk)
        out = []
        for hh in range(2):
            vb = jnp.concatenate(
                [vt_ref[pl.ds(hh * HEAD_DIM, HEAD_DIM), pl.ds(k0, tk)], ones_rows], axis=0)
            out.append(acc[hh] + _dot(vb, p2[:, hh * tq:(hh + 1) * tq]))
        return tuple(out)

    p_ref[...] = weights(n_full, True)

    def step(j, acc):
        p_next = weights(j, False)
        acc = accumulate(jnp.where(j == 0, n_full, j - 1), acc, p_ref[...])
        p_ref[...] = p_next
        return acc

    acc0 = jnp.zeros((HEAD_DIM + HEADS_PAD, tq), F32)
    acc = lax.fori_loop(0, n_full, step, (acc0, acc0))
    acc = accumulate(jnp.maximum(n_full - 1, 0), acc, p_ref[...])
    outs = [a[0:HEAD_DIM] / a[HEAD_DIM:HEAD_DIM + 1] for a in acc]
    o_ref[...] = jnp.concatenate(outs, axis=0).T.astype(BF16)


def _attn_prompt_kernel(qt_ref, k_ref, vt_ref, ft_ref, fcol_ref, o_ref, fkb_ref, s_ref, p_ref,
                        *, tq, tk):
    hp = pl.program_id(1)
    qi = pl.program_id(2)
    n_full = (qi * tq) // tk

    @pl.when(qi == 0)
    def _():
        _fill_key_forget_sums(fcol_ref, fkb_ref, hp)

    causal = (n_full * tk + lax.broadcasted_iota(jnp.int32, (tk, tq), 0)
              <= qi * tq + lax.broadcasted_iota(jnp.int32, (tk, tq), 1))
    ones_rows = jnp.ones((HEADS_PAD, tk), BF16)
    q2 = _pair_query_columns(qt_ref[...])
    fq = [ft_ref[pl.ds(2 * hp + hh, 1), :] * LOG2E for hh in range(2)]

    def biased_scores(j):
        k0 = pl.multiple_of(j * tk, tk)
        s2 = _dot(k_ref[pl.ds(k0, tk), :], q2)
        a, cmax = [], []
        for hh in range(2):
            fk = fkb_ref[hh, pl.ds(k0, tk), :]
            ah = s2[:, hh * tq:(hh + 1) * tq] - jnp.concatenate([fk] * (tq // LANES), axis=1)
            a.append(ah)
            cmax.append(jnp.max(ah, axis=0, keepdims=True))
        return jnp.concatenate(a, axis=1), cmax

    def weights(a2, cmax, m_run):
        ps, alphas, ms = [], [], []
        for hh in range(2):
            m_new = jnp.maximum(m_run[hh], cmax[hh] + fq[hh])
            alphas.append(jnp.exp2(m_run[hh] - m_new))
            ps.append(jnp.exp2(a2[:, hh * tq:(hh + 1) * tq] + (fq[hh] - m_new)).astype(BF16))
            ms.append(m_new)
        return ps, alphas, ms

    def accumulate(j, acc, p, alpha):
        k0 = pl.multiple_of(j * tk, tk)
        out = []
        for hh in range(2):
            vb = jnp.concatenate(
                [vt_ref[pl.ds(hh * HEAD_DIM, HEAD_DIM), pl.ds(k0, tk)], ones_rows], axis=0)
            out.append(alpha[hh] * acc[hh] + _dot(vb, p[hh]))
        return out

    a0, cmax0 = biased_scores(0)
    s_ref[...] = a0
    p_ref[...] = jnp.zeros(p_ref.shape, BF16)

    def step(j, carry):
        acc, m_run, alpha, cmax = carry
        a_cur = s_ref[...]
        p_prev = p_ref[...]
        a_next, cmax_next = biased_scores(j + 1)
        p, alpha_new, m_new = weights(a_cur, cmax, m_run)
        acc = accumulate(jnp.maximum(j - 1, 0), acc, [p_prev[:, :tq], p_prev[:, tq:]], alpha)
        s_ref[...] = a_next
        p_ref[...] = jnp.concatenate(p, axis=1)
        return tuple(acc), tuple(m_new), tuple(alpha_new), tuple(cmax_next)

    acc0 = jnp.zeros((HEAD_DIM + HEADS_PAD, tq), F32)
    m0 = jnp.full((1, tq), NEG, F32)
    one = jnp.ones((1, tq), F32)
    acc, m_run, alpha, _ = lax.fori_loop(
        0, n_full, step, ((acc0, acc0), (m0, m0), (one, one), tuple(cmax0)))
    p_prev = p_ref[...]
    a_diag = s_ref[...]
    a_diag = jnp.concatenate(
        [jnp.where(causal, a_diag[:, hh * tq:(hh + 1) * tq], NEG) for hh in range(2)], axis=1)
    cmax = [jnp.max(a_diag[:, hh * tq:(hh + 1) * tq], axis=0, keepdims=True) for hh in range(2)]
    p, alpha_new, _ = weights(a_diag, cmax, m_run)
    acc = accumulate(jnp.maximum(n_full - 1, 0), acc, [p_prev[:, :tq], p_prev[:, tq:]], alpha)
    acc = accumulate(n_full, acc, p, alpha_new)
    outs = [a[0:HEAD_DIM] / a[HEAD_DIM:HEAD_DIM + 1] for a in acc]
    o_ref[...] = jnp.concatenate(outs, axis=0).T.astype(BF16)


def _attn_prompt(qt, kbf, vt, ft, fcol, q_gain, k_gain, *, tq, tk):
    b, t, _ = kbf.shape
    pairs = N_HEADS // 2
    in_specs = [
        pl.BlockSpec((None, 2 * HEAD_DIM, tq), lambda i, p, j: (i, p, j)),
        pl.BlockSpec((None, t, 2 * HEAD_DIM), lambda i, p, j: (i, 0, p)),
        pl.BlockSpec((None, 2 * HEAD_DIM, t), lambda i, p, j: (i, p, 0)),
        pl.BlockSpec((None, HEADS_PAD, tq), lambda i, p, j: (i, 0, j)),
        pl.BlockSpec((None, t, HEADS_PAD), lambda i, p, j: (i, 0, 0)),
    ]
    common = dict(
        grid=(b, pairs, t // tq),
        out_specs=pl.BlockSpec((None, tq, 2 * HEAD_DIM), lambda i, p, j: (i, j, p)),
        out_shape=jax.ShapeDtypeStruct((b, t, ATT_WIDTH), BF16),
        compiler_params=pltpu.CompilerParams(
            dimension_semantics=("arbitrary", "arbitrary", "arbitrary"),
            vmem_limit_bytes=VMEM_LIMIT),
    )
    key_sums = pltpu.VMEM((2, t, LANES), F32)
    weights_tile = pltpu.VMEM((tk, 2 * tq), BF16)

    def running_max(_, *args):
        return pl.pallas_call(
            functools.partial(_attn_prompt_kernel, tq=tq, tk=tk),
            in_specs=in_specs,
            scratch_shapes=[key_sums, pltpu.VMEM((tk, 2 * tq), F32), weights_tile],
            name="attn_prompt", **common)(*args)

    def bounded(kmax, *args):
        return pl.pallas_call(
            functools.partial(_attn_prompt_bounded_kernel, tq=tq, tk=tk),
            in_specs=[pl.BlockSpec(memory_space=pltpu.SMEM)] + in_specs,
            scratch_shapes=[key_sums, weights_tile],
            name="attn_prompt_bounded", **common)(kmax, *args)

    qmax = 1.01 * LOG2E * jnp.max(jnp.abs(q_gain))
    kmax = 1.01 * HEAD_DIM ** 0.5 * jnp.max(jnp.abs(k_gain))
    return lax.cond(2.0 * qmax * kmax <= BOUND_SLACK_OCTAVES, bounded, running_max,
                    kmax.reshape(1, 1).astype(F32), qt, kbf, vt, ft, fcol)


def _attn_sample_kernel(q_ref, ck_ref, cv_ref, kn_ref, vn_ref, f_ref, lrow_ref, lcol_ref, o_ref,
                        qbd_ref, m_ref, l_ref, acc_ref, fq_ref, fnew_ref, *, tk, past, tt):
    j = pl.program_id(1)
    rows = N_HEADS * tt
    own = ((lax.broadcasted_iota(jnp.int32, (rows, ATT_WIDTH), 0) // tt)
           == (lax.broadcasted_iota(jnp.int32, (rows, ATT_WIDTH), 1) // HEAD_DIM))

    @pl.when(j == 0)
    def _():
        q = q_ref[...]
        qbd_ref[...] = jnp.where(own, jnp.concatenate([q] * N_HEADS, axis=0), jnp.zeros((), BF16))
        m_ref[...] = jnp.full(m_ref.shape, NEG, F32)
        l_ref[...] = jnp.zeros(l_ref.shape, F32)
        acc_ref[...] = jnp.zeros(acc_ref.shape, F32)
        f_last = f_ref[:, past - 1:past]
        upper = (lax.broadcasted_iota(jnp.int32, (tt, tt), 0)
                 <= lax.broadcasted_iota(jnp.int32, (tt, tt), 1)).astype(F32)
        run = jnp.dot(lrow_ref[...], upper, precision=lax.Precision.HIGHEST,
                      preferred_element_type=F32)
        fnew_ref[...] = f_last + run[:N_HEADS]
        r_i = lax.broadcasted_iota(jnp.int32, (rows, rows), 0)
        c_i = lax.broadcasted_iota(jnp.int32, (rows, rows), 1)
        lower = ((r_i // tt == c_i // tt) & (c_i <= r_i)).astype(F32)
        run_col = jnp.dot(lower, jnp.broadcast_to(lcol_ref[...], (rows, LANES)),
                          precision=lax.Precision.HIGHEST, preferred_element_type=F32)
        f_last_rows = jnp.broadcast_to(f_last[:, None, :], (N_HEADS, tt, 1)).reshape(rows, 1)
        fq_ref[...] = f_last_rows + run_col[:, 0:1]

    fq3 = fq_ref[...].reshape(N_HEADS, tt, 1)

    def update(kb, vb, fk, mask, time_on_lanes):
        if time_on_lanes:
            n = kb.shape[1]
            s = _dot(qbd_ref[...], kb.astype(BF16))
        else:
            n = kb.shape[0]
            s = _dot_nt(qbd_ref[...], kb.astype(BF16))
        s = s.reshape(N_HEADS, tt, n) + fq3 - fk[:, None, :]
        if mask is not None:
            s = jnp.where(mask, s, NEG)
        m_old = m_ref[...].reshape(N_HEADS, tt, 1)
        m_new = jnp.maximum(m_old, jnp.max(s, axis=-1, keepdims=True))
        alpha = jnp.exp(m_old - m_new)
        p = jnp.exp(s - m_new)
        l_ref[...] = (alpha * l_ref[...].reshape(N_HEADS, tt, 1)
                      + jnp.sum(p, axis=-1, keepdims=True)).reshape(rows, 1)
        m_ref[...] = m_new.reshape(rows, 1)
        pb = p.reshape(rows, n).astype(BF16)
        pv = _dot_nt(pb, vb.astype(BF16)) if time_on_lanes else _dot(pb, vb.astype(BF16))
        acc_ref[...] = alpha.reshape(rows, 1) * acc_ref[...] + pv

    k0 = pl.multiple_of(j * tk, tk)
    update(ck_ref[...], cv_ref[...], f_ref[:, pl.ds(k0, tk)], None, True)

    @pl.when(j == pl.num_programs(1) - 1)
    def _():
        mask = (lax.broadcasted_iota(jnp.int32, (1, tt, tt), 2)
                <= lax.broadcasted_iota(jnp.int32, (1, tt, tt), 1))
        update(kn_ref[...], vn_ref[...], fnew_ref[...], mask, False)
        o = jnp.where(own, acc_ref[...] / l_ref[...], 0.0).reshape(N_HEADS, tt, ATT_WIDTH)
        o_ref[...] = jnp.sum(o, axis=0).astype(BF16)


def _attn_sample(q, cache_k, cache_v, k_new, v_new, f_past, logf_row, logf_col, layer, *, tk):
    nb, tt, _ = q.shape
    past = cache_k.shape[3]
    rows = N_HEADS * tt
    return pl.pallas_call(
        functools.partial(_attn_sample_kernel, tk=tk, past=past, tt=tt),
        grid=(nb, past // tk),
        in_specs=[
            pl.BlockSpec((None, tt, ATT_WIDTH), lambda i, j: (i, 0, 0)),
            pl.BlockSpec((None, None, ATT_WIDTH, tk), lambda i, j: (layer, i, 0, j)),
            pl.BlockSpec((None, None, ATT_WIDTH, tk), lambda i, j: (layer, i, 0, j)),
            pl.BlockSpec((None, None, tt, ATT_WIDTH), lambda i, j: (layer, i, 0, 0)),
            pl.BlockSpec((None, None, tt, ATT_WIDTH), lambda i, j: (layer, i, 0, 0)),
            pl.BlockSpec((None, None, N_HEADS, past), lambda i, j: (layer, i, 0, 0)),
            pl.BlockSpec((None, HEADS_PAD, tt), lambda i, j: (i, 0, 0)),
            pl.BlockSpec((None, rows, 1), lambda i, j: (i, 0, 0)),
        ],
        out_specs=pl.BlockSpec((None, tt, ATT_WIDTH), lambda i, j: (i, 0, 0)),
        out_shape=jax.ShapeDtypeStruct((nb, tt, ATT_WIDTH), BF16),
        scratch_shapes=[
            pltpu.VMEM((rows, ATT_WIDTH), BF16),
            pltpu.VMEM((rows, 1), F32),
            pltpu.VMEM((rows, 1), F32),
            pltpu.VMEM((rows, ATT_WIDTH), F32),
            pltpu.VMEM((rows, 1), F32),
            pltpu.VMEM((N_HEADS, tt), F32),
        ],
        compiler_params=pltpu.CompilerParams(
            dimension_semantics=("arbitrary", "arbitrary"), vmem_limit_bytes=VMEM_LIMIT),
        name="attn_sample",
    )(q, cache_k, cache_v, k_new, v_new, f_past, logf_row, logf_col)


def _out_mlp_kernel(x_ref, py_ref, att_ref, g1_ref, sh_ref, sc_ref, g2_ref,
                    wo_ref, wup_ref, wdn_ref, y_ref, *, bb, tt, ff_chunk):
    m = bb * tt
    mix = (_dot(py_ref[...].reshape(m, POOL_WIDTH), wo_ref[0:POOL_WIDTH, :])
           + _dot(att_ref[...].reshape(m, ATT_WIDTH), wo_ref[POOL_WIDTH:, :]))
    x1 = x_ref[...] + g1_ref[...] * mix.reshape(bb, tt, D_MODEL)
    ms = jnp.mean(x1 * x1, axis=-1, keepdims=True)
    h = (x1 * lax.rsqrt(ms + EPS)) * (1.0 + sc_ref[...]) + sh_ref[...]
    hb = h.reshape(m, D_MODEL).astype(BF16)
    ff = jnp.zeros((m, D_MODEL), F32)
    for c in range(D_FF // ff_chunk):
        up = _dot(hb, wup_ref[:, c * ff_chunk:(c + 1) * ff_chunk])
        act = jnp.square(jnp.maximum(up, 0.0)).astype(BF16)
        ff = ff + _dot(act, wdn_ref[c * ff_chunk:(c + 1) * ff_chunk, :])
    y_ref[...] = x1 + g2_ref[...] * ff.reshape(bb, tt, D_MODEL)


def _out_mlp(x, py, att, g1, sh, sc, g2, w, layer, *, bb, tt, name):
    b, t, _ = x.shape

    def tile(width):
        return pl.BlockSpec((bb, tt, width), lambda i, j: (i, j, 0))

    def per_b():
        return pl.BlockSpec((bb, 1, D_MODEL), lambda i, j: (i, 0, 0))

    return pl.pallas_call(
        functools.partial(_out_mlp_kernel, bb=bb, tt=tt, ff_chunk=1024),
        grid=(b // bb, t // tt),
        in_specs=[
            tile(D_MODEL), tile(POOL_WIDTH), tile(ATT_WIDTH),
            per_b(), per_b(), per_b(), per_b(),
            _resident_layer((POOL_WIDTH + ATT_WIDTH, D_MODEL), layer),
            _resident_layer((D_MODEL, D_FF), layer), _resident_layer((D_FF, D_MODEL), layer),
        ],
        out_specs=tile(D_MODEL),
        out_shape=jax.ShapeDtypeStruct((b, t, D_MODEL), F32),
        compiler_params=pltpu.CompilerParams(
            dimension_semantics=("arbitrary", "arbitrary"), vmem_limit_bytes=VMEM_LIMIT),
        name=name,
    )(x, py, att, g1, sh, sc, g2, w["w_out"], w["w_up"], w["w_down"])


def _layer_weights(depth, layer, stacked, w_in, b_f, q_gain, k_gain, w_pool, pool_scale):
    w_f = w_in[layer, :, POOL_WIDTH + 3 * ATT_WIDTH:]
    head_of_col = np.arange(ATT_WIDTH) // HEAD_DIM
    ind = (head_of_col[:, None] == np.arange(LANES)[None, :]).astype(np.float32)
    w_pool_bd = jnp.zeros((POOL_WIDTH, POOL_WIDTH), F32)
    for g in range(len(POOL_WINDOWS)):
        sl = slice(g * POOL_GROUP_DIM, (g + 1) * POOL_GROUP_DIM)
        w_pool_bd = w_pool_bd.at[sl, sl].set(w_pool[layer, g])
    return {
        "depth": depth,
        **stacked,
        "w_f": jnp.pad(w_f, ((0, 0), (0, LANES - N_HEADS))).astype(BF16),
        "w_ft": jnp.pad(w_f.T, ((0, HEADS_PAD - N_HEADS), (0, 0))).astype(BF16),
        "bf_row": jnp.pad(b_f[layer], (0, LANES - N_HEADS)).reshape(1, LANES),
        "bf_col": jnp.pad(b_f[layer], (0, HEADS_PAD - N_HEADS)).reshape(HEADS_PAD, 1),
        "qg": jnp.tile(q_gain[layer], N_HEADS).reshape(1, ATT_WIDTH),
        "kg": jnp.tile(k_gain[layer], N_HEADS).reshape(1, ATT_WIDTH),
        "ind": jnp.asarray(ind, BF16),
        "ind_t": jnp.asarray(ind.T, BF16),
        "w_pool": w_pool_bd.astype(BF16),
        "pool_scale": pool_scale[layer].reshape(1, POOL_WIDTH),
    }


def _mod_parts(mod_rows):
    b = mod_rows.shape[0]
    parts = mod_rows.reshape(b, 6, 1, D_MODEL)
    return [parts[:, i] for i in range(6)]


def kernel(x_prompt, x_sample, c_prompt, c_sample, cache_k, cache_v, cache_logf, cache_pool,
           w_mod, b_mod, w_in, b_f, q_gain, k_gain, w_pool, pool_scale, w_out, w_up, w_down):
    depth = w_mod.shape[0]
    bp, seq, _ = x_prompt.shape
    bs, dec_seq, _ = x_sample.shape
    past = cache_k.shape[2]
    tq = 512
    tk_p = 1024
    tm_in = 512
    tm_out = 512
    tk_cache = 2048

    n_c = bp + bs
    c_rows = -(-n_c // 8) * 8
    c_all = jnp.pad(jnp.concatenate([c_prompt, c_sample], axis=0), ((0, c_rows - n_c), (0, 0)))
    mod = _modulation(c_all, w_mod, b_mod)

    ck = jnp.transpose(cache_k, (0, 1, 3, 4, 2)).reshape(depth, bs, ATT_WIDTH, past)
    cv = jnp.transpose(cache_v, (0, 1, 3, 4, 2)).reshape(depth, bs, ATT_WIDTH, past)
    f_past = _cumsum_time(jnp.transpose(cache_logf, (0, 3, 1, 2)),
                          rows_per_block=64 * (past // CUMSUM_CHUNK))
    f_past = jnp.transpose(f_past, (0, 2, 1, 3))
    stacked = {"w_in": w_in.astype(BF16), "w_out": w_out.astype(BF16),
               "w_up": w_up.astype(BF16), "w_down": w_down.astype(BF16)}
    hist_p = jnp.zeros((bp, HIST_ROWS, POOL_WIDTH), F32)

    y_p, y_s = x_prompt, x_sample
    slabs_p = slabs_s = None
    for l in range(depth):
        w = _layer_weights(depth, l, stacked, w_in, b_f, q_gain, k_gain, w_pool, pool_scale)
        sh1, sc1, g1, sh2, sc2, g2 = _mod_parts(mod[l, :bp])
        slabs_p, (kbf, qt, vt, logft, py) = _in_proj(
            y_p, sh1, sc1, hist_p, w, slabs_p, l, bb=1, tt=tm_in, pos0=0, transposed=True)
        ft = _cumsum_time(logft, rows_per_block=bp * HEADS_PAD * (seq // CUMSUM_CHUNK))
        att = _attn_prompt(qt, kbf, vt, ft, jnp.swapaxes(ft, 1, 2), q_gain[l], k_gain[l],
                           tq=tq, tk=tk_p)
        y_p = _out_mlp(y_p, py, att, g1, sh2, sc2, g2, w, l, bb=1, tt=tm_out, name="out_mlp_prompt")

        sh1, sc1, g1, sh2, sc2, g2 = _mod_parts(mod[l, bp:n_c])
        hist_s = jnp.pad(cache_pool[l], ((0, 0), (HIST_ROWS - POOL_HIST, 0), (0, 0)))
        slabs_s, (_, q_s, logft_s, py_s) = _in_proj(
            y_s, sh1, sc1, hist_s, w, slabs_s, l, bb=bs, tt=dec_seq, pos0=past, transposed=False)
        logf_row = jnp.swapaxes(logft_s.reshape(HEADS_PAD, bs, dec_seq), 0, 1)
        logf_col = logf_row[:, :N_HEADS].reshape(bs, N_HEADS * dec_seq, 1)
        att_s = _attn_sample(q_s, ck, cv, slabs_s[0], slabs_s[1], f_past, logf_row, logf_col, l,
                             tk=tk_cache)
        y_s = _out_mlp(y_s, py_s, att_s, g1, sh2, sc2, g2, w, l, bb=bs, tt=dec_seq,
                       name="out_mlp_sample")

    k_p, v_p, f_p, p_p = slabs_p
    k_s, v_s, f_s, p_s = slabs_s
    return (y_p, y_s,
            k_p.reshape(depth, bp, seq, N_HEADS, HEAD_DIM), v_p.reshape(depth, bp, seq, N_HEADS, HEAD_DIM),
            f_p, p_p,
            k_s.reshape(depth, bs, dec_seq, N_HEADS, HEAD_DIM), v_s.reshape(depth, bs, dec_seq, N_HEADS, HEAD_DIM),
            f_s, p_s)
```

```python
import functools

import numpy as np
import jax
import jax.numpy as jnp
from jax import lax
from jax.experimental import pallas as pl
from jax.experimental.pallas import tpu as pltpu

D_MODEL = 1024
POOL_WIDTH = 256
POOL_GROUP_DIM = 64
POOL_WINDOWS = (2, 4, 8, 16)
POOL_HIST = 15
HIST_ROWS = 16
ATT_WIDTH = 768
HEAD_DIM = 64
N_HEADS = 12
HEADS_PAD = 16
D_FF = 4096
EPS = 1e-6
NEG = -1e30
LOG2E = 1.4426950408889634
BOUND_SLACK_OCTAVES = 64.0

LANES = 128
SUBLANES = 8
CUMSUM_CHUNK = 256
VMEM_LIMIT = 56 * 1024 * 1024

F32 = jnp.float32
BF16 = jnp.bfloat16


def _dot(a, b):
    return jnp.dot(a, b, preferred_element_type=F32)


def _dot_nt(a, b):
    return lax.dot_general(a, b, (((1,), (1,)), ((), ())), preferred_element_type=F32)


def _split_dot(a, b):
    hi = a.astype(BF16)
    lo = (a - hi.astype(F32)).astype(BF16)
    return _dot(hi, b) + _dot(lo, b)


def _resident(shape):
    nd = len(shape)
    return pl.BlockSpec(shape, lambda *_: (0,) * nd, pipeline_mode=pl.Buffered(1))


def _resident_layer(shape, layer):
    nd = len(shape)
    return pl.BlockSpec((None,) + tuple(shape), lambda *_: (layer,) + (0,) * nd,
                        pipeline_mode=pl.Buffered(1))


def _mod_kernel(c_ref, w_ref, b_ref, o_ref):
    c = c_ref[...]
    s = c / (1.0 + jnp.exp(-c))
    o_ref[...] = _dot(s.astype(BF16), w_ref[...].astype(BF16)) + b_ref[...]


def _modulation(c_all, w_mod, b_mod):
    depth, _, n = w_mod.shape
    rows = c_all.shape[0]
    tn = 1536
    return pl.pallas_call(
        _mod_kernel,
        grid=(depth, n // tn),
        in_specs=[
            pl.BlockSpec((rows, D_MODEL), lambda l, j: (0, 0)),
            pl.BlockSpec((None, D_MODEL, tn), lambda l, j: (l, 0, j)),
            pl.BlockSpec((None, 1, tn), lambda l, j: (l, 0, j)),
        ],
        out_specs=pl.BlockSpec((None, rows, tn), lambda l, j: (l, 0, j)),
        out_shape=jax.ShapeDtypeStruct((depth, rows, n), F32),
        compiler_params=pltpu.CompilerParams(
            dimension_semantics=("arbitrary", "arbitrary"), vmem_limit_bytes=VMEM_LIMIT),
        name="modulation",
    )(c_all, w_mod, b_mod.reshape(depth, 1, n))


def _head_rms(a, ind, ind_t):
    ssum = _dot((a * a).astype(BF16), ind)
    r = lax.rsqrt(ssum * (1.0 / HEAD_DIM) + EPS)
    return a * _split_dot(r, ind_t)


def _log_sigmoid(x):
    return jnp.minimum(x, 0.0) - jnp.log1p(jnp.exp(-jnp.abs(x)))


def _in_proj_kernel(*refs, n_alias, bb, tt, pos0, transposed):
    refs = refs[n_alias:]
    (x_ref, sh_ref, sc_ref, hist_ref, win_ref, wf_ref, wft_ref,
     bfr_ref, bfc_ref, qg_ref, kg_ref, ind_ref, indt_ref, wp_ref, ps_ref,
     k32_ref, v32_ref, logf_ref, pstate_ref, kbf_ref, q_ref) = refs[:21]
    vt_ref = refs[21] if transposed else None
    o = POOL_WIDTH
    wu_ref = win_ref.at[0:o, :]
    wq_ref = win_ref.at[o:o + ATT_WIDTH, :]
    wk_ref = win_ref.at[o + ATT_WIDTH:o + 2 * ATT_WIDTH, :]
    wv_ref = win_ref.at[o + 2 * ATT_WIDTH:o + 3 * ATT_WIDTH, :]
    logft_ref, py_ref, z_ref = refs[-3:]
    ti = pl.program_id(1)
    m = bb * tt

    x = x_ref[...]
    ms = jnp.mean(x * x, axis=-1, keepdims=True)
    h = (x * lax.rsqrt(ms + EPS)) * (1.0 + sc_ref[...]) + sh_ref[...]
    hb = h.reshape(m, D_MODEL).astype(BF16)

    ind = ind_ref[...]
    ind_t = indt_ref[...]
    q = _head_rms(_dot_nt(hb, wq_ref[...]), ind, ind_t) * qg_ref[...]
    k = _head_rms(_dot_nt(hb, wk_ref[...]), ind, ind_t) * kg_ref[...]
    v = _dot_nt(hb, wv_ref[...])
    k32_ref[...] = k.reshape(bb, tt, ATT_WIDTH)
    v32_ref[...] = v.reshape(bb, tt, ATT_WIDTH)
    kbf_ref[...] = k.reshape(bb, tt, ATT_WIDTH).astype(BF16)
    qs = q * (HEAD_DIM ** -0.5 * (LOG2E if transposed else 1.0))
    if transposed:
        q_ref[...] = qs.T.astype(BF16)
        vt_ref[...] = v.T.astype(BF16)
    else:
        q_ref[...] = qs.reshape(bb, tt, ATT_WIDTH).astype(BF16)

    fl = _dot(hb, wf_ref[...]) + bfr_ref[...]
    logf_ref[...] = _log_sigmoid(fl)[:, :N_HEADS].reshape(bb, tt, N_HEADS)
    flt = _dot_nt(wft_ref[...], hb) + bfc_ref[...]
    logft_ref[...] = _log_sigmoid(flt)

    u3 = _dot_nt(hb, wu_ref[...]).reshape(bb, tt, POOL_WIDTH)

    @pl.when(ti == 0)
    def _():
        z_ref[:, 0:HIST_ROWS, :] = hist_ref[...]

    z_ref[:, HIST_ROWS:HIST_ROWS + tt, :] = u3
    acc = u3
    sums = {}
    for i in range(1, max(POOL_WINDOWS)):
        acc = acc + z_ref[:, HIST_ROWS - i:HIST_ROWS - i + tt, :]
        if i + 1 in POOL_WINDOWS:
            sums[i + 1] = acc
    group = lax.broadcasted_iota(jnp.int32, (1, 1, POOL_WIDTH), 2) // POOL_GROUP_DIM
    win = sums[POOL_WINDOWS[-1]]
    width = jnp.full((1, 1, POOL_WIDTH), float(POOL_WINDOWS[-1]), F32)
    for g in range(len(POOL_WINDOWS) - 2, -1, -1):
        win = jnp.where(group == g, sums[POOL_WINDOWS[g]], win)
        width = jnp.where(group == g, float(POOL_WINDOWS[g]), width)
    pos = (pos0 + ti * tt + lax.broadcasted_iota(jnp.int32, (1, tt, 1), 1)).astype(F32)
    count = jnp.minimum(pos + 1.0, width)
    mixed = win / count - u3
    y = _dot(mixed.reshape(m, POOL_WIDTH).astype(BF16), wp_ref[...]) * ps_ref[...]
    py_ref[...] = y.reshape(bb, tt, POOL_WIDTH).astype(BF16)

    pstate_ref[...] = z_ref[:, tt + 1:tt + HIST_ROWS, :]
    z_ref[:, 0:HIST_ROWS, :] = z_ref[:, tt:tt + HIST_ROWS, :]


def _in_proj(x, sh, sc, hist, w, slabs, layer, *, bb, tt, pos0, transposed):
    b, t, _ = x.shape
    depth = w["depth"]
    nb, nt = b // bb, t // tt
    m = bb * tt
    n_alias = 0 if slabs is None else 4

    def per_b(shape):
        return pl.BlockSpec(shape, lambda i, j: (i,) + (0,) * (len(shape) - 1))

    in_specs = [pl.BlockSpec(memory_space=pl.ANY)] * n_alias + [
        pl.BlockSpec((bb, tt, D_MODEL), lambda i, j: (i, j, 0)),
        per_b((bb, 1, D_MODEL)), per_b((bb, 1, D_MODEL)), per_b((bb, HIST_ROWS, POOL_WIDTH)),
        _resident_layer((w["w_in"].shape[1], D_MODEL), layer),
        _resident((D_MODEL, LANES)), _resident((HEADS_PAD, D_MODEL)),
        _resident((1, LANES)), _resident((HEADS_PAD, 1)),
        _resident((1, ATT_WIDTH)), _resident((1, ATT_WIDTH)),
        _resident((ATT_WIDTH, LANES)), _resident((LANES, ATT_WIDTH)),
        _resident((POOL_WIDTH, POOL_WIDTH)), _resident((1, POOL_WIDTH)),
    ]
    slab_shapes = [
        jax.ShapeDtypeStruct((depth, b, t, ATT_WIDTH), F32),
        jax.ShapeDtypeStruct((depth, b, t, ATT_WIDTH), F32),
        jax.ShapeDtypeStruct((depth, b, t, N_HEADS), F32),
        jax.ShapeDtypeStruct((depth, b, POOL_HIST, POOL_WIDTH), F32),
    ]
    slab_specs = [
        pl.BlockSpec((None, bb, tt, ATT_WIDTH), lambda i, j: (layer, i, j, 0)),
        pl.BlockSpec((None, bb, tt, ATT_WIDTH), lambda i, j: (layer, i, j, 0)),
        pl.BlockSpec((None, bb, tt, N_HEADS), lambda i, j: (layer, i, j, 0)),
        pl.BlockSpec((None, bb, POOL_HIST, POOL_WIDTH), lambda i, j: (layer, i, 0, 0)),
    ]
    if transposed:
        q_shape = jax.ShapeDtypeStruct((b, ATT_WIDTH, t), BF16)
        q_spec = pl.BlockSpec((None, ATT_WIDTH, tt), lambda i, j: (i, 0, j))
    else:
        q_shape = jax.ShapeDtypeStruct((b, t, ATT_WIDTH), BF16)
        q_spec = pl.BlockSpec((bb, tt, ATT_WIDTH), lambda i, j: (i, j, 0))
    out_shape = slab_shapes + [
        jax.ShapeDtypeStruct((b, t, ATT_WIDTH), BF16),
        q_shape,
    ] + ([q_shape] if transposed else []) + [
        jax.ShapeDtypeStruct((nb, HEADS_PAD, nt * m), F32),
        jax.ShapeDtypeStruct((b, t, POOL_WIDTH), BF16),
    ]
    out_specs = slab_specs + [
        pl.BlockSpec((bb, tt, ATT_WIDTH), lambda i, j: (i, j, 0)),
        q_spec,
    ] + ([q_spec] if transposed else []) + [
        pl.BlockSpec((None, HEADS_PAD, m), lambda i, j: (i, 0, j)),
        pl.BlockSpec((bb, tt, POOL_WIDTH), lambda i, j: (i, j, 0)),
    ]
    args = ([] if slabs is None else list(slabs)) + [
        x, sh, sc, hist, w["w_in"], w["w_f"], w["w_ft"],
        w["bf_row"], w["bf_col"], w["qg"], w["kg"], w["ind"], w["ind_t"], w["w_pool"],
        w["pool_scale"],
    ]
    outs = pl.pallas_call(
        functools.partial(_in_proj_kernel, n_alias=n_alias, bb=bb, tt=tt, pos0=pos0,
                          transposed=transposed),
        grid=(nb, nt),
        in_specs=in_specs,
        out_specs=out_specs,
        out_shape=out_shape,
        scratch_shapes=[pltpu.VMEM((bb, tt + HIST_ROWS, POOL_WIDTH), F32)],
        input_output_aliases={i: i for i in range(n_alias)},
        compiler_params=pltpu.CompilerParams(
            dimension_semantics=("arbitrary", "arbitrary"), vmem_limit_bytes=VMEM_LIMIT),
        name="in_proj_prompt" if transposed else "in_proj_sample",
    )(*args)
    return outs[:4], outs[4:]


def _cumsum_kernel(x_ref, o_ref):
    rb, length = x_ref.shape
    c = CUMSUM_CHUNK
    tri = (lax.broadcasted_iota(jnp.int32, (c, c), 0)
           <= lax.broadcasted_iota(jnp.int32, (c, c), 1)).astype(F32)
    total = jnp.zeros((rb, 1), F32)
    for i in range(length // c):
        y = total + jnp.dot(x_ref[:, i * c:(i + 1) * c], tri, precision=lax.Precision.HIGHEST,
                            preferred_element_type=F32)
        o_ref[:, i * c:(i + 1) * c] = y
        total = y[:, c - 1:c]


def _cumsum_time(x, rows_per_block):
    length = x.shape[-1]
    r = int(np.prod(x.shape[:-1]))
    rb = rows_per_block
    out = pl.pallas_call(
        _cumsum_kernel,
        grid=(r // rb,),
        in_specs=[pl.BlockSpec((rb, length), lambda i: (i, 0))],
        out_specs=pl.BlockSpec((rb, length), lambda i: (i, 0)),
        out_shape=jax.ShapeDtypeStruct((r, length), F32),
        compiler_params=pltpu.CompilerParams(
            dimension_semantics=("arbitrary",), vmem_limit_bytes=VMEM_LIMIT),
        name="cumsum_time",
    )(x.reshape(r, length))
    return out.reshape(x.shape)


def _fill_key_forget_sums(fcol_ref, fkb_ref, hp):
    chunk = 512
    lane = lax.broadcasted_iota(jnp.int32, (chunk, HEADS_PAD), 1)
    for hh in range(2):
        def fill(c, carry, hh=hh):
            r0 = pl.multiple_of(c * chunk, chunk)
            blk = fcol_ref[pl.ds(r0, chunk), :]
            col = jnp.sum(jnp.where(lane == 2 * hp + hh, blk, 0.0), axis=1, keepdims=True)
            fkb_ref[hh, pl.ds(r0, chunk), :] = jnp.broadcast_to(col * LOG2E, (chunk, LANES))
            return carry
        lax.fori_loop(0, fcol_ref.shape[0] // chunk, fill, 0)


def _pair_query_columns(qt):
    row_head = lax.broadcasted_iota(jnp.int32, (2 * HEAD_DIM, 1), 0) // HEAD_DIM
    zero = jnp.zeros_like(qt)
    return jnp.concatenate(
        [jnp.where(row_head == 0, qt, zero), jnp.where(row_head == 1, qt, zero)], axis=1)


def _attn_prompt_bounded_kernel(kmax_ref, qt_ref, k_ref, vt_ref, ft_ref, fcol_ref, o_ref,
                                fkb_ref, p_ref, *, tq, tk):
    hp = pl.program_id(1)
    qi = pl.program_id(2)
    n_full = (qi * tq) // tk

    @pl.when(qi == 0)
    def _():
        _fill_key_forget_sums(fcol_ref, fkb_ref, hp)

    causal = (n_full * tk + lax.broadcasted_iota(jnp.int32, (tk, tq), 0)
              <= qi * tq + lax.broadcasted_iota(jnp.int32, (tk, tq), 1))
    ones_rows = jnp.ones((HEADS_PAD, tk), BF16)
    qt = qt_ref[...]
    q2 = _pair_query_columns(qt)
    qsq = jnp.square(qt.astype(F32))
    shift = []
    for hh in range(2):
        qnorm = jnp.sqrt(jnp.sum(qsq[hh * HEAD_DIM:(hh + 1) * HEAD_DIM], axis=0, keepdims=True))
        shift.append(ft_ref[pl.ds(2 * hp + hh, 1), :] * LOG2E - qnorm * kmax_ref[0, 0])

    def weights(j, masked):
        k0 = pl.multiple_of(j * tk, tk)
        s2 = _dot(k_ref[pl.ds(k0, tk), :], q2)
        ps = []
        for hh in range(2):
            fk = fkb_ref[hh, pl.ds(k0, tk), :]
            a = (s2[:, hh * tq:(hh + 1) * tq] - jnp.concatenate([fk] * (tq // LANES), axis=1)
                 + shift[hh])
            if masked:
                a = jnp.where(causal, a, NEG)
            ps.append(jnp.exp2(a).astype(BF16))
        return jnp.concatenate(ps, axis=1)

    def accumulate(j, acc, p2):
        k0 = pl.multiple_of(j * tk, tk)
        out = []
        for hh in range(2):
            vb = jnp.concatenate(
                [vt_ref[pl.ds(hh * HEAD_DIM, HEAD_DIM), pl.ds(k0, tk)], ones_rows], axis=0)
            out.append(acc[hh] + _dot(vb, p2[:, hh * tq:(hh + 1) * tq]))
        return tuple(out)

    p_ref[...] = weights(n_full, True)

    def step(j, acc):
        p_next = weights(j, False)
        acc = accumulate(jnp.where(j == 0, n_full, j - 1), acc, p_ref[...])
        p_ref[...] = p_next
        return acc

    acc0 = jnp.zeros((HEAD_DIM + HEADS_PAD, tq), F32)
    acc = lax.fori_loop(0, n_full, step, (acc0, acc0))
    acc = accumulate(jnp.maximum(n_full - 1, 0), acc, p_ref[...])
    outs = [a[0:HEAD_DIM] / a[HEAD_DIM:HEAD_DIM + 1] for a in acc]
    o_ref[...] = jnp.concatenate(outs, axis=0).T.astype(BF16)


def _attn_prompt_kernel(qt_ref, k_ref, vt_ref, ft_ref, fcol_ref, o_ref, fkb_ref, s_ref, p_ref,
                        *, tq, tk):
    hp = pl.program_id(1)
    qi = pl.program_id(2)
    n_full = (qi * tq) // tk

    @pl.when(qi == 0)
    def _():
        _fill_key_forget_sums(fcol_ref, fkb_ref, hp)

    causal = (n_full * tk + lax.broadcasted_iota(jnp.int32, (tk, tq), 0)
              <= qi * tq + lax.broadcasted_iota(jnp.int32, (tk, tq), 1))
    ones_rows = jnp.ones((HEADS_PAD, tk), BF16)
    q2 = _pair_query_columns(qt_ref[...])
    fq = [ft_ref[pl.ds(2 * hp + hh, 1), :] * LOG2E for hh in range(2)]

    def biased_scores(j):
        k0 = pl.multiple_of(j * tk, tk)
        s2 = _dot(k_ref[pl.ds(k0, tk), :], q2)
        a, cmax = [], []
        for hh in range(2):
            fk = fkb_ref[hh, pl.ds(k0, tk), :]
            ah = s2[:, hh * tq:(hh + 1) * tq] - jnp.concatenate([fk] * (tq // LANES), axis=1)
            a.append(ah)
            cmax.append(jnp.max(ah, axis=0, keepdims=True))
        return jnp.concatenate(a, axis=1), cmax

    def weights(a2, cmax, m_run):
        ps, alphas, ms = [], [], []
        for hh in range(2):
            m_new = jnp.maximum(m_run[hh], cmax[hh] + fq[hh])
            alphas.append(jnp.exp2(m_run[hh] - m_new))
            ps.append(jnp.exp2(a2[:, hh * tq:(hh + 1) * tq] + (fq[hh] - m_new)).astype(BF16))
            ms.append(m_new)
        return ps, alphas, ms

    def accumulate(j, acc, p, alpha):
        k0 = pl.multiple_of(j * tk, tk)
        out = []
        for hh in range(2):
            vb = jnp.concatenate(
                [vt_ref[pl.ds(hh * HEAD_DIM, HEAD_DIM), pl.ds(k0, tk)], ones_rows], axis=0)
            out.append(alpha[hh] * acc[hh] + _dot(vb, p[hh]))
        return out

    a0, cmax0 = biased_scores(0)
    s_ref[...] = a0
    p_ref[...] = jnp.zeros(p_ref.shape, BF16)

    def step(j, carry):
        acc, m_run, alpha, cmax = carry
        a_cur = s_ref[...]
        p_prev = p_ref[...]
        a_next, cmax_next = biased_scores(j + 1)
        p, alpha_new, m_new = weights(a_cur, cmax, m_run)
        acc = accumulate(jnp.maximum(j - 1, 0), acc, [p_prev[:, :tq], p_prev[:, tq:]], alpha)
        s_ref[...] = a_next
        p_ref[...] = jnp.concatenate(p, axis=1)
        return tuple(acc), tuple(m_new), tuple(alpha_new), tuple(cmax_next)

    acc0 = jnp.zeros((HEAD_DIM + HEADS_PAD, tq), F32)
    m0 = jnp.full((1, tq), NEG, F32)
    one = jnp.ones((1, tq), F32)
    acc, m_run, alpha, _ = lax.fori_loop(
        0, n_full, step, ((acc0, acc0), (m0, m0), (one, one), tuple(cmax0)))
    p_prev = p_ref[...]
    a_diag = s_ref[...]
    a_diag = jnp.concatenate(
        [jnp.where(causal, a_diag[:, hh * tq:(hh + 1) * tq], NEG) for hh in range(2)], axis=1)
    cmax = [jnp.max(a_diag[:, hh * tq:(hh + 1) * tq], axis=0, keepdims=True) for hh in range(2)]
    p, alpha_new, _ = weights(a_diag, cmax, m_run)
    acc = accumulate(jnp.maximum(n_full - 1, 0), acc, [p_prev[:, :tq], p_prev[:, tq:]], alpha)
    acc = accumulate(n_full, acc, p, alpha_new)
    outs = [a[0:HEAD_DIM] / a[HEAD_DIM:HEAD_DIM + 1] for a in acc]
    o_ref[...] = jnp.concatenate(outs, axis=0).T.astype(BF16)


def _attn_prompt(qt, kbf, vt, ft, fcol, q_gain, k_gain, *, tq, tk):
    b, t, _ = kbf.shape
    pairs = N_HEADS // 2
    in_specs = [
        pl.BlockSpec((None, 2 * HEAD_DIM, tq), lambda i, p, j: (i, p, j)),
        pl.BlockSpec((None, t, 2 * HEAD_DIM), lambda i, p, j: (i, 0, p)),
        pl.BlockSpec((None, 2 * HEAD_DIM, t), lambda i, p, j: (i, p, 0)),
        pl.BlockSpec((None, HEADS_PAD, tq), lambda i, p, j: (i, 0, j)),
        pl.BlockSpec((None, t, HEADS_PAD), lambda i, p, j: (i, 0, 0)),
    ]
    common = dict(
        grid=(b, pairs, t // tq),
        out_specs=pl.BlockSpec((None, tq, 2 * HEAD_DIM), lambda i, p, j: (i, j, p)),
        out_shape=jax.ShapeDtypeStruct((b, t, ATT_WIDTH), BF16),
        compiler_params=pltpu.CompilerParams(
            dimension_semantics=("arbitrary", "arbitrary", "arbitrary"),
            vmem_limit_bytes=VMEM_LIMIT),
    )
    key_sums = pltpu.VMEM((2, t, LANES), F32)
    weights_tile = pltpu.VMEM((tk, 2 * tq), BF16)

    def running_max(_, *args):
        return pl.pallas_call(
            functools.partial(_attn_prompt_kernel, tq=tq, tk=tk),
            in_specs=in_specs,
            scratch_shapes=[key_sums, pltpu.VMEM((tk, 2 * tq), F32), weights_tile],
            name="attn_prompt", **common)(*args)

    def bounded(kmax, *args):
        return pl.pallas_call(
            functools.partial(_attn_prompt_bounded_kernel, tq=tq, tk=tk),
            in_specs=[pl.BlockSpec(memory_space=pltpu.SMEM)] + in_specs,
            scratch_shapes=[key_sums, weights_tile],
            name="attn_prompt_bounded", **common)(kmax, *args)

    qmax = 1.01 * LOG2E * jnp.max(jnp.abs(q_gain))
    kmax = 1.01 * HEAD_DIM ** 0.5 * jnp.max(jnp.abs(k_gain))
    return lax.cond(2.0 * qmax * kmax <= BOUND_SLACK_OCTAVES, bounded, running_max,
                    kmax.reshape(1, 1).astype(F32), qt, kbf, vt, ft, fcol)


def _attn_sample_kernel(q_ref, ck_ref, cv_ref, kn_ref, vn_ref, f_ref, lrow_ref, lcol_ref, o_ref,
                        qbd_ref, m_ref, l_ref, acc_ref, fq_ref, fnew_ref, *, tk, past, tt):
    i = pl.program_id(0)
    j = pl.program_id(1)
    rows = N_HEADS * tt
    own = ((lax.broadcasted_iota(jnp.int32, (rows, ATT_WIDTH), 0) // tt)
           == (lax.broadcasted_iota(jnp.int32, (rows, ATT_WIDTH), 1) // HEAD_DIM))

    def cached_forget_sums(start, size):
        blk = f_ref[:, :, pl.ds(start, size)]
        mine = lax.broadcasted_iota(jnp.int32, (1, SUBLANES, 1), 1) == i % SUBLANES
        return jnp.sum(jnp.where(mine, blk, 0.0), axis=1)

    @pl.when(j == 0)
    def _():
        q = q_ref[...]
        qbd_ref[...] = jnp.where(own, jnp.concatenate([q] * N_HEADS, axis=0), jnp.zeros((), BF16))
        m_ref[...] = jnp.full(m_ref.shape, NEG, F32)
        l_ref[...] = jnp.zeros(l_ref.shape, F32)
        acc_ref[...] = jnp.zeros(acc_ref.shape, F32)
        f_last = cached_forget_sums(past - 1, 1)
        upper = (lax.broadcasted_iota(jnp.int32, (tt, tt), 0)
                 <= lax.broadcasted_iota(jnp.int32, (tt, tt), 1)).astype(F32)
        run = jnp.dot(lrow_ref[...], upper, precision=lax.Precision.HIGHEST,
                      preferred_element_type=F32)
        fnew_ref[...] = f_last + run[:N_HEADS]
        r_i = lax.broadcasted_iota(jnp.int32, (rows, rows), 0)
        c_i = lax.broadcasted_iota(jnp.int32, (rows, rows), 1)
        lower = ((r_i // tt == c_i // tt) & (c_i <= r_i)).astype(F32)
        run_col = jnp.dot(lower, jnp.broadcast_to(lcol_ref[...], (rows, LANES)),
                          precision=lax.Precision.HIGHEST, preferred_element_type=F32)
        f_last_rows = jnp.broadcast_to(f_last[:, None, :], (N_HEADS, tt, 1)).reshape(rows, 1)
        fq_ref[...] = f_last_rows + run_col[:, 0:1]

    fq3 = fq_ref[...].reshape(N_HEADS, tt, 1)

    def update(kb, vb, fk, mask, time_on_lanes):
        if time_on_lanes:
            n = kb.shape[1]
            s = _dot(qbd_ref[...], kb.astype(BF16))
        else:
            n = kb.shape[0]
            s = _dot_nt(qbd_ref[...], kb.astype(BF16))
        s = s.reshape(N_HEADS, tt, n) + fq3 - fk[:, None, :]
        if mask is not None:
            s = jnp.where(mask, s, NEG)
        m_old = m_ref[...].reshape(N_HEADS, tt, 1)
        m_new = jnp.maximum(m_old, jnp.max(s, axis=-1, keepdims=True))
        alpha = jnp.exp(m_old - m_new)
        p = jnp.exp(s - m_new)
        l_ref[...] = (alpha * l_ref[...].reshape(N_HEADS, tt, 1)
                      + jnp.sum(p, axis=-1, keepdims=True)).reshape(rows, 1)
        m_ref[...] = m_new.reshape(rows, 1)
        pb = p.reshape(rows, n).astype(BF16)
        pv = _dot_nt(pb, vb.astype(BF16)) if time_on_lanes else _dot(pb, vb.astype(BF16))
        acc_ref[...] = alpha.reshape(rows, 1) * acc_ref[...] + pv

    k0 = pl.multiple_of(j * tk, tk)
    update(ck_ref[...], cv_ref[...], cached_forget_sums(k0, tk), None, True)

    @pl.when(j == pl.num_programs(1) - 1)
    def _():
        mask = (lax.broadcasted_iota(jnp.int32, (1, tt, tt), 2)
                <= lax.broadcasted_iota(jnp.int32, (1, tt, tt), 1))
        update(kn_ref[...], vn_ref[...], fnew_ref[...], mask, False)
        o = jnp.where(own, acc_ref[...] / l_ref[...], 0.0).reshape(N_HEADS, tt, ATT_WIDTH)
        o_ref[...] = jnp.sum(o, axis=0).astype(BF16)


def _attn_sample(q, cache_k, cache_v, k_new, v_new, f_past, logf_row, logf_col, layer, *, tk):
    nb, tt, _ = q.shape
    past = cache_k.shape[3]
    rows = N_HEADS * tt
    return pl.pallas_call(
        functools.partial(_attn_sample_kernel, tk=tk, past=past, tt=tt),
        grid=(nb, past // tk),
        in_specs=[
            pl.BlockSpec((None, tt, ATT_WIDTH), lambda i, j: (i, 0, 0)),
            pl.BlockSpec((None, None, ATT_WIDTH, tk), lambda i, j: (layer, i, 0, j)),
            pl.BlockSpec((None, None, ATT_WIDTH, tk), lambda i, j: (layer, i, 0, j)),
            pl.BlockSpec((None, None, tt, ATT_WIDTH), lambda i, j: (layer, i, 0, 0)),
            pl.BlockSpec((None, None, tt, ATT_WIDTH), lambda i, j: (layer, i, 0, 0)),
            pl.BlockSpec((None, N_HEADS, SUBLANES, past), lambda i, j: (layer, 0, i // SUBLANES, 0)),
            pl.BlockSpec((None, HEADS_PAD, tt), lambda i, j: (i, 0, 0)),
            pl.BlockSpec((None, rows, 1), lambda i, j: (i, 0, 0)),
        ],
        out_specs=pl.BlockSpec((None, tt, ATT_WIDTH), lambda i, j: (i, 0, 0)),
        out_shape=jax.ShapeDtypeStruct((nb, tt, ATT_WIDTH), BF16),
        scratch_shapes=[
            pltpu.VMEM((rows, ATT_WIDTH), BF16),
            pltpu.VMEM((rows, 1), F32),
            pltpu.VMEM((rows, 1), F32),
            pltpu.VMEM((rows, ATT_WIDTH), F32),
            pltpu.VMEM((rows, 1), F32),
            pltpu.VMEM((N_HEADS, tt), F32),
        ],
        compiler_params=pltpu.CompilerParams(
            dimension_semantics=("arbitrary", "arbitrary"), vmem_limit_bytes=VMEM_LIMIT),
        name="attn_sample",
    )(q, cache_k, cache_v, k_new, v_new, f_past, logf_row, logf_col)


def _out_mlp_kernel(x_ref, py_ref, att_ref, g1_ref, sh_ref, sc_ref, g2_ref,
                    wo_ref, wup_ref, wdn_ref, y_ref, *, bb, tt, ff_chunk):
    m = bb * tt
    mix = (_dot(py_ref[...].reshape(m, POOL_WIDTH), wo_ref[0:POOL_WIDTH, :])
           + _dot(att_ref[...].reshape(m, ATT_WIDTH), wo_ref[POOL_WIDTH:, :]))
    x1 = x_ref[...] + g1_ref[...] * mix.reshape(bb, tt, D_MODEL)
    ms = jnp.mean(x1 * x1, axis=-1, keepdims=True)
    h = (x1 * lax.rsqrt(ms + EPS)) * (1.0 + sc_ref[...]) + sh_ref[...]
    hb = h.reshape(m, D_MODEL).astype(BF16)
    ff = jnp.zeros((m, D_MODEL), F32)
    for c in range(D_FF // ff_chunk):
        up = _dot(hb, wup_ref[:, c * ff_chunk:(c + 1) * ff_chunk])
        act = jnp.square(jnp.maximum(up, 0.0)).astype(BF16)
        ff = ff + _dot(act, wdn_ref[c * ff_chunk:(c + 1) * ff_chunk, :])
    y_ref[...] = x1 + g2_ref[...] * ff.reshape(bb, tt, D_MODEL)


def _out_mlp(x, py, att, g1, sh, sc, g2, w, layer, *, bb, tt, name):
    b, t, _ = x.shape

    def tile(width):
        return pl.BlockSpec((bb, tt, width), lambda i, j: (i, j, 0))

    def per_b():
        return pl.BlockSpec((bb, 1, D_MODEL), lambda i, j: (i, 0, 0))

    return pl.pallas_call(
        functools.partial(_out_mlp_kernel, bb=bb, tt=tt, ff_chunk=1024),
        grid=(b // bb, t // tt),
        in_specs=[
            tile(D_MODEL), tile(POOL_WIDTH), tile(ATT_WIDTH),
            per_b(), per_b(), per_b(), per_b(),
            _resident_layer((POOL_WIDTH + ATT_WIDTH, D_MODEL), layer),
            _resident_layer((D_MODEL, D_FF), layer), _resident_layer((D_FF, D_MODEL), layer),
        ],
        out_specs=tile(D_MODEL),
        out_shape=jax.ShapeDtypeStruct((b, t, D_MODEL), F32),
        compiler_params=pltpu.CompilerParams(
            dimension_semantics=("arbitrary", "arbitrary"), vmem_limit_bytes=VMEM_LIMIT),
        name=name,
    )(x, py, att, g1, sh, sc, g2, w["w_out"], w["w_up"], w["w_down"])


def _layer_weights(depth, layer, stacked, w_in, b_f, q_gain, k_gain, w_pool, pool_scale):
    w_f = w_in[layer, :, POOL_WIDTH + 3 * ATT_WIDTH:]
    head_of_col = np.arange(ATT_WIDTH) // HEAD_DIM
    ind = (head_of_col[:, None] == np.arange(LANES)[None, :]).astype(np.float32)
    w_pool_bd = jnp.zeros((POOL_WIDTH, POOL_WIDTH), F32)
    for g in range(len(POOL_WINDOWS)):
        sl = slice(g * POOL_GROUP_DIM, (g + 1) * POOL_GROUP_DIM)
        w_pool_bd = w_pool_bd.at[sl, sl].set(w_pool[layer, g])
    return {
        "depth": depth,
        **stacked,
        "w_f": jnp.pad(w_f, ((0, 0), (0, LANES - N_HEADS))).astype(BF16),
        "w_ft": jnp.pad(w_f.T, ((0, HEADS_PAD - N_HEADS), (0, 0))).astype(BF16),
        "bf_row": jnp.pad(b_f[layer], (0, LANES - N_HEADS)).reshape(1, LANES),
        "bf_col": jnp.pad(b_f[layer], (0, HEADS_PAD - N_HEADS)).reshape(HEADS_PAD, 1),
        "qg": jnp.tile(q_gain[layer], N_HEADS).reshape(1, ATT_WIDTH),
        "kg": jnp.tile(k_gain[layer], N_HEADS).reshape(1, ATT_WIDTH),
        "ind": jnp.asarray(ind, BF16),
        "ind_t": jnp.asarray(ind.T, BF16),
        "w_pool": w_pool_bd.astype(BF16),
        "pool_scale": pool_scale[layer].reshape(1, POOL_WIDTH),
    }


def _mod_parts(mod_rows):
    b = mod_rows.shape[0]
    parts = mod_rows.reshape(b, 6, 1, D_MODEL)
    return [parts[:, i] for i in range(6)]


def kernel(x_prompt, x_sample, c_prompt, c_sample, cache_k, cache_v, cache_logf, cache_pool,
           w_mod, b_mod, w_in, b_f, q_gain, k_gain, w_pool, pool_scale, w_out, w_up, w_down):
    depth = w_mod.shape[0]
    bp, seq, _ = x_prompt.shape
    bs, dec_seq, _ = x_sample.shape
    past = cache_k.shape[2]
    tq = 512
    tk_p = 1024
    tm_in = 512
    tm_out = 512
    tk_cache = 2048

    n_c = bp + bs
    c_rows = -(-n_c // 8) * 8
    c_all = jnp.pad(jnp.concatenate([c_prompt, c_sample], axis=0), ((0, c_rows - n_c), (0, 0)))
    mod = _modulation(c_all, w_mod, b_mod)

    ck = jnp.transpose(cache_k, (0, 1, 3, 4, 2)).reshape(depth, bs, ATT_WIDTH, past)
    cv = jnp.transpose(cache_v, (0, 1, 3, 4, 2)).reshape(depth, bs, ATT_WIDTH, past)
    f_past = _cumsum_time(jnp.transpose(cache_logf, (0, 3, 1, 2)), rows_per_block=8 * bs)
    stacked = {"w_in": jnp.swapaxes(w_in, 1, 2).astype(BF16), "w_out": w_out.astype(BF16),
               "w_up": w_up.astype(BF16), "w_down": w_down.astype(BF16)}
    hist_p = jnp.zeros((bp, HIST_ROWS, POOL_WIDTH), F32)

    y_p, y_s = x_prompt, x_sample
    slabs_p = slabs_s = None
    for l in range(depth):
        w = _layer_weights(depth, l, stacked, w_in, b_f, q_gain, k_gain, w_pool, pool_scale)
        sh1, sc1, g1, sh2, sc2, g2 = _mod_parts(mod[l, :bp])
        slabs_p, (kbf, qt, vt, logft, py) = _in_proj(
            y_p, sh1, sc1, hist_p, w, slabs_p, l, bb=1, tt=tm_in, pos0=0, transposed=True)
        ft = _cumsum_time(logft, rows_per_block=bp * HEADS_PAD)
        att = _attn_prompt(qt, kbf, vt, ft, jnp.swapaxes(ft, 1, 2), q_gain[l], k_gain[l],
                           tq=tq, tk=tk_p)
        y_p = _out_mlp(y_p, py, att, g1, sh2, sc2, g2, w, l, bb=1, tt=tm_out, name="out_mlp_prompt")

        sh1, sc1, g1, sh2, sc2, g2 = _mod_parts(mod[l, bp:n_c])
        hist_s = jnp.pad(cache_pool[l], ((0, 0), (HIST_ROWS - POOL_HIST, 0), (0, 0)))
        slabs_s, (_, q_s, logft_s, py_s) = _in_proj(
            y_s, sh1, sc1, hist_s, w, slabs_s, l, bb=bs, tt=dec_seq, pos0=past, transposed=False)
        logf_row = jnp.swapaxes(logft_s.reshape(HEADS_PAD, bs, dec_seq), 0, 1)
        logf_col = logf_row[:, :N_HEADS].reshape(bs, N_HEADS * dec_seq, 1)
        att_s = _attn_sample(q_s, ck, cv, slabs_s[0], slabs_s[1], f_past, logf_row, logf_col, l,
                             tk=tk_cache)
        y_s = _out_mlp(y_s, py_s, att_s, g1, sh2, sc2, g2, w, l, bb=bs, tt=dec_seq,
                       name="out_mlp_sample")

    k_p, v_p, f_p, p_p = slabs_p
    k_s, v_s, f_s, p_s = slabs_s
    return (y_p, y_s,
            k_p.reshape(depth, bp, seq, N_HEADS, HEAD_DIM), v_p.reshape(depth, bp, seq, N_HEADS, HEAD_DIM),
            f_p, p_p,
            k_s.reshape(depth, bs, dec_seq, N_HEADS, HEAD_DIM), v_s.reshape(depth, bs, dec_seq, N_HEADS, HEAD_DIM),
            f_s, p_s)
```

```python
import functools

import numpy as np
import jax
import jax.numpy as jnp
from jax import lax
from jax.experimental import pallas as pl
from jax.experimental.pallas import tpu as pltpu

D_MODEL = 1024
POOL_WIDTH = 256
POOL_GROUP_DIM = 64
POOL_WINDOWS = (2, 4, 8, 16)
POOL_HIST = 15
HIST_ROWS = 16
ATT_WIDTH = 768
HEAD_DIM = 64
N_HEADS = 12
HEADS_PAD = 16
D_FF = 4096
EPS = 1e-6
NEG = -1e30
LOG2E = 1.4426950408889634
BOUND_SLACK_OCTAVES = 64.0

LANES = 128
SUBLANES = 8
CUMSUM_CHUNK = 256
VMEM_LIMIT = 56 * 1024 * 1024

F32 = jnp.float32
BF16 = jnp.bfloat16


def _dot(a, b):
    return jnp.dot(a, b, preferred_element_type=F32)


def _dot_nt(a, b):
    return lax.dot_general(a, b, (((1,), (1,)), ((), ())), preferred_element_type=F32)


def _split_dot(a, b):
    hi = a.astype(BF16)
    lo = (a - hi.astype(F32)).astype(BF16)
    return _dot(hi, b) + _dot(lo, b)


def _resident(shape):
    nd = len(shape)
    return pl.BlockSpec(shape, lambda *_: (0,) * nd, pipeline_mode=pl.Buffered(1))


def _resident_layer(shape, layer):
    nd = len(shape)
    return pl.BlockSpec((None,) + tuple(shape), lambda *_: (layer,) + (0,) * nd,
                        pipeline_mode=pl.Buffered(1))


def _mod_kernel(c_ref, w_ref, b_ref, o_ref):
    c = c_ref[...]
    s = c / (1.0 + jnp.exp(-c))
    o_ref[...] = _dot(s.astype(BF16), w_ref[...].astype(BF16)) + b_ref[...]


def _modulation(c_all, w_mod, b_mod):
    depth, _, n = w_mod.shape
    rows = c_all.shape[0]
    tn = 1536
    return pl.pallas_call(
        _mod_kernel,
        grid=(depth, n // tn),
        in_specs=[
            pl.BlockSpec((rows, D_MODEL), lambda l, j: (0, 0)),
            pl.BlockSpec((None, D_MODEL, tn), lambda l, j: (l, 0, j)),
            pl.BlockSpec((None, 1, tn), lambda l, j: (l, 0, j)),
        ],
        out_specs=pl.BlockSpec((None, rows, tn), lambda l, j: (l, 0, j)),
        out_shape=jax.ShapeDtypeStruct((depth, rows, n), F32),
        compiler_params=pltpu.CompilerParams(
            dimension_semantics=("arbitrary", "arbitrary"), vmem_limit_bytes=VMEM_LIMIT),
        name="modulation",
    )(c_all, w_mod, b_mod.reshape(depth, 1, n))


def _head_rms(a, ind, ind_t):
    ssum = _dot((a * a).astype(BF16), ind)
    r = lax.rsqrt(ssum * (1.0 / HEAD_DIM) + EPS)
    return a * _split_dot(r, ind_t)


def _log_sigmoid(x):
    return jnp.minimum(x, 0.0) - jnp.log1p(jnp.exp(-jnp.abs(x)))


def _in_proj_kernel(*refs, n_alias, bb, tt, pos0, transposed):
    refs = refs[n_alias:]
    (x_ref, sh_ref, sc_ref, hist_ref, win_ref, wf_ref, wft_ref,
     bfr_ref, bfc_ref, qg_ref, kg_ref, ind_ref, indt_ref, wp_ref, ps_ref,
     k32_ref, v32_ref, logf_ref, pstate_ref, kbf_ref, q_ref) = refs[:21]
    vt_ref = refs[21] if transposed else None
    o = POOL_WIDTH
    wu_ref = win_ref.at[0:o, :]
    wq_ref = win_ref.at[o:o + ATT_WIDTH, :]
    wk_ref = win_ref.at[o + ATT_WIDTH:o + 2 * ATT_WIDTH, :]
    wv_ref = win_ref.at[o + 2 * ATT_WIDTH:o + 3 * ATT_WIDTH, :]
    logft_ref, py_ref, z_ref = refs[-3:]
    ti = pl.program_id(1)
    m = bb * tt

    x = x_ref[...]
    ms = jnp.mean(x * x, axis=-1, keepdims=True)
    h = (x * lax.rsqrt(ms + EPS)) * (1.0 + sc_ref[...]) + sh_ref[...]
    hb = h.reshape(m, D_MODEL).astype(BF16)

    ind = ind_ref[...]
    ind_t = indt_ref[...]
    q = _head_rms(_dot_nt(hb, wq_ref[...]), ind, ind_t) * qg_ref[...]
    k = _head_rms(_dot_nt(hb, wk_ref[...]), ind, ind_t) * kg_ref[...]
    v = _dot_nt(hb, wv_ref[...])
    kbf_ref[...] = k.reshape(bb, tt, ATT_WIDTH).astype(BF16)
    qs = q * (HEAD_DIM ** -0.5 * (LOG2E if transposed else 1.0))
    if transposed:
        k_t = k.T
        v_t = v.T
        k32_ref[...] = k_t
        v32_ref[...] = v_t
        q_ref[...] = qs.T.astype(BF16)
        vt_ref[...] = v_t.astype(BF16)
    else:
        k32_ref[...] = k.reshape(bb, tt, ATT_WIDTH)
        v32_ref[...] = v.reshape(bb, tt, ATT_WIDTH)
        q_ref[...] = qs.reshape(bb, tt, ATT_WIDTH).astype(BF16)

    fl = _dot(hb, wf_ref[...]) + bfr_ref[...]
    logf_ref[...] = _log_sigmoid(fl)[:, :N_HEADS].reshape(bb, tt, N_HEADS)
    flt = _dot_nt(wft_ref[...], hb) + bfc_ref[...]
    logft_ref[...] = _log_sigmoid(flt)

    u3 = _dot_nt(hb, wu_ref[...]).reshape(bb, tt, POOL_WIDTH)

    @pl.when(ti == 0)
    def _():
        z_ref[:, 0:HIST_ROWS, :] = hist_ref[...]

    z_ref[:, HIST_ROWS:HIST_ROWS + tt, :] = u3
    acc = u3
    sums = {}
    for i in range(1, max(POOL_WINDOWS)):
        acc = acc + z_ref[:, HIST_ROWS - i:HIST_ROWS - i + tt, :]
        if i + 1 in POOL_WINDOWS:
            sums[i + 1] = acc
    group = lax.broadcasted_iota(jnp.int32, (1, 1, POOL_WIDTH), 2) // POOL_GROUP_DIM
    win = sums[POOL_WINDOWS[-1]]
    width = jnp.full((1, 1, POOL_WIDTH), float(POOL_WINDOWS[-1]), F32)
    for g in range(len(POOL_WINDOWS) - 2, -1, -1):
        win = jnp.where(group == g, sums[POOL_WINDOWS[g]], win)
        width = jnp.where(group == g, float(POOL_WINDOWS[g]), width)
    pos = (pos0 + ti * tt + lax.broadcasted_iota(jnp.int32, (1, tt, 1), 1)).astype(F32)
    count = jnp.minimum(pos + 1.0, width)
    mixed = win / count - u3
    y = _dot(mixed.reshape(m, POOL_WIDTH).astype(BF16), wp_ref[...]) * ps_ref[...]
    py_ref[...] = y.reshape(bb, tt, POOL_WIDTH).astype(BF16)

    pstate_ref[...] = z_ref[:, tt + 1:tt + HIST_ROWS, :]
    z_ref[:, 0:HIST_ROWS, :] = z_ref[:, tt:tt + HIST_ROWS, :]


def _in_proj(x, sh, sc, hist, w, slabs, layer, *, bb, tt, pos0, transposed):
    b, t, _ = x.shape
    depth = w["depth"]
    nb, nt = b // bb, t // tt
    m = bb * tt
    n_alias = 0 if slabs is None else 4

    def per_b(shape):
        return pl.BlockSpec(shape, lambda i, j: (i,) + (0,) * (len(shape) - 1))

    in_specs = [pl.BlockSpec(memory_space=pl.ANY)] * n_alias + [
        pl.BlockSpec((bb, tt, D_MODEL), lambda i, j: (i, j, 0)),
        per_b((bb, 1, D_MODEL)), per_b((bb, 1, D_MODEL)), per_b((bb, HIST_ROWS, POOL_WIDTH)),
        _resident_layer((w["w_in"].shape[1], D_MODEL), layer),
        _resident((D_MODEL, LANES)), _resident((HEADS_PAD, D_MODEL)),
        _resident((1, LANES)), _resident((HEADS_PAD, 1)),
        _resident((1, ATT_WIDTH)), _resident((1, ATT_WIDTH)),
        _resident((ATT_WIDTH, LANES)), _resident((LANES, ATT_WIDTH)),
        _resident((POOL_WIDTH, POOL_WIDTH)), _resident((1, POOL_WIDTH)),
    ]
    if transposed:
        assert bb == 1
        kv_shape = jax.ShapeDtypeStruct((depth, b, ATT_WIDTH, t), F32)
        kv_spec = pl.BlockSpec((None, None, ATT_WIDTH, tt), lambda i, j: (layer, i, 0, j))
    else:
        kv_shape = jax.ShapeDtypeStruct((depth, b, t, ATT_WIDTH), F32)
        kv_spec = pl.BlockSpec((None, bb, tt, ATT_WIDTH), lambda i, j: (layer, i, j, 0))
    slab_shapes = [
        kv_shape,
        kv_shape,
        jax.ShapeDtypeStruct((depth, b, t, N_HEADS), F32),
        jax.ShapeDtypeStruct((depth, b, POOL_HIST, POOL_WIDTH), F32),
    ]
    slab_specs = [
        kv_spec,
        kv_spec,
        pl.BlockSpec((None, bb, tt, N_HEADS), lambda i, j: (layer, i, j, 0)),
        pl.BlockSpec((None, bb, POOL_HIST, POOL_WIDTH), lambda i, j: (layer, i, 0, 0)),
    ]
    if transposed:
        q_shape = jax.ShapeDtypeStruct((b, ATT_WIDTH, t), BF16)
        q_spec = pl.BlockSpec((None, ATT_WIDTH, tt), lambda i, j: (i, 0, j))
    else:
        q_shape = jax.ShapeDtypeStruct((b, t, ATT_WIDTH), BF16)
        q_spec = pl.BlockSpec((bb, tt, ATT_WIDTH), lambda i, j: (i, j, 0))
    out_shape = slab_shapes + [
        jax.ShapeDtypeStruct((b, t, ATT_WIDTH), BF16),
        q_shape,
    ] + ([q_shape] if transposed else []) + [
        jax.ShapeDtypeStruct((nb, HEADS_PAD, nt * m), F32),
        jax.ShapeDtypeStruct((b, t, POOL_WIDTH), BF16),
    ]
    out_specs = slab_specs + [
        pl.BlockSpec((bb, tt, ATT_WIDTH), lambda i, j: (i, j, 0)),
        q_spec,
    ] + ([q_spec] if transposed else []) + [
        pl.BlockSpec((None, HEADS_PAD, m), lambda i, j: (i, 0, j)),
        pl.BlockSpec((bb, tt, POOL_WIDTH), lambda i, j: (i, j, 0)),
    ]
    args = ([] if slabs is None else list(slabs)) + [
        x, sh, sc, hist, w["w_in"], w["w_f"], w["w_ft"],
        w["bf_row"], w["bf_col"], w["qg"], w["kg"], w["ind"], w["ind_t"], w["w_pool"],
        w["pool_scale"],
    ]
    outs = pl.pallas_call(
        functools.partial(_in_proj_kernel, n_alias=n_alias, bb=bb, tt=tt, pos0=pos0,
                          transposed=transposed),
        grid=(nb, nt),
        in_specs=in_specs,
        out_specs=out_specs,
        out_shape=out_shape,
        scratch_shapes=[pltpu.VMEM((bb, tt + HIST_ROWS, POOL_WIDTH), F32)],
        input_output_aliases={i: i for i in range(n_alias)},
        compiler_params=pltpu.CompilerParams(
            dimension_semantics=("arbitrary", "arbitrary"), vmem_limit_bytes=VMEM_LIMIT),
        name="in_proj_prompt" if transposed else "in_proj_sample",
    )(*args)
    return outs[:4], outs[4:]


def _cumsum_kernel(x_ref, o_ref):
    rb, length = x_ref.shape
    c = CUMSUM_CHUNK
    tri = (lax.broadcasted_iota(jnp.int32, (c, c), 0)
           <= lax.broadcasted_iota(jnp.int32, (c, c), 1)).astype(F32)
    total = jnp.zeros((rb, 1), F32)
    for i in range(length // c):
        y = total + jnp.dot(x_ref[:, i * c:(i + 1) * c], tri, precision=lax.Precision.HIGHEST,
                            preferred_element_type=F32)
        o_ref[:, i * c:(i + 1) * c] = y
        total = y[:, c - 1:c]


def _cumsum_time(x, rows_per_block):
    length = x.shape[-1]
    r = int(np.prod(x.shape[:-1]))
    rb = rows_per_block
    out = pl.pallas_call(
        _cumsum_kernel,
        grid=(r // rb,),
        in_specs=[pl.BlockSpec((rb, length), lambda i: (i, 0))],
        out_specs=pl.BlockSpec((rb, length), lambda i: (i, 0)),
        out_shape=jax.ShapeDtypeStruct((r, length), F32),
        compiler_params=pltpu.CompilerParams(
            dimension_semantics=("arbitrary",), vmem_limit_bytes=VMEM_LIMIT),
        name="cumsum_time",
    )(x.reshape(r, length))
    return out.reshape(x.shape)


def _fill_key_forget_sums(fcol_ref, fkb_ref, hp):
    chunk = 512
    lane = lax.broadcasted_iota(jnp.int32, (chunk, HEADS_PAD), 1)
    for hh in range(2):
        def fill(c, carry, hh=hh):
            r0 = pl.multiple_of(c * chunk, chunk)
            blk = fcol_ref[pl.ds(r0, chunk), :]
            col = jnp.sum(jnp.where(lane == 2 * hp + hh, blk, 0.0), axis=1, keepdims=True)
            fkb_ref[hh, pl.ds(r0, chunk), :] = jnp.broadcast_to(col * LOG2E, (chunk, LANES))
            return carry
        lax.fori_loop(0, fcol_ref.shape[0] // chunk, fill, 0)


def _pair_query_columns(qt):
    row_head = lax.broadcasted_iota(jnp.int32, (2 * HEAD_DIM, 1), 0) // HEAD_DIM
    zero = jnp.zeros_like(qt)
    return jnp.concatenate(
        [jnp.where(row_head == 0, qt, zero), jnp.where(row_head == 1, qt, zero)], axis=1)


def _attn_prompt_bounded_kernel(kmax_ref, qt_ref, k_ref, vt_ref, ft_ref, fcol_ref, o_ref,
                                fkb_ref, p_ref, *, tq, tk):
    hp = pl.program_id(1)
    qi = pl.program_id(2)
    n_full = (qi * tq) // tk

    @pl.when(qi == 0)
    def _():
        _fill_key_forget_sums(fcol_ref, fkb_ref, hp)

    causal = (n_full * tk + lax.broadcasted_iota(jnp.int32, (tk, tq), 0)
              <= qi * tq + lax.broadcasted_iota(jnp.int32, (tk, tq), 1))
    ones_rows = jnp.ones((HEADS_PAD, tk), BF16)
    qt = qt_ref[...]
    q2 = _pair_query_columns(qt)
    qsq = jnp.square(qt.astype(F32))
    shift = []
    for hh in range(2):
        qnorm = jnp.sqrt(jnp.sum(qsq[hh * HEAD_DIM:(hh + 1) * HEAD_DIM], axis=0, keepdims=True))
        shift.append(ft_ref[pl.ds(2 * hp + hh, 1), :] * LOG2E - qnorm * kmax_ref[0, 0])

    def weights(j, masked):
        k0 = pl.multiple_of(j * tk, tk)
        s2 = _dot(k_ref[pl.ds(k0, tk), :], q2)
        ps = []
        for hh in range(2):
            fk = fkb_ref[hh, pl.ds(k0, tk), :]
            a = (s2[:, hh * tq:(hh + 1) * tq] - jnp.concatenate([fk] * (tq // LANES), axis=1)
                 + shift[hh])
            if masked:
                a = jnp.where(causal, a, NEG)
            ps.append(jnp.exp2(a).astype(BF16))
        return jnp.concatenate(ps, axis=1)

    def accumulate(j, acc, p2):
        k0 = pl.multiple_of(j * tk, tk)
        out = []
        for hh in range(2):
            vb = jnp.concatenate(
                [vt_ref[pl.ds(hh * HEAD_DIM, HEAD_DIM), pl.ds(k0, tk)], ones_rows], axis=0)
            out.append(acc[hh] + _dot(vb, p2[:, hh * tq:(hh + 1) * tq]))
        return tuple(out)

    p_ref[...] = weights(n_full, True)

    def step(j, acc):
        p_next = weights(j, False)
        acc = accumulate(jnp.where(j == 0, n_full, j - 1), acc, p_ref[...])
        p_ref[...] = p_next
        return acc

    acc0 = jnp.zeros((HEAD_DIM + HEADS_PAD, tq), F32)
    acc = lax.fori_loop(0, n_full, step, (acc0, acc0))
    acc = accumulate(jnp.maximum(n_full - 1, 0), acc, p_ref[...])
    outs = [a[0:HEAD_DIM] / a[HEAD_DIM:HEAD_DIM + 1] for a in acc]
    o_ref[...] = jnp.concatenate(outs, axis=0).T.astype(BF16)


def _attn_prompt_kernel(qt_ref, k_ref, vt_ref, ft_ref, fcol_ref, o_ref, fkb_ref, s_ref, p_ref,
                        *, tq, tk):
    hp = pl.program_id(1)
    qi = pl.program_id(2)
    n_full = (qi * tq) // tk

    @pl.when(qi == 0)
    def _():
        _fill_key_forget_sums(fcol_ref, fkb_ref, hp)

    causal = (n_full * tk + lax.broadcasted_iota(jnp.int32, (tk, tq), 0)
              <= qi * tq + lax.broadcasted_iota(jnp.int32, (tk, tq), 1))
    ones_rows = jnp.ones((HEADS_PAD, tk), BF16)
    q2 = _pair_query_columns(qt_ref[...])
    fq = [ft_ref[pl.ds(2 * hp + hh, 1), :] * LOG2E for hh in range(2)]

    def biased_scores(j):
        k0 = pl.multiple_of(j * tk, tk)
        s2 = _dot(k_ref[pl.ds(k0, tk), :], q2)
        a, cmax = [], []
        for hh in range(2):
            fk = fkb_ref[hh, pl.ds(k0, tk), :]
            ah = s2[:, hh * tq:(hh + 1) * tq] - jnp.concatenate([fk] * (tq // LANES), axis=1)
            a.append(ah)
            cmax.append(jnp.max(ah, axis=0, keepdims=True))
        return jnp.concatenate(a, axis=1), cmax

    def weights(a2, cmax, m_run):
        ps, alphas, ms = [], [], []
        for hh in range(2):
            m_new = jnp.maximum(m_run[hh], cmax[hh] + fq[hh])
            alphas.append(jnp.exp2(m_run[hh] - m_new))
            ps.append(jnp.exp2(a2[:, hh * tq:(hh + 1) * tq] + (fq[hh] - m_new)).astype(BF16))
            ms.append(m_new)
        return ps, alphas, ms

    def accumulate(j, acc, p, alpha):
        k0 = pl.multiple_of(j * tk, tk)
        out = []
        for hh in range(2):
            vb = jnp.concatenate(
                [vt_ref[pl.ds(hh * HEAD_DIM, HEAD_DIM), pl.ds(k0, tk)], ones_rows], axis=0)
            out.append(alpha[hh] * acc[hh] + _dot(vb, p[hh]))
        return out

    a0, cmax0 = biased_scores(0)
    s_ref[...] = a0
    p_ref[...] = jnp.zeros(p_ref.shape, BF16)

    def step(j, carry):
        acc, m_run, alpha, cmax = carry
        a_cur = s_ref[...]
        p_prev = p_ref[...]
        a_next, cmax_next = biased_scores(j + 1)
        p, alpha_new, m_new = weights(a_cur, cmax, m_run)
        acc = accumulate(jnp.maximum(j - 1, 0), acc, [p_prev[:, :tq], p_prev[:, tq:]], alpha)
        s_ref[...] = a_next
        p_ref[...] = jnp.concatenate(p, axis=1)
        return tuple(acc), tuple(m_new), tuple(alpha_new), tuple(cmax_next)

    acc0 = jnp.zeros((HEAD_DIM + HEADS_PAD, tq), F32)
    m0 = jnp.full((1, tq), NEG, F32)
    one = jnp.ones((1, tq), F32)
    acc, m_run, alpha, _ = lax.fori_loop(
        0, n_full, step, ((acc0, acc0), (m0, m0), (one, one), tuple(cmax0)))
    p_prev = p_ref[...]
    a_diag = s_ref[...]
    a_diag = jnp.concatenate(
        [jnp.where(causal, a_diag[:, hh * tq:(hh + 1) * tq], NEG) for hh in range(2)], axis=1)
    cmax = [jnp.max(a_diag[:, hh * tq:(hh + 1) * tq], axis=0, keepdims=True) for hh in range(2)]
    p, alpha_new, _ = weights(a_diag, cmax, m_run)
    acc = accumulate(jnp.maximum(n_full - 1, 0), acc, [p_prev[:, :tq], p_prev[:, tq:]], alpha)
    acc = accumulate(n_full, acc, p, alpha_new)
    outs = [a[0:HEAD_DIM] / a[HEAD_DIM:HEAD_DIM + 1] for a in acc]
    o_ref[...] = jnp.concatenate(outs, axis=0).T.astype(BF16)


def _attn_prompt(qt, kbf, vt, ft, fcol, q_gain, k_gain, *, tq, tk):
    b, t, _ = kbf.shape
    pairs = N_HEADS // 2
    in_specs = [
        pl.BlockSpec((None, 2 * HEAD_DIM, tq), lambda i, p, j: (i, p, j)),
        pl.BlockSpec((None, t, 2 * HEAD_DIM), lambda i, p, j: (i, 0, p)),
        pl.BlockSpec((None, 2 * HEAD_DIM, t), lambda i, p, j: (i, p, 0)),
        pl.BlockSpec((None, HEADS_PAD, tq), lambda i, p, j: (i, 0, j)),
        pl.BlockSpec((None, t, HEADS_PAD), lambda i, p, j: (i, 0, 0)),
    ]
    common = dict(
        grid=(b, pairs, t // tq),
        out_specs=pl.BlockSpec((None, tq, 2 * HEAD_DIM), lambda i, p, j: (i, j, p)),
        out_shape=jax.ShapeDtypeStruct((b, t, ATT_WIDTH), BF16),
        compiler_params=pltpu.CompilerParams(
            dimension_semantics=("arbitrary", "arbitrary", "arbitrary"),
            vmem_limit_bytes=VMEM_LIMIT),
    )
    key_sums = pltpu.VMEM((2, t, LANES), F32)
    weights_tile = pltpu.VMEM((tk, 2 * tq), BF16)

    def running_max(_, *args):
        return pl.pallas_call(
            functools.partial(_attn_prompt_kernel, tq=tq, tk=tk),
            in_specs=in_specs,
            scratch_shapes=[key_sums, pltpu.VMEM((tk, 2 * tq), F32), weights_tile],
            name="attn_prompt", **common)(*args)

    def bounded(kmax, *args):
        return pl.pallas_call(
            functools.partial(_attn_prompt_bounded_kernel, tq=tq, tk=tk),
            in_specs=[pl.BlockSpec(memory_space=pltpu.SMEM)] + in_specs,
            scratch_shapes=[key_sums, weights_tile],
            name="attn_prompt_bounded", **common)(kmax, *args)

    qmax = 1.01 * LOG2E * jnp.max(jnp.abs(q_gain))
    kmax = 1.01 * HEAD_DIM ** 0.5 * jnp.max(jnp.abs(k_gain))
    return lax.cond(2.0 * qmax * kmax <= BOUND_SLACK_OCTAVES, bounded, running_max,
                    kmax.reshape(1, 1).astype(F32), qt, kbf, vt, ft, fcol)


def _attn_sample_kernel(q_ref, ck_ref, cv_ref, kn_ref, vn_ref, f_ref, lrow_ref, lcol_ref, o_ref,
                        qbd_ref, m_ref, l_ref, acc_ref, fq_ref, fnew_ref, *, tk, past, tt):
    i = pl.program_id(0)
    j = pl.program_id(1)
    rows = N_HEADS * tt
    own = ((lax.broadcasted_iota(jnp.int32, (rows, ATT_WIDTH), 0) // tt)
           == (lax.broadcasted_iota(jnp.int32, (rows, ATT_WIDTH), 1) // HEAD_DIM))

    def cached_forget_sums(start, size):
        blk = f_ref[:, :, pl.ds(start, size)]
        mine = lax.broadcasted_iota(jnp.int32, (1, SUBLANES, 1), 1) == i % SUBLANES
        return jnp.sum(jnp.where(mine, blk, 0.0), axis=1)

    @pl.when(j == 0)
    def _():
        q = q_ref[...]
        qbd_ref[...] = jnp.where(own, jnp.concatenate([q] * N_HEADS, axis=0), jnp.zeros((), BF16))
        m_ref[...] = jnp.full(m_ref.shape, NEG, F32)
        l_ref[...] = jnp.zeros(l_ref.shape, F32)
        acc_ref[...] = jnp.zeros(acc_ref.shape, F32)
        f_last = cached_forget_sums(past - 1, 1)
        upper = (lax.broadcasted_iota(jnp.int32, (tt, tt), 0)
                 <= lax.broadcasted_iota(jnp.int32, (tt, tt), 1)).astype(F32)
        run = jnp.dot(lrow_ref[...], upper, precision=lax.Precision.HIGHEST,
                      preferred_element_type=F32)
        fnew_ref[...] = f_last + run[:N_HEADS]
        r_i = lax.broadcasted_iota(jnp.int32, (rows, rows), 0)
        c_i = lax.broadcasted_iota(jnp.int32, (rows, rows), 1)
        lower = ((r_i // tt == c_i // tt) & (c_i <= r_i)).astype(F32)
        run_col = jnp.dot(lower, jnp.broadcast_to(lcol_ref[...], (rows, LANES)),
                          precision=lax.Precision.HIGHEST, preferred_element_type=F32)
        f_last_rows = jnp.broadcast_to(f_last[:, None, :], (N_HEADS, tt, 1)).reshape(rows, 1)
        fq_ref[...] = f_last_rows + run_col[:, 0:1]

    fq3 = fq_ref[...].reshape(N_HEADS, tt, 1)

    def update(kb, vb, fk, mask, time_on_lanes):
        if time_on_lanes:
            n = kb.shape[1]
            s = _dot(qbd_ref[...], kb.astype(BF16))
        else:
            n = kb.shape[0]
            s = _dot_nt(qbd_ref[...], kb.astype(BF16))
        s = s.reshape(N_HEADS, tt, n) + fq3 - fk[:, None, :]
        if mask is not None:
            s = jnp.where(mask, s, NEG)
        m_old = m_ref[...].reshape(N_HEADS, tt, 1)
        m_new = jnp.maximum(m_old, jnp.max(s, axis=-1, keepdims=True))
        alpha = jnp.exp(m_old - m_new)
        p = jnp.exp(s - m_new)
        l_ref[...] = (alpha * l_ref[...].reshape(N_HEADS, tt, 1)
                      + jnp.sum(p, axis=-1, keepdims=True)).reshape(rows, 1)
        m_ref[...] = m_new.reshape(rows, 1)
        pb = p.reshape(rows, n).astype(BF16)
        pv = _dot_nt(pb, vb.astype(BF16)) if time_on_lanes else _dot(pb, vb.astype(BF16))
        acc_ref[...] = alpha.reshape(rows, 1) * acc_ref[...] + pv

    k0 = pl.multiple_of(j * tk, tk)
    update(ck_ref[...], cv_ref[...], cached_forget_sums(k0, tk), None, True)

    @pl.when(j == pl.num_programs(1) - 1)
    def _():
        mask = (lax.broadcasted_iota(jnp.int32, (1, tt, tt), 2)
                <= lax.broadcasted_iota(jnp.int32, (1, tt, tt), 1))
        update(kn_ref[...], vn_ref[...], fnew_ref[...], mask, False)
        o = jnp.where(own, acc_ref[...] / l_ref[...], 0.0).reshape(N_HEADS, tt, ATT_WIDTH)
        o_ref[...] = jnp.sum(o, axis=0).astype(BF16)


def _attn_sample(q, cache_k, cache_v, k_new, v_new, f_past, logf_row, logf_col, layer, *, tk):
    nb, tt, _ = q.shape
    past = cache_k.shape[3]
    rows = N_HEADS * tt
    return pl.pallas_call(
        functools.partial(_attn_sample_kernel, tk=tk, past=past, tt=tt),
        grid=(nb, past // tk),
        in_specs=[
            pl.BlockSpec((None, tt, ATT_WIDTH), lambda i, j: (i, 0, 0)),
            pl.BlockSpec((None, None, ATT_WIDTH, tk), lambda i, j: (layer, i, 0, j)),
            pl.BlockSpec((None, None, ATT_WIDTH, tk), lambda i, j: (layer, i, 0, j)),
            pl.BlockSpec((None, None, tt, ATT_WIDTH), lambda i, j: (layer, i, 0, 0)),
            pl.BlockSpec((None, None, tt, ATT_WIDTH), lambda i, j: (layer, i, 0, 0)),
            pl.BlockSpec((None, N_HEADS, SUBLANES, past), lambda i, j: (layer, 0, i // SUBLANES, 0)),
            pl.BlockSpec((None, HEADS_PAD, tt), lambda i, j: (i, 0, 0)),
            pl.BlockSpec((None, rows, 1), lambda i, j: (i, 0, 0)),
        ],
        out_specs=pl.BlockSpec((None, tt, ATT_WIDTH), lambda i, j: (i, 0, 0)),
        out_shape=jax.ShapeDtypeStruct((nb, tt, ATT_WIDTH), BF16),
        scratch_shapes=[
            pltpu.VMEM((rows, ATT_WIDTH), BF16),
            pltpu.VMEM((rows, 1), F32),
            pltpu.VMEM((rows, 1), F32),
            pltpu.VMEM((rows, ATT_WIDTH), F32),
            pltpu.VMEM((rows, 1), F32),
            pltpu.VMEM((N_HEADS, tt), F32),
        ],
        compiler_params=pltpu.CompilerParams(
            dimension_semantics=("arbitrary", "arbitrary"), vmem_limit_bytes=VMEM_LIMIT),
        name="attn_sample",
    )(q, cache_k, cache_v, k_new, v_new, f_past, logf_row, logf_col)


def _out_mlp_kernel(x_ref, py_ref, att_ref, g1_ref, sh_ref, sc_ref, g2_ref,
                    wo_ref, wup_ref, wdn_ref, y_ref, *, bb, tt, ff_chunk):
    m = bb * tt
    mix = (_dot(py_ref[...].reshape(m, POOL_WIDTH), wo_ref[0:POOL_WIDTH, :])
           + _dot(att_ref[...].reshape(m, ATT_WIDTH), wo_ref[POOL_WIDTH:, :]))
    x1 = x_ref[...] + g1_ref[...] * mix.reshape(bb, tt, D_MODEL)
    ms = jnp.mean(x1 * x1, axis=-1, keepdims=True)
    h = (x1 * lax.rsqrt(ms + EPS)) * (1.0 + sc_ref[...]) + sh_ref[...]
    hb = h.reshape(m, D_MODEL).astype(BF16)
    ff = jnp.zeros((m, D_MODEL), F32)
    for c in range(D_FF // ff_chunk):
        up = _dot(hb, wup_ref[:, c * ff_chunk:(c + 1) * ff_chunk])
        act = jnp.square(jnp.maximum(up, 0.0)).astype(BF16)
        ff = ff + _dot(act, wdn_ref[c * ff_chunk:(c + 1) * ff_chunk, :])
    y_ref[...] = x1 + g2_ref[...] * ff.reshape(bb, tt, D_MODEL)


def _out_mlp(x, py, att, g1, sh, sc, g2, w, layer, *, bb, tt, name):
    b, t, _ = x.shape

    def tile(width):
        return pl.BlockSpec((bb, tt, width), lambda i, j: (i, j, 0))

    def per_b():
        return pl.BlockSpec((bb, 1, D_MODEL), lambda i, j: (i, 0, 0))

    return pl.pallas_call(
        functools.partial(_out_mlp_kernel, bb=bb, tt=tt, ff_chunk=1024),
        grid=(b // bb, t // tt),
        in_specs=[
            tile(D_MODEL), tile(POOL_WIDTH), tile(ATT_WIDTH),
            per_b(), per_b(), per_b(), per_b(),
            _resident_layer((POOL_WIDTH + ATT_WIDTH, D_MODEL), layer),
            _resident_layer((D_MODEL, D_FF), layer), _resident_layer((D_FF, D_MODEL), layer),
        ],
        out_specs=tile(D_MODEL),
        out_shape=jax.ShapeDtypeStruct((b, t, D_MODEL), F32),
        compiler_params=pltpu.CompilerParams(
            dimension_semantics=("arbitrary", "arbitrary"), vmem_limit_bytes=VMEM_LIMIT),
        name=name,
    )(x, py, att, g1, sh, sc, g2, w["w_out"], w["w_up"], w["w_down"])


def _layer_weights(depth, layer, stacked, w_in, b_f, q_gain, k_gain, w_pool, pool_scale):
    w_f = w_in[layer, :, POOL_WIDTH + 3 * ATT_WIDTH:]
    head_of_col = np.arange(ATT_WIDTH) // HEAD_DIM
    ind = (head_of_col[:, None] == np.arange(LANES)[None, :]).astype(np.float32)
    w_pool_bd = jnp.zeros((POOL_WIDTH, POOL_WIDTH), F32)
    for g in range(len(POOL_WINDOWS)):
        sl = slice(g * POOL_GROUP_DIM, (g + 1) * POOL_GROUP_DIM)
        w_pool_bd = w_pool_bd.at[sl, sl].set(w_pool[layer, g])
    return {
        "depth": depth,
        **stacked,
        "w_f": jnp.pad(w_f, ((0, 0), (0, LANES - N_HEADS))).astype(BF16),
        "w_ft": jnp.pad(w_f.T, ((0, HEADS_PAD - N_HEADS), (0, 0))).astype(BF16),
        "bf_row": jnp.pad(b_f[layer], (0, LANES - N_HEADS)).reshape(1, LANES),
        "bf_col": jnp.pad(b_f[layer], (0, HEADS_PAD - N_HEADS)).reshape(HEADS_PAD, 1),
        "qg": jnp.tile(q_gain[layer], N_HEADS).reshape(1, ATT_WIDTH),
        "kg": jnp.tile(k_gain[layer], N_HEADS).reshape(1, ATT_WIDTH),
        "ind": jnp.asarray(ind, BF16),
        "ind_t": jnp.asarray(ind.T, BF16),
        "w_pool": w_pool_bd.astype(BF16),
        "pool_scale": pool_scale[layer].reshape(1, POOL_WIDTH),
    }


def _mod_parts(mod_rows):
    b = mod_rows.shape[0]
    parts = mod_rows.reshape(b, 6, 1, D_MODEL)
    return [parts[:, i] for i in range(6)]


def _time_major(slab):
    depth, b, _, t = slab.shape
    return jnp.transpose(slab.reshape(depth, b, N_HEADS, HEAD_DIM, t), (0, 1, 4, 2, 3))


def kernel(x_prompt, x_sample, c_prompt, c_sample, cache_k, cache_v, cache_logf, cache_pool,
           w_mod, b_mod, w_in, b_f, q_gain, k_gain, w_pool, pool_scale, w_out, w_up, w_down):
    depth = w_mod.shape[0]
    bp, seq, _ = x_prompt.shape
    bs, dec_seq, _ = x_sample.shape
    past = cache_k.shape[2]
    tq = 512
    tk_p = 1024
    tm_in = 512
    tm_out = 512
    tk_cache = 2048

    n_c = bp + bs
    c_rows = -(-n_c // 8) * 8
    c_all = jnp.pad(jnp.concatenate([c_prompt, c_sample], axis=0), ((0, c_rows - n_c), (0, 0)))
    mod = _modulation(c_all, w_mod, b_mod)

    ck = jnp.transpose(cache_k, (0, 1, 3, 4, 2)).reshape(depth, bs, ATT_WIDTH, past)
    cv = jnp.transpose(cache_v, (0, 1, 3, 4, 2)).reshape(depth, bs, ATT_WIDTH, past)
    f_past = _cumsum_time(jnp.transpose(cache_logf, (0, 3, 1, 2)), rows_per_block=8 * bs)
    stacked = {"w_in": jnp.swapaxes(w_in, 1, 2).astype(BF16), "w_out": w_out.astype(BF16),
               "w_up": w_up.astype(BF16), "w_down": w_down.astype(BF16)}
    hist_p = jnp.zeros((bp, HIST_ROWS, POOL_WIDTH), F32)

    y_p, y_s = x_prompt, x_sample
    slabs_p = slabs_s = None
    for l in range(depth):
        w = _layer_weights(depth, l, stacked, w_in, b_f, q_gain, k_gain, w_pool, pool_scale)
        sh1, sc1, g1, sh2, sc2, g2 = _mod_parts(mod[l, :bp])
        slabs_p, (kbf, qt, vt, logft, py) = _in_proj(
            y_p, sh1, sc1, hist_p, w, slabs_p, l, bb=1, tt=tm_in, pos0=0, transposed=True)
        ft = _cumsum_time(logft, rows_per_block=bp * HEADS_PAD)
        att = _attn_prompt(qt, kbf, vt, ft, jnp.swapaxes(ft, 1, 2), q_gain[l], k_gain[l],
                           tq=tq, tk=tk_p)
        y_p = _out_mlp(y_p, py, att, g1, sh2, sc2, g2, w, l, bb=1, tt=tm_out, name="out_mlp_prompt")

        sh1, sc1, g1, sh2, sc2, g2 = _mod_parts(mod[l, bp:n_c])
        hist_s = jnp.pad(cache_pool[l], ((0, 0), (HIST_ROWS - POOL_HIST, 0), (0, 0)))
        slabs_s, (_, q_s, logft_s, py_s) = _in_proj(
            y_s, sh1, sc1, hist_s, w, slabs_s, l, bb=bs, tt=dec_seq, pos0=past, transposed=False)
        logf_row = jnp.swapaxes(logft_s.reshape(HEADS_PAD, bs, dec_seq), 0, 1)
        logf_col = logf_row[:, :N_HEADS].reshape(bs, N_HEADS * dec_seq, 1)
        att_s = _attn_sample(q_s, ck, cv, slabs_s[0], slabs_s[1], f_past, logf_row, logf_col, l,
                             tk=tk_cache)
        y_s = _out_mlp(y_s, py_s, att_s, g1, sh2, sc2, g2, w, l, bb=bs, tt=dec_seq,
                       name="out_mlp_sample")

    k_p, v_p, f_p, p_p = slabs_p
    k_s, v_s, f_s, p_s = slabs_s
    return (y_p, y_s,
            _time_major(k_p), _time_major(v_p), f_p, p_p,
            k_s.reshape(depth, bs, dec_seq, N_HEADS, HEAD_DIM), v_s.reshape(depth, bs, dec_seq, N_HEADS, HEAD_DIM),
            f_s, p_s)
```

```python
import functools

import numpy as np
import jax
import jax.numpy as jnp
from jax import lax
from jax.experimental import pallas as pl
from jax.experimental.pallas import tpu as pltpu

D_MODEL = 1024
POOL_WIDTH = 256
POOL_GROUP_DIM = 64
POOL_WINDOWS = (2, 4, 8, 16)
POOL_HIST = 15
HIST_ROWS = 16
ATT_WIDTH = 768
HEAD_DIM = 64
N_HEADS = 12
HEADS_PAD = 16
D_FF = 4096
EPS = 1e-6
NEG = -1e30
LOG2E = 1.4426950408889634
BOUND_SLACK_OCTAVES = 64.0

LANES = 128
SUBLANES = 8
CUMSUM_CHUNK = 256
VMEM_LIMIT = 56 * 1024 * 1024

F32 = jnp.float32
BF16 = jnp.bfloat16


def _dot(a, b):
    return jnp.dot(a, b, preferred_element_type=F32)


def _dot_nt(a, b):
    return lax.dot_general(a, b, (((1,), (1,)), ((), ())), preferred_element_type=F32)


def _split_dot(a, b):
    hi = a.astype(BF16)
    lo = (a - hi.astype(F32)).astype(BF16)
    return _dot(hi, b) + _dot(lo, b)


def _resident(shape):
    nd = len(shape)
    return pl.BlockSpec(shape, lambda *_: (0,) * nd, pipeline_mode=pl.Buffered(1))


def _resident_layer(shape, layer):
    nd = len(shape)
    return pl.BlockSpec((None,) + tuple(shape), lambda *_: (layer,) + (0,) * nd,
                        pipeline_mode=pl.Buffered(1))


def _mod_kernel(c_ref, w_ref, b_ref, o_ref):
    c = c_ref[...]
    s = c / (1.0 + jnp.exp(-c))
    o_ref[...] = _dot(s.astype(BF16), w_ref[...].astype(BF16)) + b_ref[...]


def _modulation(c_all, w_mod, b_mod):
    depth, _, n = w_mod.shape
    rows = c_all.shape[0]
    tn = 1536
    return pl.pallas_call(
        _mod_kernel,
        grid=(depth, n // tn),
        in_specs=[
            pl.BlockSpec((rows, D_MODEL), lambda l, j: (0, 0)),
            pl.BlockSpec((None, D_MODEL, tn), lambda l, j: (l, 0, j)),
            pl.BlockSpec((None, 1, tn), lambda l, j: (l, 0, j)),
        ],
        out_specs=pl.BlockSpec((None, rows, tn), lambda l, j: (l, 0, j)),
        out_shape=jax.ShapeDtypeStruct((depth, rows, n), F32),
        compiler_params=pltpu.CompilerParams(
            dimension_semantics=("arbitrary", "arbitrary"), vmem_limit_bytes=VMEM_LIMIT),
        name="modulation",
    )(c_all, w_mod, b_mod.reshape(depth, 1, n))


def _head_rms(a, ind, ind_t):
    ssum = _dot((a * a).astype(BF16), ind)
    r = lax.rsqrt(ssum * (1.0 / HEAD_DIM) + EPS)
    return a * _split_dot(r, ind_t)


def _head_rms_t(a_t):
    m = a_t.shape[1]
    a3 = a_t.reshape(N_HEADS, HEAD_DIM, m)
    r = lax.rsqrt(jnp.mean(a3 * a3, axis=1, keepdims=True) + EPS)
    return (a3 * r).reshape(ATT_WIDTH, m)


def _log_sigmoid(x):
    return jnp.minimum(x, 0.0) - jnp.log1p(jnp.exp(-jnp.abs(x)))


def _in_proj_kernel(*refs, n_alias, bb, tt, pos0, transposed):
    refs = refs[n_alias:]
    (x_ref, sh_ref, sc_ref, hist_ref, win_ref, wf_ref, wft_ref,
     bfr_ref, bfc_ref, qg_ref, kg_ref, ind_ref, indt_ref, wp_ref, ps_ref,
     k32_ref, v32_ref, logf_ref, pstate_ref, kbf_ref, q_ref) = refs[:21]
    vt_ref = refs[21] if transposed else None
    o = POOL_WIDTH
    wu_ref = win_ref.at[0:o, :]
    wq_ref = win_ref.at[o:o + ATT_WIDTH, :]
    wk_ref = win_ref.at[o + ATT_WIDTH:o + 2 * ATT_WIDTH, :]
    wv_ref = win_ref.at[o + 2 * ATT_WIDTH:o + 3 * ATT_WIDTH, :]
    logft_ref, py_ref, z_ref = refs[-3:]
    ti = pl.program_id(1)
    m = bb * tt

    x = x_ref[...]
    ms = jnp.mean(x * x, axis=-1, keepdims=True)
    h = (x * lax.rsqrt(ms + EPS)) * (1.0 + sc_ref[...]) + sh_ref[...]
    hb = h.reshape(m, D_MODEL).astype(BF16)

    q_raw = _dot_nt(hb, wq_ref[...])
    k_raw = _dot_nt(hb, wk_ref[...])
    v = _dot_nt(hb, wv_ref[...])
    if transposed:
        def along_lanes(g):
            return jnp.concatenate([g] * (m // LANES), axis=1)
        k_t = _head_rms_t(k_raw.T) * along_lanes(kg_ref[...])
        q_t = _head_rms_t(q_raw.T) * along_lanes(qg_ref[...])
        v_t = v.T
        k32_ref[...] = k_t
        v32_ref[...] = v_t
        kbf_ref[...] = k_t.T.reshape(bb, tt, ATT_WIDTH).astype(BF16)
        q_ref[...] = (q_t * (HEAD_DIM ** -0.5 * LOG2E)).astype(BF16)
        vt_ref[...] = v_t.astype(BF16)
    else:
        ind = ind_ref[...]
        ind_t = indt_ref[...]
        q = _head_rms(q_raw, ind, ind_t) * qg_ref[...]
        k = _head_rms(k_raw, ind, ind_t) * kg_ref[...]
        k32_ref[...] = k.reshape(bb, tt, ATT_WIDTH)
        v32_ref[...] = v.reshape(bb, tt, ATT_WIDTH)
        kbf_ref[...] = k.reshape(bb, tt, ATT_WIDTH).astype(BF16)
        q_ref[...] = (q * HEAD_DIM ** -0.5).reshape(bb, tt, ATT_WIDTH).astype(BF16)

    fl = _dot(hb, wf_ref[...]) + bfr_ref[...]
    logf_ref[...] = _log_sigmoid(fl)[:, :N_HEADS].reshape(bb, tt, N_HEADS)
    flt = _dot_nt(wft_ref[...], hb) + bfc_ref[...]
    logft_ref[...] = _log_sigmoid(flt)

    u3 = _dot_nt(hb, wu_ref[...]).reshape(bb, tt, POOL_WIDTH)

    @pl.when(ti == 0)
    def _():
        z_ref[:, 0:HIST_ROWS, :] = hist_ref[...]

    z_ref[:, HIST_ROWS:HIST_ROWS + tt, :] = u3
    acc = u3
    sums = {}
    for i in range(1, max(POOL_WINDOWS)):
        acc = acc + z_ref[:, HIST_ROWS - i:HIST_ROWS - i + tt, :]
        if i + 1 in POOL_WINDOWS:
            sums[i + 1] = acc
    group = lax.broadcasted_iota(jnp.int32, (1, 1, POOL_WIDTH), 2) // POOL_GROUP_DIM
    win = sums[POOL_WINDOWS[-1]]
    width = jnp.full((1, 1, POOL_WIDTH), float(POOL_WINDOWS[-1]), F32)
    for g in range(len(POOL_WINDOWS) - 2, -1, -1):
        win = jnp.where(group == g, sums[POOL_WINDOWS[g]], win)
        width = jnp.where(group == g, float(POOL_WINDOWS[g]), width)
    pos = (pos0 + ti * tt + lax.broadcasted_iota(jnp.int32, (1, tt, 1), 1)).astype(F32)
    count = jnp.minimum(pos + 1.0, width)
    mixed = win / count - u3
    y = _dot(mixed.reshape(m, POOL_WIDTH).astype(BF16), wp_ref[...]) * ps_ref[...]
    py_ref[...] = y.reshape(bb, tt, POOL_WIDTH).astype(BF16)

    pstate_ref[...] = z_ref[:, tt + 1:tt + HIST_ROWS, :]
    z_ref[:, 0:HIST_ROWS, :] = z_ref[:, tt:tt + HIST_ROWS, :]


def _in_proj(x, sh, sc, hist, w, slabs, layer, *, bb, tt, pos0, transposed):
    b, t, _ = x.shape
    depth = w["depth"]
    nb, nt = b // bb, t // tt
    m = bb * tt
    n_alias = 0 if slabs is None else 4
    gain_shape = (ATT_WIDTH, LANES) if transposed else (1, ATT_WIDTH)

    def per_b(shape):
        return pl.BlockSpec(shape, lambda i, j: (i,) + (0,) * (len(shape) - 1))

    in_specs = [pl.BlockSpec(memory_space=pl.ANY)] * n_alias + [
        pl.BlockSpec((bb, tt, D_MODEL), lambda i, j: (i, j, 0)),
        per_b((bb, 1, D_MODEL)), per_b((bb, 1, D_MODEL)), per_b((bb, HIST_ROWS, POOL_WIDTH)),
        _resident_layer((w["w_in"].shape[1], D_MODEL), layer),
        _resident((D_MODEL, LANES)), _resident((HEADS_PAD, D_MODEL)),
        _resident((1, LANES)), _resident((HEADS_PAD, 1)),
        _resident(gain_shape), _resident(gain_shape),
        _resident((ATT_WIDTH, LANES)), _resident((LANES, ATT_WIDTH)),
        _resident((POOL_WIDTH, POOL_WIDTH)), _resident((1, POOL_WIDTH)),
    ]
    if transposed:
        assert bb == 1
        kv_shape = jax.ShapeDtypeStruct((depth, b, ATT_WIDTH, t), F32)
        kv_spec = pl.BlockSpec((None, None, ATT_WIDTH, tt), lambda i, j: (layer, i, 0, j))
    else:
        kv_shape = jax.ShapeDtypeStruct((depth, b, t, ATT_WIDTH), F32)
        kv_spec = pl.BlockSpec((None, bb, tt, ATT_WIDTH), lambda i, j: (layer, i, j, 0))
    slab_shapes = [
        kv_shape,
        kv_shape,
        jax.ShapeDtypeStruct((depth, b, t, N_HEADS), F32),
        jax.ShapeDtypeStruct((depth, b, POOL_HIST, POOL_WIDTH), F32),
    ]
    slab_specs = [
        kv_spec,
        kv_spec,
        pl.BlockSpec((None, bb, tt, N_HEADS), lambda i, j: (layer, i, j, 0)),
        pl.BlockSpec((None, bb, POOL_HIST, POOL_WIDTH), lambda i, j: (layer, i, 0, 0)),
    ]
    if transposed:
        q_shape = jax.ShapeDtypeStruct((b, ATT_WIDTH, t), BF16)
        q_spec = pl.BlockSpec((None, ATT_WIDTH, tt), lambda i, j: (i, 0, j))
    else:
        q_shape = jax.ShapeDtypeStruct((b, t, ATT_WIDTH), BF16)
        q_spec = pl.BlockSpec((bb, tt, ATT_WIDTH), lambda i, j: (i, j, 0))
    out_shape = slab_shapes + [
        jax.ShapeDtypeStruct((b, t, ATT_WIDTH), BF16),
        q_shape,
    ] + ([q_shape] if transposed else []) + [
        jax.ShapeDtypeStruct((nb, HEADS_PAD, nt * m), F32),
        jax.ShapeDtypeStruct((b, t, POOL_WIDTH), BF16),
    ]
    out_specs = slab_specs + [
        pl.BlockSpec((bb, tt, ATT_WIDTH), lambda i, j: (i, j, 0)),
        q_spec,
    ] + ([q_spec] if transposed else []) + [
        pl.BlockSpec((None, HEADS_PAD, m), lambda i, j: (i, 0, j)),
        pl.BlockSpec((bb, tt, POOL_WIDTH), lambda i, j: (i, j, 0)),
    ]
    args = ([] if slabs is None else list(slabs)) + [
        x, sh, sc, hist, w["w_in"], w["w_f"], w["w_ft"],
        w["bf_row"], w["bf_col"], w["qg_col" if transposed else "qg"],
        w["kg_col" if transposed else "kg"], w["ind"], w["ind_t"], w["w_pool"],
        w["pool_scale"],
    ]
    outs = pl.pallas_call(
        functools.partial(_in_proj_kernel, n_alias=n_alias, bb=bb, tt=tt, pos0=pos0,
                          transposed=transposed),
        grid=(nb, nt),
        in_specs=in_specs,
        out_specs=out_specs,
        out_shape=out_shape,
        scratch_shapes=[pltpu.VMEM((bb, tt + HIST_ROWS, POOL_WIDTH), F32)],
        input_output_aliases={i: i for i in range(n_alias)},
        compiler_params=pltpu.CompilerParams(
            dimension_semantics=("arbitrary", "arbitrary"), vmem_limit_bytes=VMEM_LIMIT),
        name="in_proj_prompt" if transposed else "in_proj_sample",
    )(*args)
    return outs[:4], outs[4:]


def _cumsum_kernel(x_ref, o_ref):
    rb, length = x_ref.shape
    c = CUMSUM_CHUNK
    tri = (lax.broadcasted_iota(jnp.int32, (c, c), 0)
           <= lax.broadcasted_iota(jnp.int32, (c, c), 1)).astype(F32)
    total = jnp.zeros((rb, 1), F32)
    for i in range(length // c):
        y = total + jnp.dot(x_ref[:, i * c:(i + 1) * c], tri, precision=lax.Precision.HIGHEST,
                            preferred_element_type=F32)
        o_ref[:, i * c:(i + 1) * c] = y
        total = y[:, c - 1:c]


def _cumsum_time(x, rows_per_block):
    length = x.shape[-1]
    r = int(np.prod(x.shape[:-1]))
    rb = rows_per_block
    out = pl.pallas_call(
        _cumsum_kernel,
        grid=(r // rb,),
        in_specs=[pl.BlockSpec((rb, length), lambda i: (i, 0))],
        out_specs=pl.BlockSpec((rb, length), lambda i: (i, 0)),
        out_shape=jax.ShapeDtypeStruct((r, length), F32),
        compiler_params=pltpu.CompilerParams(
            dimension_semantics=("arbitrary",), vmem_limit_bytes=VMEM_LIMIT),
        name="cumsum_time",
    )(x.reshape(r, length))
    return out.reshape(x.shape)


def _fill_key_forget_sums(fcol_ref, fkb_ref, hp):
    chunk = 512
    for pair in range(N_HEADS // 2):
        @pl.when(hp == pair)
        def _(pair=pair):
            def fill(c, carry):
                r0 = pl.multiple_of(c * chunk, chunk)
                for hh in range(2):
                    h = 2 * pair + hh
                    col = fcol_ref[pl.ds(r0, chunk), h:h + 1]
                    fkb_ref[hh, pl.ds(r0, chunk), :] = jnp.broadcast_to(col * LOG2E, (chunk, LANES))
                return carry
            lax.fori_loop(0, fcol_ref.shape[0] // chunk, fill, 0)


def _pair_query_columns(qt):
    row_head = lax.broadcasted_iota(jnp.int32, (2 * HEAD_DIM, 1), 0) // HEAD_DIM
    zero = jnp.zeros_like(qt)
    return jnp.concatenate(
        [jnp.where(row_head == 0, qt, zero), jnp.where(row_head == 1, qt, zero)], axis=1)


def _attn_prompt_bounded_kernel(kmax_ref, qt_ref, k_ref, vt_ref, ft_ref, fcol_ref, o_ref,
                                fkb_ref, p_ref, *, tq, tk):
    hp = pl.program_id(1)
    qi = pl.program_id(2)
    n_full = (qi * tq) // tk

    @pl.when(qi == 0)
    def _():
        _fill_key_forget_sums(fcol_ref, fkb_ref, hp)

    causal = (n_full * tk + lax.broadcasted_iota(jnp.int32, (tk, tq), 0)
              <= qi * tq + lax.broadcasted_iota(jnp.int32, (tk, tq), 1))
    ones_rows = jnp.ones((HEADS_PAD, tk), BF16)
    qt = qt_ref[...]
    q2 = _pair_query_columns(qt)
    qsq = jnp.square(qt.astype(F32))
    shift = []
    for hh in range(2):
        qnorm = jnp.sqrt(jnp.sum(qsq[hh * HEAD_DIM:(hh + 1) * HEAD_DIM], axis=0, keepdims=True))
        shift.append(ft_ref[pl.ds(2 * hp + hh, 1), :] * LOG2E - qnorm * kmax_ref[0, 0])

    def weights(j, masked):
        k0 = pl.multiple_of(j * tk, tk)
        s2 = _dot(k_ref[pl.ds(k0, tk), :], q2)
        ps = []
        for hh in range(2):
            fk = fkb_ref[hh, pl.ds(k0, tk), :]
            a = (s2[:, hh * tq:(hh + 1) * tq] - jnp.concatenate([fk] * (tq // LANES), axis=1)
                 + shift[hh])
            if masked:
                a = jnp.where(causal, a, NEG)
            ps.append(jnp.exp2(a).astype(BF16))
        return jnp.concatenate(ps, axis=1)

    def accumulate(j, acc, p2):
        k0 = pl.multiple_of(j * tk, tk)
        out = []
        for hh in range(2):
            vb = jnp.concatenate(
                [vt_ref[pl.ds(hh * HEAD_DIM, HEAD_DIM), pl.ds(k0, tk)], ones_rows], axis=0)
            out.append(acc[hh] + _dot(vb, p2[:, hh * tq:(hh + 1) * tq]))
        return tuple(out)

    p_ref[...] = weights(n_full, True)

    def step(j, acc):
        p_next = weights(j, False)
        acc = accumulate(jnp.where(j == 0, n_full, j - 1), acc, p_ref[...])
        p_ref[...] = p_next
        return acc

    acc0 = jnp.zeros((HEAD_DIM + HEADS_PAD, tq), F32)
    acc = lax.fori_loop(0, n_full, step, (acc0, acc0))
    acc = accumulate(jnp.maximum(n_full - 1, 0), acc, p_ref[...])
    outs = [a[0:HEAD_DIM] / a[HEAD_DIM:HEAD_DIM + 1] for a in acc]
    o_ref[...] = jnp.concatenate(outs, axis=0).T.astype(BF16)


def _attn_prompt_kernel(qt_ref, k_ref, vt_ref, ft_ref, fcol_ref, o_ref, fkb_ref, s_ref, p_ref,
                        *, tq, tk):
    hp = pl.program_id(1)
    qi = pl.program_id(2)
    n_full = (qi * tq) // tk

    @pl.when(qi == 0)
    def _():
        _fill_key_forget_sums(fcol_ref, fkb_ref, hp)

    causal = (n_full * tk + lax.broadcasted_iota(jnp.int32, (tk, tq), 0)
              <= qi * tq + lax.broadcasted_iota(jnp.int32, (tk, tq), 1))
    ones_rows = jnp.ones((HEADS_PAD, tk), BF16)
    q2 = _pair_query_columns(qt_ref[...])
    fq = [ft_ref[pl.ds(2 * hp + hh, 1), :] * LOG2E for hh in range(2)]

    def biased_scores(j):
        k0 = pl.multiple_of(j * tk, tk)
        s2 = _dot(k_ref[pl.ds(k0, tk), :], q2)
        a, cmax = [], []
        for hh in range(2):
            fk = fkb_ref[hh, pl.ds(k0, tk), :]
            ah = s2[:, hh * tq:(hh + 1) * tq] - jnp.concatenate([fk] * (tq // LANES), axis=1)
            a.append(ah)
            cmax.append(jnp.max(ah, axis=0, keepdims=True))
        return jnp.concatenate(a, axis=1), cmax

    def weights(a2, cmax, m_run):
        ps, alphas, ms = [], [], []
        for hh in range(2):
            m_new = jnp.maximum(m_run[hh], cmax[hh] + fq[hh])
            alphas.append(jnp.exp2(m_run[hh] - m_new))
            ps.append(jnp.exp2(a2[:, hh * tq:(hh + 1) * tq] + (fq[hh] - m_new)).astype(BF16))
            ms.append(m_new)
        return ps, alphas, ms

    def accumulate(j, acc, p, alpha):
        k0 = pl.multiple_of(j * tk, tk)
        out = []
        for hh in range(2):
            vb = jnp.concatenate(
                [vt_ref[pl.ds(hh * HEAD_DIM, HEAD_DIM), pl.ds(k0, tk)], ones_rows], axis=0)
            out.append(alpha[hh] * acc[hh] + _dot(vb, p[hh]))
        return out

    a0, cmax0 = biased_scores(0)
    s_ref[...] = a0
    p_ref[...] = jnp.zeros(p_ref.shape, BF16)

    def step(j, carry):
        acc, m_run, alpha, cmax = carry
        a_cur = s_ref[...]
        p_prev = p_ref[...]
        a_next, cmax_next = biased_scores(j + 1)
        p, alpha_new, m_new = weights(a_cur, cmax, m_run)
        acc = accumulate(jnp.maximum(j - 1, 0), acc, [p_prev[:, :tq], p_prev[:, tq:]], alpha)
        s_ref[...] = a_next
        p_ref[...] = jnp.concatenate(p, axis=1)
        return tuple(acc), tuple(m_new), tuple(alpha_new), tuple(cmax_next)

    acc0 = jnp.zeros((HEAD_DIM + HEADS_PAD, tq), F32)
    m0 = jnp.full((1, tq), NEG, F32)
    one = jnp.ones((1, tq), F32)
    acc, m_run, alpha, _ = lax.fori_loop(
        0, n_full, step, ((acc0, acc0), (m0, m0), (one, one), tuple(cmax0)))
    p_prev = p_ref[...]
    a_diag = s_ref[...]
    a_diag = jnp.concatenate(
        [jnp.where(causal, a_diag[:, hh * tq:(hh + 1) * tq], NEG) for hh in range(2)], axis=1)
    cmax = [jnp.max(a_diag[:, hh * tq:(hh + 1) * tq], axis=0, keepdims=True) for hh in range(2)]
    p, alpha_new, _ = weights(a_diag, cmax, m_run)
    acc = accumulate(jnp.maximum(n_full - 1, 0), acc, [p_prev[:, :tq], p_prev[:, tq:]], alpha)
    acc = accumulate(n_full, acc, p, alpha_new)
    outs = [a[0:HEAD_DIM] / a[HEAD_DIM:HEAD_DIM + 1] for a in acc]
    o_ref[...] = jnp.concatenate(outs, axis=0).T.astype(BF16)


def _attn_prompt(qt, kbf, vt, ft, fcol, q_gain, k_gain, *, tq, tk):
    b, t, _ = kbf.shape
    pairs = N_HEADS // 2
    in_specs = [
        pl.BlockSpec((None, 2 * HEAD_DIM, tq), lambda i, p, j: (i, p, j)),
        pl.BlockSpec((None, t, 2 * HEAD_DIM), lambda i, p, j: (i, 0, p)),
        pl.BlockSpec((None, 2 * HEAD_DIM, t), lambda i, p, j: (i, p, 0)),
        pl.BlockSpec((None, HEADS_PAD, tq), lambda i, p, j: (i, 0, j)),
        pl.BlockSpec((None, t, HEADS_PAD), lambda i, p, j: (i, 0, 0)),
    ]
    common = dict(
        grid=(b, pairs, t // tq),
        out_specs=pl.BlockSpec((None, tq, 2 * HEAD_DIM), lambda i, p, j: (i, j, p)),
        out_shape=jax.ShapeDtypeStruct((b, t, ATT_WIDTH), BF16),
        compiler_params=pltpu.CompilerParams(
            dimension_semantics=("arbitrary", "arbitrary", "arbitrary"),
            vmem_limit_bytes=VMEM_LIMIT),
    )
    key_sums = pltpu.VMEM((2, t, LANES), F32)
    weights_tile = pltpu.VMEM((tk, 2 * tq), BF16)

    def running_max(_, *args):
        return pl.pallas_call(
            functools.partial(_attn_prompt_kernel, tq=tq, tk=tk),
            in_specs=in_specs,
            scratch_shapes=[key_sums, pltpu.VMEM((tk, 2 * tq), F32), weights_tile],
            name="attn_prompt", **common)(*args)

    def bounded(kmax, *args):
        return pl.pallas_call(
            functools.partial(_attn_prompt_bounded_kernel, tq=tq, tk=tk),
            in_specs=[pl.BlockSpec(memory_space=pltpu.SMEM)] + in_specs,
            scratch_shapes=[key_sums, weights_tile],
            name="attn_prompt_bounded", **common)(kmax, *args)

    qmax = 1.01 * LOG2E * jnp.max(jnp.abs(q_gain))
    kmax = 1.01 * HEAD_DIM ** 0.5 * jnp.max(jnp.abs(k_gain))
    return lax.cond(2.0 * qmax * kmax <= BOUND_SLACK_OCTAVES, bounded, running_max,
                    kmax.reshape(1, 1).astype(F32), qt, kbf, vt, ft, fcol)


def _attn_sample_kernel(q_ref, ck_ref, cv_ref, kn_ref, vn_ref, f_ref, lrow_ref, lcol_ref, o_ref,
                        qbd_ref, m_ref, l_ref, acc_ref, fq_ref, fnew_ref, *, tk, past, tt):
    i = pl.program_id(0)
    j = pl.program_id(1)
    rows = N_HEADS * tt
    own = ((lax.broadcasted_iota(jnp.int32, (rows, ATT_WIDTH), 0) // tt)
           == (lax.broadcasted_iota(jnp.int32, (rows, ATT_WIDTH), 1) // HEAD_DIM))

    def cached_forget_sums(start, size):
        blk = f_ref[:, :, pl.ds(start, size)]
        mine = lax.broadcasted_iota(jnp.int32, (1, SUBLANES, 1), 1) == i % SUBLANES
        return jnp.sum(jnp.where(mine, blk, 0.0), axis=1)

    @pl.when(j == 0)
    def _():
        q = q_ref[...]
        qbd_ref[...] = jnp.where(own, jnp.concatenate([q] * N_HEADS, axis=0), jnp.zeros((), BF16))
        m_ref[...] = jnp.full(m_ref.shape, NEG, F32)
        l_ref[...] = jnp.zeros(l_ref.shape, F32)
        acc_ref[...] = jnp.zeros(acc_ref.shape, F32)
        f_last = cached_forget_sums(past - 1, 1)
        upper = (lax.broadcasted_iota(jnp.int32, (tt, tt), 0)
                 <= lax.broadcasted_iota(jnp.int32, (tt, tt), 1)).astype(F32)
        run = jnp.dot(lrow_ref[...], upper, precision=lax.Precision.HIGHEST,
                      preferred_element_type=F32)
        fnew_ref[...] = f_last + run[:N_HEADS]
        r_i = lax.broadcasted_iota(jnp.int32, (rows, rows), 0)
        c_i = lax.broadcasted_iota(jnp.int32, (rows, rows), 1)
        lower = ((r_i // tt == c_i // tt) & (c_i <= r_i)).astype(F32)
        run_col = jnp.dot(lower, jnp.broadcast_to(lcol_ref[...], (rows, LANES)),
                          precision=lax.Precision.HIGHEST, preferred_element_type=F32)
        f_last_rows = jnp.broadcast_to(f_last[:, None, :], (N_HEADS, tt, 1)).reshape(rows, 1)
        fq_ref[...] = f_last_rows + run_col[:, 0:1]

    fq3 = fq_ref[...].reshape(N_HEADS, tt, 1)

    def update(kb, vb, fk, mask, time_on_lanes):
        if time_on_lanes:
            n = kb.shape[1]
            s = _dot(qbd_ref[...], kb.astype(BF16))
        else:
            n = kb.shape[0]
            s = _dot_nt(qbd_ref[...], kb.astype(BF16))
        s = s.reshape(N_HEADS, tt, n) + fq3 - fk[:, None, :]
        if mask is not None:
            s = jnp.where(mask, s, NEG)
        m_old = m_ref[...].reshape(N_HEADS, tt, 1)
        m_new = jnp.maximum(m_old, jnp.max(s, axis=-1, keepdims=True))
        alpha = jnp.exp(m_old - m_new)
        p = jnp.exp(s - m_new)
        l_ref[...] = (alpha * l_ref[...].reshape(N_HEADS, tt, 1)
                      + jnp.sum(p, axis=-1, keepdims=True)).reshape(rows, 1)
        m_ref[...] = m_new.reshape(rows, 1)
        pb = p.reshape(rows, n).astype(BF16)
        pv = _dot_nt(pb, vb.astype(BF16)) if time_on_lanes else _dot(pb, vb.astype(BF16))
        acc_ref[...] = alpha.reshape(rows, 1) * acc_ref[...] + pv

    k0 = pl.multiple_of(j * tk, tk)
    update(ck_ref[...], cv_ref[...], cached_forget_sums(k0, tk), None, True)

    @pl.when(j == pl.num_programs(1) - 1)
    def _():
        mask = (lax.broadcasted_iota(jnp.int32, (1, tt, tt), 2)
                <= lax.broadcasted_iota(jnp.int32, (1, tt, tt), 1))
        update(kn_ref[...], vn_ref[...], fnew_ref[...], mask, False)
        o = jnp.where(own, acc_ref[...] / l_ref[...], 0.0).reshape(N_HEADS, tt, ATT_WIDTH)
        o_ref[...] = jnp.sum(o, axis=0).astype(BF16)


def _attn_sample(q, cache_k, cache_v, k_new, v_new, f_past, logf_row, logf_col, layer, *, tk):
    nb, tt, _ = q.shape
    past = cache_k.shape[3]
    rows = N_HEADS * tt
    return pl.pallas_call(
        functools.partial(_attn_sample_kernel, tk=tk, past=past, tt=tt),
        grid=(nb, past // tk),
        in_specs=[
            pl.BlockSpec((None, tt, ATT_WIDTH), lambda i, j: (i, 0, 0)),
            pl.BlockSpec((None, None, ATT_WIDTH, tk), lambda i, j: (layer, i, 0, j)),
            pl.BlockSpec((None, None, ATT_WIDTH, tk), lambda i, j: (layer, i, 0, j)),
            pl.BlockSpec((None, None, tt, ATT_WIDTH), lambda i, j: (layer, i, 0, 0)),
            pl.BlockSpec((None, None, tt, ATT_WIDTH), lambda i, j: (layer, i, 0, 0)),
            pl.BlockSpec((None, N_HEADS, SUBLANES, past), lambda i, j: (layer, 0, i // SUBLANES, 0)),
            pl.BlockSpec((None, HEADS_PAD, tt), lambda i, j: (i, 0, 0)),
            pl.BlockSpec((None, rows, 1), lambda i, j: (i, 0, 0)),
        ],
        out_specs=pl.BlockSpec((None, tt, ATT_WIDTH), lambda i, j: (i, 0, 0)),
        out_shape=jax.ShapeDtypeStruct((nb, tt, ATT_WIDTH), BF16),
        scratch_shapes=[
            pltpu.VMEM((rows, ATT_WIDTH), BF16),
            pltpu.VMEM((rows, 1), F32),
            pltpu.VMEM((rows, 1), F32),
            pltpu.VMEM((rows, ATT_WIDTH), F32),
            pltpu.VMEM((rows, 1), F32),
            pltpu.VMEM((N_HEADS, tt), F32),
        ],
        compiler_params=pltpu.CompilerParams(
            dimension_semantics=("arbitrary", "arbitrary"), vmem_limit_bytes=VMEM_LIMIT),
        name="attn_sample",
    )(q, cache_k, cache_v, k_new, v_new, f_past, logf_row, logf_col)


def _out_mlp_kernel(x_ref, py_ref, att_ref, g1_ref, sh_ref, sc_ref, g2_ref,
                    wo_ref, wup_ref, wdn_ref, y_ref, *, bb, tt, ff_chunk):
    m = bb * tt
    mix = (_dot(py_ref[...].reshape(m, POOL_WIDTH), wo_ref[0:POOL_WIDTH, :])
           + _dot(att_ref[...].reshape(m, ATT_WIDTH), wo_ref[POOL_WIDTH:, :]))
    x1 = x_ref[...] + g1_ref[...] * mix.reshape(bb, tt, D_MODEL)
    ms = jnp.mean(x1 * x1, axis=-1, keepdims=True)
    h = (x1 * lax.rsqrt(ms + EPS)) * (1.0 + sc_ref[...]) + sh_ref[...]
    hb = h.reshape(m, D_MODEL).astype(BF16)
    ff = jnp.zeros((m, D_MODEL), F32)
    for c in range(D_FF // ff_chunk):
        up = _dot(hb, wup_ref[:, c * ff_chunk:(c + 1) * ff_chunk])
        act = jnp.square(jnp.maximum(up, 0.0)).astype(BF16)
        ff = ff + _dot(act, wdn_ref[c * ff_chunk:(c + 1) * ff_chunk, :])
    y_ref[...] = x1 + g2_ref[...] * ff.reshape(bb, tt, D_MODEL)


def _out_mlp(x, py, att, g1, sh, sc, g2, w, layer, *, bb, tt, name):
    b, t, _ = x.shape

    def tile(width):
        return pl.BlockSpec((bb, tt, width), lambda i, j: (i, j, 0))

    def per_b():
        return pl.BlockSpec((bb, 1, D_MODEL), lambda i, j: (i, 0, 0))

    return pl.pallas_call(
        functools.partial(_out_mlp_kernel, bb=bb, tt=tt, ff_chunk=1024),
        grid=(b // bb, t // tt),
        in_specs=[
            tile(D_MODEL), tile(POOL_WIDTH), tile(ATT_WIDTH),
            per_b(), per_b(), per_b(), per_b(),
            _resident_layer((POOL_WIDTH + ATT_WIDTH, D_MODEL), layer),
            _resident_layer((D_MODEL, D_FF), layer), _resident_layer((D_FF, D_MODEL), layer),
        ],
        out_specs=tile(D_MODEL),
        out_shape=jax.ShapeDtypeStruct((b, t, D_MODEL), F32),
        compiler_params=pltpu.CompilerParams(
            dimension_semantics=("arbitrary", "arbitrary"), vmem_limit_bytes=VMEM_LIMIT),
        name=name,
    )(x, py, att, g1, sh, sc, g2, w["w_out"], w["w_up"], w["w_down"])


def _layer_weights(depth, layer, stacked, w_in, b_f, q_gain, k_gain, w_pool, pool_scale):
    w_f = w_in[layer, :, POOL_WIDTH + 3 * ATT_WIDTH:]
    head_of_col = np.arange(ATT_WIDTH) // HEAD_DIM
    ind = (head_of_col[:, None] == np.arange(LANES)[None, :]).astype(np.float32)
    w_pool_bd = jnp.zeros((POOL_WIDTH, POOL_WIDTH), F32)
    for g in range(len(POOL_WINDOWS)):
        sl = slice(g * POOL_GROUP_DIM, (g + 1) * POOL_GROUP_DIM)
        w_pool_bd = w_pool_bd.at[sl, sl].set(w_pool[layer, g])
    return {
        "depth": depth,
        **stacked,
        "w_f": jnp.pad(w_f, ((0, 0), (0, LANES - N_HEADS))).astype(BF16),
        "w_ft": jnp.pad(w_f.T, ((0, HEADS_PAD - N_HEADS), (0, 0))).astype(BF16),
        "bf_row": jnp.pad(b_f[layer], (0, LANES - N_HEADS)).reshape(1, LANES),
        "bf_col": jnp.pad(b_f[layer], (0, HEADS_PAD - N_HEADS)).reshape(HEADS_PAD, 1),
        "qg": jnp.tile(q_gain[layer], N_HEADS).reshape(1, ATT_WIDTH),
        "kg": jnp.tile(k_gain[layer], N_HEADS).reshape(1, ATT_WIDTH),
        "qg_col": jnp.broadcast_to(jnp.tile(q_gain[layer], N_HEADS)[:, None], (ATT_WIDTH, LANES)),
        "kg_col": jnp.broadcast_to(jnp.tile(k_gain[layer], N_HEADS)[:, None], (ATT_WIDTH, LANES)),
        "ind": jnp.asarray(ind, BF16),
        "ind_t": jnp.asarray(ind.T, BF16),
        "w_pool": w_pool_bd.astype(BF16),
        "pool_scale": pool_scale[layer].reshape(1, POOL_WIDTH),
    }


def _mod_parts(mod_rows):
    b = mod_rows.shape[0]
    parts = mod_rows.reshape(b, 6, 1, D_MODEL)
    return [parts[:, i] for i in range(6)]


def _time_major(slab):
    depth, b, _, t = slab.shape
    return jnp.transpose(slab.reshape(depth, b, N_HEADS, HEAD_DIM, t), (0, 1, 4, 2, 3))


def kernel(x_prompt, x_sample, c_prompt, c_sample, cache_k, cache_v, cache_logf, cache_pool,
           w_mod, b_mod, w_in, b_f, q_gain, k_gain, w_pool, pool_scale, w_out, w_up, w_down):
    depth = w_mod.shape[0]
    bp, seq, _ = x_prompt.shape
    bs, dec_seq, _ = x_sample.shape
    past = cache_k.shape[2]
    tq = 512
    tk_p = 1024
    tm_in = 512
    tm_out = 512
    tk_cache = 2048

    n_c = bp + bs
    c_rows = -(-n_c // 8) * 8
    c_all = jnp.pad(jnp.concatenate([c_prompt, c_sample], axis=0), ((0, c_rows - n_c), (0, 0)))
    mod = _modulation(c_all, w_mod, b_mod)

    ck = jnp.transpose(cache_k, (0, 1, 3, 4, 2)).reshape(depth, bs, ATT_WIDTH, past)
    cv = jnp.transpose(cache_v, (0, 1, 3, 4, 2)).reshape(depth, bs, ATT_WIDTH, past)
    f_past = _cumsum_time(jnp.transpose(cache_logf, (0, 3, 1, 2)), rows_per_block=8 * bs)
    stacked = {"w_in": jnp.swapaxes(w_in, 1, 2).astype(BF16), "w_out": w_out.astype(BF16),
               "w_up": w_up.astype(BF16), "w_down": w_down.astype(BF16)}
    hist_p = jnp.zeros((bp, HIST_ROWS, POOL_WIDTH), F32)

    y_p, y_s = x_prompt, x_sample
    slabs_p = slabs_s = None
    for l in range(depth):
        w = _layer_weights(depth, l, stacked, w_in, b_f, q_gain, k_gain, w_pool, pool_scale)
        sh1, sc1, g1, sh2, sc2, g2 = _mod_parts(mod[l, :bp])
        slabs_p, (kbf, qt, vt, logft, py) = _in_proj(
            y_p, sh1, sc1, hist_p, w, slabs_p, l, bb=1, tt=tm_in, pos0=0, transposed=True)
        ft = _cumsum_time(logft, rows_per_block=bp * HEADS_PAD)
        att = _attn_prompt(qt, kbf, vt, ft, jnp.swapaxes(ft, 1, 2), q_gain[l], k_gain[l],
                           tq=tq, tk=tk_p)
        y_p = _out_mlp(y_p, py, att, g1, sh2, sc2, g2, w, l, bb=1, tt=tm_out, name="out_mlp_prompt")

        sh1, sc1, g1, sh2, sc2, g2 = _mod_parts(mod[l, bp:n_c])
        hist_s = jnp.pad(cache_pool[l], ((0, 0), (HIST_ROWS - POOL_HIST, 0), (0, 0)))
        slabs_s, (_, q_s, logft_s, py_s) = _in_proj(
            y_s, sh1, sc1, hist_s, w, slabs_s, l, bb=bs, tt=dec_seq, pos0=past, transposed=False)
        logf_row = jnp.swapaxes(logft_s.reshape(HEADS_PAD, bs, dec_seq), 0, 1)
        logf_col = logf_row[:, :N_HEADS].reshape(bs, N_HEADS * dec_seq, 1)
        att_s = _attn_sample(q_s, ck, cv, slabs_s[0], slabs_s[1], f_past, logf_row, logf_col, l,
                             tk=tk_cache)
        y_s = _out_mlp(y_s, py_s, att_s, g1, sh2, sc2, g2, w, l, bb=bs, tt=dec_seq,
                       name="out_mlp_sample")

    k_p, v_p, f_p, p_p = slabs_p
    k_s, v_s, f_s, p_s = slabs_s
    return (y_p, y_s,
            _time_major(k_p), _time_major(v_p), f_p, p_p,
            k_s.reshape(depth, bs, dec_seq, N_HEADS, HEAD_DIM), v_s.reshape(depth, bs, dec_seq, N_HEADS, HEAD_DIM),
            f_s, p_s)
```

```python
import functools

import numpy as np
import jax
import jax.numpy as jnp
from jax import lax
from jax.experimental import pallas as pl
from jax.experimental.pallas import tpu as pltpu

D_MODEL = 1024
POOL_WIDTH = 256
POOL_GROUP_DIM = 64
POOL_WINDOWS = (2, 4, 8, 16)
POOL_HIST = 15
HIST_ROWS = 16
ATT_WIDTH = 768
HEAD_DIM = 64
N_HEADS = 12
HEADS_PAD = 16
D_FF = 4096
EPS = 1e-6
NEG = -1e30
LOG2E = 1.4426950408889634
BOUND_SLACK_OCTAVES = 64.0

LANES = 128
SUBLANES = 8
CUMSUM_CHUNK = 256
VMEM_LIMIT = 56 * 1024 * 1024

F32 = jnp.float32
BF16 = jnp.bfloat16


def _dot(a, b):
    return jnp.dot(a, b, preferred_element_type=F32)


def _dot_nt(a, b):
    return lax.dot_general(a, b, (((1,), (1,)), ((), ())), preferred_element_type=F32)


def _split_dot(a, b):
    hi = a.astype(BF16)
    lo = (a - hi.astype(F32)).astype(BF16)
    return _dot(hi, b) + _dot(lo, b)


def _resident(shape):
    nd = len(shape)
    return pl.BlockSpec(shape, lambda *_: (0,) * nd, pipeline_mode=pl.Buffered(1))


def _resident_layer(shape, layer):
    nd = len(shape)
    return pl.BlockSpec((None,) + tuple(shape), lambda *_: (layer,) + (0,) * nd,
                        pipeline_mode=pl.Buffered(1))


def _mod_kernel(c_ref, w_ref, b_ref, o_ref):
    c = c_ref[...]
    s = c / (1.0 + jnp.exp(-c))
    o_ref[...] = _dot(s.astype(BF16), w_ref[...].astype(BF16)) + b_ref[...]


def _modulation(c_all, w_mod, b_mod):
    depth, _, n = w_mod.shape
    rows = c_all.shape[0]
    tn = 1536
    return pl.pallas_call(
        _mod_kernel,
        grid=(depth, n // tn),
        in_specs=[
            pl.BlockSpec((rows, D_MODEL), lambda l, j: (0, 0)),
            pl.BlockSpec((None, D_MODEL, tn), lambda l, j: (l, 0, j)),
            pl.BlockSpec((None, 1, tn), lambda l, j: (l, 0, j)),
        ],
        out_specs=pl.BlockSpec((None, rows, tn), lambda l, j: (l, 0, j)),
        out_shape=jax.ShapeDtypeStruct((depth, rows, n), F32),
        compiler_params=pltpu.CompilerParams(
            dimension_semantics=("arbitrary", "arbitrary"), vmem_limit_bytes=VMEM_LIMIT),
        name="modulation",
    )(c_all, w_mod, b_mod.reshape(depth, 1, n))


def _head_rms(a, ind, ind_t):
    ssum = _dot((a * a).astype(BF16), ind)
    r = lax.rsqrt(ssum * (1.0 / HEAD_DIM) + EPS)
    return a * _split_dot(r, ind_t)


def _head_rms_t(a_t):
    m = a_t.shape[1]
    a3 = a_t.reshape(N_HEADS, HEAD_DIM, m)
    r = lax.rsqrt(jnp.mean(a3 * a3, axis=1, keepdims=True) + EPS)
    return (a3 * r).reshape(ATT_WIDTH, m)


def _log_sigmoid(x):
    return jnp.minimum(x, 0.0) - jnp.log1p(jnp.exp(-jnp.abs(x)))


def _in_proj_kernel(*refs, n_alias, bb, tt, pos0, transposed):
    refs = refs[n_alias:]
    (x_ref, sh_ref, sc_ref, hist_ref, win_ref, wf_ref, wft_ref,
     bfr_ref, bfc_ref, qg_ref, kg_ref, ind_ref, indt_ref, wp_ref, ps_ref,
     k32_ref, v32_ref, logf_ref, pstate_ref, kbf_ref, q_ref) = refs[:21]
    vt_ref = refs[21] if transposed else None
    o = POOL_WIDTH
    wu_ref = win_ref.at[0:o, :]
    wq_ref = win_ref.at[o:o + ATT_WIDTH, :]
    wk_ref = win_ref.at[o + ATT_WIDTH:o + 2 * ATT_WIDTH, :]
    wv_ref = win_ref.at[o + 2 * ATT_WIDTH:o + 3 * ATT_WIDTH, :]
    logft_ref, py_ref, z_ref = refs[-3:]
    ti = pl.program_id(1)
    m = bb * tt

    x = x_ref[...]
    ms = jnp.mean(x * x, axis=-1, keepdims=True)
    h = (x * lax.rsqrt(ms + EPS)) * (1.0 + sc_ref[...]) + sh_ref[...]
    hb = h.reshape(m, D_MODEL).astype(BF16)

    q_raw = _dot_nt(hb, wq_ref[...])
    k_raw = _dot_nt(hb, wk_ref[...])
    v = _dot_nt(hb, wv_ref[...])
    if transposed:
        def along_lanes(g):
            return jnp.concatenate([g] * (m // LANES), axis=1)
        k_t = _head_rms_t(k_raw.T) * along_lanes(kg_ref[...])
        q_t = _head_rms_t(q_raw.T) * along_lanes(qg_ref[...])
        v_t = v.T
        k32_ref[...] = k_t
        v32_ref[...] = v_t
        kbf_ref[...] = k_t.T.reshape(bb, tt, ATT_WIDTH).astype(BF16)
        q_ref[...] = (q_t * (HEAD_DIM ** -0.5 * LOG2E)).astype(BF16)
        vt_ref[...] = v_t.astype(BF16)
    else:
        ind = ind_ref[...]
        ind_t = indt_ref[...]
        q = _head_rms(q_raw, ind, ind_t) * qg_ref[...]
        k = _head_rms(k_raw, ind, ind_t) * kg_ref[...]
        k32_ref[...] = k.reshape(bb, tt, ATT_WIDTH)
        v32_ref[...] = v.reshape(bb, tt, ATT_WIDTH)
        kbf_ref[...] = k.reshape(bb, tt, ATT_WIDTH).astype(BF16)
        q_ref[...] = (q * HEAD_DIM ** -0.5).reshape(bb, tt, ATT_WIDTH).astype(BF16)

    fl = _dot(hb, wf_ref[...]) + bfr_ref[...]
    logf_ref[...] = _log_sigmoid(fl)[:, :N_HEADS].reshape(bb, tt, N_HEADS)
    flt = _dot_nt(wft_ref[...], hb) + bfc_ref[...]
    logft_ref[...] = _log_sigmoid(flt)

    u3 = _dot_nt(hb, wu_ref[...]).reshape(bb, tt, POOL_WIDTH)

    @pl.when(ti == 0)
    def _():
        z_ref[:, 0:HIST_ROWS, :] = hist_ref[...]

    z_ref[:, HIST_ROWS:HIST_ROWS + tt, :] = u3
    acc = u3
    sums = {}
    for i in range(1, max(POOL_WINDOWS)):
        acc = acc + z_ref[:, HIST_ROWS - i:HIST_ROWS - i + tt, :]
        if i + 1 in POOL_WINDOWS:
            sums[i + 1] = acc
    group = lax.broadcasted_iota(jnp.int32, (1, 1, POOL_WIDTH), 2) // POOL_GROUP_DIM
    win = sums[POOL_WINDOWS[-1]]
    width = jnp.full((1, 1, POOL_WIDTH), float(POOL_WINDOWS[-1]), F32)
    for g in range(len(POOL_WINDOWS) - 2, -1, -1):
        win = jnp.where(group == g, sums[POOL_WINDOWS[g]], win)
        width = jnp.where(group == g, float(POOL_WINDOWS[g]), width)
    pos = (pos0 + ti * tt + lax.broadcasted_iota(jnp.int32, (1, tt, 1), 1)).astype(F32)
    count = jnp.minimum(pos + 1.0, width)
    mixed = win / count - u3
    y = _dot(mixed.reshape(m, POOL_WIDTH).astype(BF16), wp_ref[...]) * ps_ref[...]
    py_ref[...] = y.reshape(bb, tt, POOL_WIDTH).astype(BF16)

    pstate_ref[...] = z_ref[:, tt + 1:tt + HIST_ROWS, :]
    z_ref[:, 0:HIST_ROWS, :] = z_ref[:, tt:tt + HIST_ROWS, :]


def _in_proj(x, sh, sc, hist, w, slabs, layer, *, bb, tt, pos0, transposed):
    b, t, _ = x.shape
    depth = w["depth"]
    nb, nt = b // bb, t // tt
    m = bb * tt
    n_alias = 0 if slabs is None else 4
    gain_shape = (ATT_WIDTH, LANES) if transposed else (1, ATT_WIDTH)

    def per_b(shape):
        return pl.BlockSpec(shape, lambda i, j: (i,) + (0,) * (len(shape) - 1))

    in_specs = [pl.BlockSpec(memory_space=pl.ANY)] * n_alias + [
        pl.BlockSpec((bb, tt, D_MODEL), lambda i, j: (i, j, 0)),
        per_b((bb, 1, D_MODEL)), per_b((bb, 1, D_MODEL)), per_b((bb, HIST_ROWS, POOL_WIDTH)),
        _resident_layer((w["w_in"].shape[1], D_MODEL), layer),
        _resident((D_MODEL, LANES)), _resident((HEADS_PAD, D_MODEL)),
        _resident((1, LANES)), _resident((HEADS_PAD, 1)),
        _resident(gain_shape), _resident(gain_shape),
        _resident((ATT_WIDTH, LANES)), _resident((LANES, ATT_WIDTH)),
        _resident((POOL_WIDTH, POOL_WIDTH)), _resident((1, POOL_WIDTH)),
    ]
    if transposed:
        assert bb == 1
        kv_shape = jax.ShapeDtypeStruct((depth, b, ATT_WIDTH, t), F32)
        kv_spec = pl.BlockSpec((None, None, ATT_WIDTH, tt), lambda i, j: (layer, i, 0, j))
    else:
        kv_shape = jax.ShapeDtypeStruct((depth, b, t, ATT_WIDTH), F32)
        kv_spec = pl.BlockSpec((None, bb, tt, ATT_WIDTH), lambda i, j: (layer, i, j, 0))
    slab_shapes = [
        kv_shape,
        kv_shape,
        jax.ShapeDtypeStruct((depth, b, t, N_HEADS), F32),
        jax.ShapeDtypeStruct((depth, b, POOL_HIST, POOL_WIDTH), F32),
    ]
    slab_specs = [
        kv_spec,
        kv_spec,
        pl.BlockSpec((None, bb, tt, N_HEADS), lambda i, j: (layer, i, j, 0)),
        pl.BlockSpec((None, bb, POOL_HIST, POOL_WIDTH), lambda i, j: (layer, i, 0, 0)),
    ]
    if transposed:
        q_shape = jax.ShapeDtypeStruct((b, ATT_WIDTH, t), BF16)
        q_spec = pl.BlockSpec((None, ATT_WIDTH, tt), lambda i, j: (i, 0, j))
    else:
        q_shape = jax.ShapeDtypeStruct((b, t, ATT_WIDTH), BF16)
        q_spec = pl.BlockSpec((bb, tt, ATT_WIDTH), lambda i, j: (i, j, 0))
    out_shape = slab_shapes + [
        jax.ShapeDtypeStruct((b, t, ATT_WIDTH), BF16),
        q_shape,
    ] + ([q_shape] if transposed else []) + [
        jax.ShapeDtypeStruct((nb, HEADS_PAD, nt * m), F32),
        jax.ShapeDtypeStruct((b, t, POOL_WIDTH), BF16),
    ]
    out_specs = slab_specs + [
        pl.BlockSpec((bb, tt, ATT_WIDTH), lambda i, j: (i, j, 0)),
        q_spec,
    ] + ([q_spec] if transposed else []) + [
        pl.BlockSpec((None, HEADS_PAD, m), lambda i, j: (i, 0, j)),
        pl.BlockSpec((bb, tt, POOL_WIDTH), lambda i, j: (i, j, 0)),
    ]
    args = ([] if slabs is None else list(slabs)) + [
        x, sh, sc, hist, w["w_in"], w["w_f"], w["w_ft"],
        w["bf_row"], w["bf_col"], w["qg_col" if transposed else "qg"],
        w["kg_col" if transposed else "kg"], w["ind"], w["ind_t"], w["w_pool"],
        w["pool_scale"],
    ]
    outs = pl.pallas_call(
        functools.partial(_in_proj_kernel, n_alias=n_alias, bb=bb, tt=tt, pos0=pos0,
                          transposed=transposed),
        grid=(nb, nt),
        in_specs=in_specs,
        out_specs=out_specs,
        out_shape=out_shape,
        scratch_shapes=[pltpu.VMEM((bb, tt + HIST_ROWS, POOL_WIDTH), F32)],
        input_output_aliases={i: i for i in range(n_alias)},
        compiler_params=pltpu.CompilerParams(
            dimension_semantics=("arbitrary", "arbitrary"), vmem_limit_bytes=VMEM_LIMIT),
        name="in_proj_prompt" if transposed else "in_proj_sample",
    )(*args)
    return outs[:4], outs[4:]


def _cumsum_kernel(x_ref, o_ref):
    rb, length = x_ref.shape
    c = CUMSUM_CHUNK
    tri = (lax.broadcasted_iota(jnp.int32, (c, c), 0)
           <= lax.broadcasted_iota(jnp.int32, (c, c), 1)).astype(F32)
    total = jnp.zeros((rb, 1), F32)
    for i in range(length // c):
        y = total + jnp.dot(x_ref[:, i * c:(i + 1) * c], tri, precision=lax.Precision.HIGHEST,
                            preferred_element_type=F32)
        o_ref[:, i * c:(i + 1) * c] = y
        total = y[:, c - 1:c]


def _cumsum_time(x, rows_per_block):
    length = x.shape[-1]
    r = int(np.prod(x.shape[:-1]))
    rb = rows_per_block
    out = pl.pallas_call(
        _cumsum_kernel,
        grid=(r // rb,),
        in_specs=[pl.BlockSpec((rb, length), lambda i: (i, 0))],
        out_specs=pl.BlockSpec((rb, length), lambda i: (i, 0)),
        out_shape=jax.ShapeDtypeStruct((r, length), F32),
        compiler_params=pltpu.CompilerParams(
            dimension_semantics=("arbitrary",), vmem_limit_bytes=VMEM_LIMIT),
        name="cumsum_time",
    )(x.reshape(r, length))
    return out.reshape(x.shape)


def _fill_key_forget_sums(fcol_ref, fkb_ref, hp):
    chunk = 512
    for pair in range(N_HEADS // 2):
        @pl.when(hp == pair)
        def _(pair=pair):
            def fill(c, carry):
                r0 = pl.multiple_of(c * chunk, chunk)
                for hh in range(2):
                    h = 2 * pair + hh
                    col = fcol_ref[pl.ds(r0, chunk), h:h + 1]
                    fkb_ref[hh, pl.ds(r0, chunk), :] = jnp.broadcast_to(col * LOG2E, (chunk, LANES))
                return carry
            lax.fori_loop(0, fcol_ref.shape[0] // chunk, fill, 0)


def _pair_query_columns(qt):
    row_head = lax.broadcasted_iota(jnp.int32, (2 * HEAD_DIM, 1), 0) // HEAD_DIM
    zero = jnp.zeros_like(qt)
    return jnp.concatenate(
        [jnp.where(row_head == 0, qt, zero), jnp.where(row_head == 1, qt, zero)], axis=1)


def _attn_prompt_bounded_kernel(kmax_ref, qt_ref, k_ref, vt_ref, ft_ref, fcol_ref, o_ref,
                                fkb_ref, p_ref, *, tq, tk):
    hp = pl.program_id(1)
    qi = pl.program_id(2)
    n_full = (qi * tq) // tk

    @pl.when(qi == 0)
    def _():
        _fill_key_forget_sums(fcol_ref, fkb_ref, hp)

    causal = (n_full * tk + lax.broadcasted_iota(jnp.int32, (tk, tq), 0)
              <= qi * tq + lax.broadcasted_iota(jnp.int32, (tk, tq), 1))
    ones_rows = jnp.ones((HEADS_PAD, tk), BF16)
    qt = qt_ref[...]
    q2 = _pair_query_columns(qt)
    qsq = jnp.square(qt.astype(F32))
    shift = []
    for hh in range(2):
        qnorm = jnp.sqrt(jnp.sum(qsq[hh * HEAD_DIM:(hh + 1) * HEAD_DIM], axis=0, keepdims=True))
        shift.append(ft_ref[pl.ds(2 * hp + hh, 1), :] * LOG2E - qnorm * kmax_ref[0, 0])

    def weights(j, masked):
        k0 = pl.multiple_of(j * tk, tk)
        s2 = _dot(k_ref[pl.ds(k0, tk), :], q2)
        ps = []
        for hh in range(2):
            fk = fkb_ref[hh, pl.ds(k0, tk), :]
            a = (s2[:, hh * tq:(hh + 1) * tq] - jnp.concatenate([fk] * (tq // LANES), axis=1)
                 + shift[hh])
            if masked:
                a = jnp.where(causal, a, NEG)
            ps.append(jnp.exp2(a).astype(BF16))
        return jnp.concatenate(ps, axis=1)

    def accumulate(j, acc, p2):
        k0 = pl.multiple_of(j * tk, tk)
        out = []
        for hh in range(2):
            vb = jnp.concatenate(
                [vt_ref[pl.ds(hh * HEAD_DIM, HEAD_DIM), pl.ds(k0, tk)], ones_rows], axis=0)
            out.append(acc[hh] + _dot(vb, p2[:, hh * tq:(hh + 1) * tq]))
        return tuple(out)

    p_ref[...] = weights(n_full, True)

    def step(j, acc):
        p_next = weights(j, False)
        acc = accumulate(jnp.where(j == 0, n_full, j - 1), acc, p_ref[...])
        p_ref[...] = p_next
        return acc

    acc0 = jnp.zeros((HEAD_DIM + HEADS_PAD, tq), F32)
    acc = lax.fori_loop(0, n_full, step, (acc0, acc0))
    acc = accumulate(jnp.maximum(n_full - 1, 0), acc, p_ref[...])
    outs = [a[0:HEAD_DIM] / a[HEAD_DIM:HEAD_DIM + 1] for a in acc]
    o_ref[...] = jnp.concatenate(outs, axis=0).T.astype(BF16)


def _attn_prompt_kernel(qt_ref, k_ref, vt_ref, ft_ref, fcol_ref, o_ref, fkb_ref, s_ref, p_ref,
                        *, tq, tk):
    hp = pl.program_id(1)
    qi = pl.program_id(2)
    n_full = (qi * tq) // tk

    @pl.when(qi == 0)
    def _():
        _fill_key_forget_sums(fcol_ref, fkb_ref, hp)

    causal = (n_full * tk + lax.broadcasted_iota(jnp.int32, (tk, tq), 0)
              <= qi * tq + lax.broadcasted_iota(jnp.int32, (tk, tq), 1))
    ones_rows = jnp.ones((HEADS_PAD, tk), BF16)
    q2 = _pair_query_columns(qt_ref[...])
    fq = [ft_ref[pl.ds(2 * hp + hh, 1), :] * LOG2E for hh in range(2)]

    def biased_scores(j):
        k0 = pl.multiple_of(j * tk, tk)
        s2 = _dot(k_ref[pl.ds(k0, tk), :], q2)
        a, cmax = [], []
        for hh in range(2):
            fk = fkb_ref[hh, pl.ds(k0, tk), :]
            ah = s2[:, hh * tq:(hh + 1) * tq] - jnp.concatenate([fk] * (tq // LANES), axis=1)
            a.append(ah)
            cmax.append(jnp.max(ah, axis=0, keepdims=True))
        return jnp.concatenate(a, axis=1), cmax

    def weights(a2, cmax, m_run):
        ps, alphas, ms = [], [], []
        for hh in range(2):
            m_new = jnp.maximum(m_run[hh], cmax[hh] + fq[hh])
            alphas.append(jnp.exp2(m_run[hh] - m_new))
            ps.append(jnp.exp2(a2[:, hh * tq:(hh + 1) * tq] + (fq[hh] - m_new)).astype(BF16))
            ms.append(m_new)
        return ps, alphas, ms

    def accumulate(j, acc, p, alpha):
        k0 = pl.multiple_of(j * tk, tk)
        out = []
        for hh in range(2):
            vb = jnp.concatenate(
                [vt_ref[pl.ds(hh * HEAD_DIM, HEAD_DIM), pl.ds(k0, tk)], ones_rows], axis=0)
            out.append(alpha[hh] * acc[hh] + _dot(vb, p[hh]))
        return out

    a0, cmax0 = biased_scores(0)
    s_ref[...] = a0
    p_ref[...] = jnp.zeros(p_ref.shape, BF16)

    def step(j, carry):
        acc, m_run, alpha, cmax = carry
        a_cur = s_ref[...]
        p_prev = p_ref[...]
        a_next, cmax_next = biased_scores(j + 1)
        p, alpha_new, m_new = weights(a_cur, cmax, m_run)
        acc = accumulate(jnp.maximum(j - 1, 0), acc, [p_prev[:, :tq], p_prev[:, tq:]], alpha)
        s_ref[...] = a_next
        p_ref[...] = jnp.concatenate(p, axis=1)
        return tuple(acc), tuple(m_new), tuple(alpha_new), tuple(cmax_next)

    acc0 = jnp.zeros((HEAD_DIM + HEADS_PAD, tq), F32)
    m0 = jnp.full((1, tq), NEG, F32)
    one = jnp.ones((1, tq), F32)
    acc, m_run, alpha, _ = lax.fori_loop(
        0, n_full, step, ((acc0, acc0), (m0, m0), (one, one), tuple(cmax0)))
    p_prev = p_ref[...]
    a_diag = s_ref[...]
    a_diag = jnp.concatenate(
        [jnp.where(causal, a_diag[:, hh * tq:(hh + 1) * tq], NEG) for hh in range(2)], axis=1)
    cmax = [jnp.max(a_diag[:, hh * tq:(hh + 1) * tq], axis=0, keepdims=True) for hh in range(2)]
    p, alpha_new, _ = weights(a_diag, cmax, m_run)
    acc = accumulate(jnp.maximum(n_full - 1, 0), acc, [p_prev[:, :tq], p_prev[:, tq:]], alpha)
    acc = accumulate(n_full, acc, p, alpha_new)
    outs = [a[0:HEAD_DIM] / a[HEAD_DIM:HEAD_DIM + 1] for a in acc]
    o_ref[...] = jnp.concatenate(outs, axis=0).T.astype(BF16)


def _attn_prompt(qt, kbf, vt, ft, fcol, q_gain, k_gain, *, tq, tk):
    b, t, _ = kbf.shape
    pairs = N_HEADS // 2
    in_specs = [
        pl.BlockSpec((None, 2 * HEAD_DIM, tq), lambda i, p, j: (i, p, j)),
        pl.BlockSpec((None, t, 2 * HEAD_DIM), lambda i, p, j: (i, 0, p)),
        pl.BlockSpec((None, 2 * HEAD_DIM, t), lambda i, p, j: (i, p, 0)),
        pl.BlockSpec((None, HEADS_PAD, tq), lambda i, p, j: (i, 0, j)),
        pl.BlockSpec((None, t, HEADS_PAD), lambda i, p, j: (i, 0, 0)),
    ]
    common = dict(
        grid=(b, pairs, t // tq),
        out_specs=pl.BlockSpec((None, tq, 2 * HEAD_DIM), lambda i, p, j: (i, j, p)),
        out_shape=jax.ShapeDtypeStruct((b, t, ATT_WIDTH), BF16),
        compiler_params=pltpu.CompilerParams(
            dimension_semantics=("arbitrary", "arbitrary", "arbitrary"),
            vmem_limit_bytes=VMEM_LIMIT),
    )
    key_sums = pltpu.VMEM((2, t, LANES), F32)
    weights_tile = pltpu.VMEM((tk, 2 * tq), BF16)

    def running_max(_, *args):
        return pl.pallas_call(
            functools.partial(_attn_prompt_kernel, tq=tq, tk=tk),
            in_specs=in_specs,
            scratch_shapes=[key_sums, pltpu.VMEM((tk, 2 * tq), F32), weights_tile],
            name="attn_prompt", **common)(*args)

    def bounded(kmax, *args):
        return pl.pallas_call(
            functools.partial(_attn_prompt_bounded_kernel, tq=tq, tk=tk),
            in_specs=[pl.BlockSpec(memory_space=pltpu.SMEM)] + in_specs,
            scratch_shapes=[key_sums, weights_tile],
            name="attn_prompt_bounded", **common)(kmax, *args)

    qmax = 1.01 * LOG2E * jnp.max(jnp.abs(q_gain))
    kmax = 1.01 * HEAD_DIM ** 0.5 * jnp.max(jnp.abs(k_gain))
    return lax.cond(2.0 * qmax * kmax <= BOUND_SLACK_OCTAVES, bounded, running_max,
                    kmax.reshape(1, 1).astype(F32), qt, kbf, vt, ft, fcol)


def _attn_sample_kernel(q_ref, ck_ref, cv_ref, kn_ref, vn_ref, f_ref, lrow_ref, lcol_ref, o_ref,
                        qbd_ref, m_ref, l_ref, acc_ref, fq_ref, fnew_ref, *, tk, past, tt):
    i = pl.program_id(0)
    j = pl.program_id(1)
    rows = N_HEADS * tt
    own = ((lax.broadcasted_iota(jnp.int32, (rows, ATT_WIDTH), 0) // tt)
           == (lax.broadcasted_iota(jnp.int32, (rows, ATT_WIDTH), 1) // HEAD_DIM))

    def cached_forget_sums(start, size):
        blk = f_ref[:, :, pl.ds(start, size)]
        mine = lax.broadcasted_iota(jnp.int32, (1, SUBLANES, 1), 1) == i % SUBLANES
        return jnp.sum(jnp.where(mine, blk, 0.0), axis=1)

    @pl.when(j == 0)
    def _():
        q = q_ref[...]
        qbd_ref[...] = jnp.where(own, jnp.concatenate([q] * N_HEADS, axis=0), jnp.zeros((), BF16))
        m_ref[...] = jnp.full(m_ref.shape, NEG, F32)
        l_ref[...] = jnp.zeros(l_ref.shape, F32)
        acc_ref[...] = jnp.zeros(acc_ref.shape, F32)
        f_last = cached_forget_sums(past - 1, 1)
        upper = (lax.broadcasted_iota(jnp.int32, (tt, tt), 0)
                 <= lax.broadcasted_iota(jnp.int32, (tt, tt), 1)).astype(F32)
        run = jnp.dot(lrow_ref[...], upper, precision=lax.Precision.HIGHEST,
                      preferred_element_type=F32)
        fnew_ref[...] = f_last + run[:N_HEADS]
        r_i = lax.broadcasted_iota(jnp.int32, (rows, rows), 0)
        c_i = lax.broadcasted_iota(jnp.int32, (rows, rows), 1)
        lower = ((r_i // tt == c_i // tt) & (c_i <= r_i)).astype(F32)
        run_col = jnp.dot(lower, jnp.broadcast_to(lcol_ref[...], (rows, LANES)),
                          precision=lax.Precision.HIGHEST, preferred_element_type=F32)
        f_last_rows = jnp.broadcast_to(f_last[:, None, :], (N_HEADS, tt, 1)).reshape(rows, 1)
        fq_ref[...] = f_last_rows + run_col[:, 0:1]

    fq3 = fq_ref[...].reshape(N_HEADS, tt, 1)

    def update(kb, vb, fk, mask, time_on_lanes):
        if time_on_lanes:
            n = kb.shape[1]
            s = _dot(qbd_ref[...], kb.astype(BF16))
        else:
            n = kb.shape[0]
            s = _dot_nt(qbd_ref[...], kb.astype(BF16))
        s = s.reshape(N_HEADS, tt, n) + fq3 - fk[:, None, :]
        if mask is not None:
            s = jnp.where(mask, s, NEG)
        m_old = m_ref[...].reshape(N_HEADS, tt, 1)
        m_new = jnp.maximum(m_old, jnp.max(s, axis=-1, keepdims=True))
        alpha = jnp.exp(m_old - m_new)
        p = jnp.exp(s - m_new)
        l_ref[...] = (alpha * l_ref[...].reshape(N_HEADS, tt, 1)
                      + jnp.sum(p, axis=-1, keepdims=True)).reshape(rows, 1)
        m_ref[...] = m_new.reshape(rows, 1)
        pb = p.reshape(rows, n).astype(BF16)
        pv = _dot_nt(pb, vb.astype(BF16)) if time_on_lanes else _dot(pb, vb.astype(BF16))
        acc_ref[...] = alpha.reshape(rows, 1) * acc_ref[...] + pv

    k0 = pl.multiple_of(j * tk, tk)
    update(ck_ref[...], cv_ref[...], cached_forget_sums(k0, tk), None, True)

    @pl.when(j == pl.num_programs(1) - 1)
    def _():
        mask = (lax.broadcasted_iota(jnp.int32, (1, tt, tt), 2)
                <= lax.broadcasted_iota(jnp.int32, (1, tt, tt), 1))
        update(kn_ref[...], vn_ref[...], fnew_ref[...], mask, False)
        o = jnp.where(own, acc_ref[...] / l_ref[...], 0.0).reshape(N_HEADS, tt, ATT_WIDTH)
        o_ref[...] = jnp.sum(o, axis=0).astype(BF16)


def _attn_sample(q, cache_k, cache_v, k_new, v_new, f_past, logf_row, logf_col, layer, *, tk):
    nb, tt, _ = q.shape
    past = cache_k.shape[3]
    rows = N_HEADS * tt
    return pl.pallas_call(
        functools.partial(_attn_sample_kernel, tk=tk, past=past, tt=tt),
        grid=(nb, past // tk),
        in_specs=[
            pl.BlockSpec((None, tt, ATT_WIDTH), lambda i, j: (i, 0, 0)),
            pl.BlockSpec((None, None, ATT_WIDTH, tk), lambda i, j: (layer, i, 0, j)),
            pl.BlockSpec((None, None, ATT_WIDTH, tk), lambda i, j: (layer, i, 0, j)),
            pl.BlockSpec((None, None, tt, ATT_WIDTH), lambda i, j: (layer, i, 0, 0)),
            pl.BlockSpec((None, None, tt, ATT_WIDTH), lambda i, j: (layer, i, 0, 0)),
            pl.BlockSpec((None, N_HEADS, SUBLANES, past), lambda i, j: (layer, 0, i // SUBLANES, 0)),
            pl.BlockSpec((None, HEADS_PAD, tt), lambda i, j: (i, 0, 0)),
            pl.BlockSpec((None, rows, 1), lambda i, j: (i, 0, 0)),
        ],
        out_specs=pl.BlockSpec((None, tt, ATT_WIDTH), lambda i, j: (i, 0, 0)),
        out_shape=jax.ShapeDtypeStruct((nb, tt, ATT_WIDTH), BF16),
        scratch_shapes=[
            pltpu.VMEM((rows, ATT_WIDTH), BF16),
            pltpu.VMEM((rows, 1), F32),
            pltpu.VMEM((rows, 1), F32),
            pltpu.VMEM((rows, ATT_WIDTH), F32),
            pltpu.VMEM((rows, 1), F32),
            pltpu.VMEM((N_HEADS, tt), F32),
        ],
        compiler_params=pltpu.CompilerParams(
            dimension_semantics=("arbitrary", "arbitrary"), vmem_limit_bytes=VMEM_LIMIT),
        name="attn_sample",
    )(q, cache_k, cache_v, k_new, v_new, f_past, logf_row, logf_col)


def _out_mlp_kernel(x_ref, py_ref, att_ref, g1_ref, sh_ref, sc_ref, g2_ref,
                    wo_ref, wup_ref, wdn_ref, y_ref, *, bb, tt, ff_chunk):
    m = bb * tt
    mix = (_dot(py_ref[...].reshape(m, POOL_WIDTH), wo_ref[0:POOL_WIDTH, :])
           + _dot(att_ref[...].reshape(m, ATT_WIDTH), wo_ref[POOL_WIDTH:, :]))
    x1 = x_ref[...] + g1_ref[...] * mix.reshape(bb, tt, D_MODEL)
    ms = jnp.mean(x1 * x1, axis=-1, keepdims=True)
    h = (x1 * lax.rsqrt(ms + EPS)) * (1.0 + sc_ref[...]) + sh_ref[...]
    hb = h.reshape(m, D_MODEL).astype(BF16)
    ff = jnp.zeros((m, D_MODEL), F32)
    for c in range(D_FF // ff_chunk):
        up = _dot(hb, wup_ref[:, c * ff_chunk:(c + 1) * ff_chunk])
        act = jnp.square(jnp.maximum(up, 0.0)).astype(BF16)
        ff = ff + _dot(act, wdn_ref[c * ff_chunk:(c + 1) * ff_chunk, :])
    y_ref[...] = x1 + g2_ref[...] * ff.reshape(bb, tt, D_MODEL)


def _out_mlp(x, py, att, g1, sh, sc, g2, w, layer, *, bb, tt, name):
    b, t, _ = x.shape

    def tile(width):
        return pl.BlockSpec((bb, tt, width), lambda i, j: (i, j, 0))

    def per_b():
        return pl.BlockSpec((bb, 1, D_MODEL), lambda i, j: (i, 0, 0))

    return pl.pallas_call(
        functools.partial(_out_mlp_kernel, bb=bb, tt=tt, ff_chunk=1024),
        grid=(b // bb, t // tt),
        in_specs=[
            tile(D_MODEL), tile(POOL_WIDTH), tile(ATT_WIDTH),
            per_b(), per_b(), per_b(), per_b(),
            _resident_layer((POOL_WIDTH + ATT_WIDTH, D_MODEL), layer),
            _resident_layer((D_MODEL, D_FF), layer), _resident_layer((D_FF, D_MODEL), layer),
        ],
        out_specs=tile(D_MODEL),
        out_shape=jax.ShapeDtypeStruct((b, t, D_MODEL), F32),
        compiler_params=pltpu.CompilerParams(
            dimension_semantics=("arbitrary", "arbitrary"), vmem_limit_bytes=VMEM_LIMIT),
        name=name,
    )(x, py, att, g1, sh, sc, g2, w["w_out"], w["w_up"], w["w_down"])


def _layer_weights(depth, layer, stacked, w_in, b_f, q_gain, k_gain, w_pool, pool_scale):
    w_f = w_in[layer, :, POOL_WIDTH + 3 * ATT_WIDTH:]
    head_of_col = np.arange(ATT_WIDTH) // HEAD_DIM
    ind = (head_of_col[:, None] == np.arange(LANES)[None, :]).astype(np.float32)
    w_pool_bd = jnp.zeros((POOL_WIDTH, POOL_WIDTH), F32)
    for g in range(len(POOL_WINDOWS)):
        sl = slice(g * POOL_GROUP_DIM, (g + 1) * POOL_GROUP_DIM)
        w_pool_bd = w_pool_bd.at[sl, sl].set(w_pool[layer, g])
    return {
        "depth": depth,
        **stacked,
        "w_f": jnp.pad(w_f, ((0, 0), (0, LANES - N_HEADS))).astype(BF16),
        "w_ft": jnp.pad(w_f.T, ((0, HEADS_PAD - N_HEADS), (0, 0))).astype(BF16),
        "bf_row": jnp.pad(b_f[layer], (0, LANES - N_HEADS)).reshape(1, LANES),
        "bf_col": jnp.pad(b_f[layer], (0, HEADS_PAD - N_HEADS)).reshape(HEADS_PAD, 1),
        "qg": jnp.tile(q_gain[layer], N_HEADS).reshape(1, ATT_WIDTH),
        "kg": jnp.tile(k_gain[layer], N_HEADS).reshape(1, ATT_WIDTH),
        "qg_col": jnp.broadcast_to(jnp.tile(q_gain[layer], N_HEADS)[:, None], (ATT_WIDTH, LANES)),
        "kg_col": jnp.broadcast_to(jnp.tile(k_gain[layer], N_HEADS)[:, None], (ATT_WIDTH, LANES)),
        "ind": jnp.asarray(ind, BF16),
        "ind_t": jnp.asarray(ind.T, BF16),
        "w_pool": w_pool_bd.astype(BF16),
        "pool_scale": pool_scale[layer].reshape(1, POOL_WIDTH),
    }


def _mod_parts(mod_rows):
    b = mod_rows.shape[0]
    parts = mod_rows.reshape(b, 6, 1, D_MODEL)
    return [parts[:, i] for i in range(6)]


def _time_major(slab):
    depth, b, _, t = slab.shape
    return jnp.transpose(slab.reshape(depth, b, N_HEADS, HEAD_DIM, t), (0, 1, 4, 2, 3))


def kernel(x_prompt, x_sample, c_prompt, c_sample, cache_k, cache_v, cache_logf, cache_pool,
           w_mod, b_mod, w_in, b_f, q_gain, k_gain, w_pool, pool_scale, w_out, w_up, w_down):
    depth = w_mod.shape[0]
    bp, seq, _ = x_prompt.shape
    bs, dec_seq, _ = x_sample.shape
    past = cache_k.shape[2]
    tq = 1024
    tk_p = 1024
    tm_in = 512
    tm_out = 512
    tk_cache = 2048

    n_c = bp + bs
    c_rows = -(-n_c // 8) * 8
    c_all = jnp.pad(jnp.concatenate([c_prompt, c_sample], axis=0), ((0, c_rows - n_c), (0, 0)))
    mod = _modulation(c_all, w_mod, b_mod)

    ck = jnp.transpose(cache_k, (0, 1, 3, 4, 2)).reshape(depth, bs, ATT_WIDTH, past)
    cv = jnp.transpose(cache_v, (0, 1, 3, 4, 2)).reshape(depth, bs, ATT_WIDTH, past)
    f_past = _cumsum_time(jnp.transpose(cache_logf, (0, 3, 1, 2)), rows_per_block=8 * bs)
    stacked = {"w_in": jnp.swapaxes(w_in, 1, 2).astype(BF16), "w_out": w_out.astype(BF16),
               "w_up": w_up.astype(BF16), "w_down": w_down.astype(BF16)}
    hist_p = jnp.zeros((bp, HIST_ROWS, POOL_WIDTH), F32)

    y_p, y_s = x_prompt, x_sample
    slabs_p = slabs_s = None
    for l in range(depth):
        w = _layer_weights(depth, l, stacked, w_in, b_f, q_gain, k_gain, w_pool, pool_scale)
        sh1, sc1, g1, sh2, sc2, g2 = _mod_parts(mod[l, :bp])
        slabs_p, (kbf, qt, vt, logft, py) = _in_proj(
            y_p, sh1, sc1, hist_p, w, slabs_p, l, bb=1, tt=tm_in, pos0=0, transposed=True)
        ft = _cumsum_time(logft, rows_per_block=bp * HEADS_PAD)
        att = _attn_prompt(qt, kbf, vt, ft, jnp.swapaxes(ft, 1, 2), q_gain[l], k_gain[l],
                           tq=tq, tk=tk_p)
        y_p = _out_mlp(y_p, py, att, g1, sh2, sc2, g2, w, l, bb=1, tt=tm_out, name="out_mlp_prompt")

        sh1, sc1, g1, sh2, sc2, g2 = _mod_parts(mod[l, bp:n_c])
        hist_s = jnp.pad(cache_pool[l], ((0, 0), (HIST_ROWS - POOL_HIST, 0), (0, 0)))
        slabs_s, (_, q_s, logft_s, py_s) = _in_proj(
            y_s, sh1, sc1, hist_s, w, slabs_s, l, bb=bs, tt=dec_seq, pos0=past, transposed=False)
        logf_row = jnp.swapaxes(logft_s.reshape(HEADS_PAD, bs, dec_seq), 0, 1)
        logf_col = logf_row[:, :N_HEADS].reshape(bs, N_HEADS * dec_seq, 1)
        att_s = _attn_sample(q_s, ck, cv, slabs_s[0], slabs_s[1], f_past, logf_row, logf_col, l,
                             tk=tk_cache)
        y_s = _out_mlp(y_s, py_s, att_s, g1, sh2, sc2, g2, w, l, bb=bs, tt=dec_seq,
                       name="out_mlp_sample")

    k_p, v_p, f_p, p_p = slabs_p
    k_s, v_s, f_s, p_s = slabs_s
    return (y_p, y_s,
            _time_major(k_p), _time_major(v_p), f_p, p_p,
            k_s.reshape(depth, bs, dec_seq, N_HEADS, HEAD_DIM), v_s.reshape(depth, bs, dec_seq, N_HEADS, HEAD_DIM),
            f_s, p_s)
```

```python
import functools

import numpy as np
import jax
import jax.numpy as jnp
from jax import lax
from jax.experimental import pallas as pl
from jax.experimental.pallas import tpu as pltpu

D_MODEL = 1024
POOL_WIDTH = 256
POOL_GROUP_DIM = 64
POOL_WINDOWS = (2, 4, 8, 16)
POOL_HIST = 15
HIST_ROWS = 16
ATT_WIDTH = 768
HEAD_DIM = 64
N_HEADS = 12
HEADS_PAD = 16
D_FF = 4096
EPS = 1e-6
NEG = -1e30
LOG2E = 1.4426950408889634
BOUND_SLACK_OCTAVES = 64.0

LANES = 128
SUBLANES = 8
CUMSUM_CHUNK = 256
VMEM_LIMIT = 56 * 1024 * 1024

F32 = jnp.float32
BF16 = jnp.bfloat16


def _dot(a, b):
    return jnp.dot(a, b, preferred_element_type=F32)


def _dot_nt(a, b):
    return lax.dot_general(a, b, (((1,), (1,)), ((), ())), preferred_element_type=F32)


def _split_dot(a, b):
    hi = a.astype(BF16)
    lo = (a - hi.astype(F32)).astype(BF16)
    return _dot(hi, b) + _dot(lo, b)


def _resident(shape):
    nd = len(shape)
    return pl.BlockSpec(shape, lambda *_: (0,) * nd, pipeline_mode=pl.Buffered(1))


def _resident_layer(shape, layer):
    nd = len(shape)
    return pl.BlockSpec((None,) + tuple(shape), lambda *_: (layer,) + (0,) * nd,
                        pipeline_mode=pl.Buffered(1))


def _mod_kernel(c_ref, w_ref, b_ref, o_ref):
    c = c_ref[...]
    s = c / (1.0 + jnp.exp(-c))
    o_ref[...] = _dot(s.astype(BF16), w_ref[...].astype(BF16)) + b_ref[...]


def _modulation(c_all, w_mod, b_mod):
    depth, _, n = w_mod.shape
    rows = c_all.shape[0]
    tn = 1536
    return pl.pallas_call(
        _mod_kernel,
        grid=(depth, n // tn),
        in_specs=[
            pl.BlockSpec((rows, D_MODEL), lambda l, j: (0, 0)),
            pl.BlockSpec((None, D_MODEL, tn), lambda l, j: (l, 0, j)),
            pl.BlockSpec((None, 1, tn), lambda l, j: (l, 0, j)),
        ],
        out_specs=pl.BlockSpec((None, rows, tn), lambda l, j: (l, 0, j)),
        out_shape=jax.ShapeDtypeStruct((depth, rows, n), F32),
        compiler_params=pltpu.CompilerParams(
            dimension_semantics=("arbitrary", "arbitrary"), vmem_limit_bytes=VMEM_LIMIT),
        name="modulation",
    )(c_all, w_mod, b_mod.reshape(depth, 1, n))


def _head_rms(a, ind, ind_t):
    ssum = _dot((a * a).astype(BF16), ind)
    r = lax.rsqrt(ssum * (1.0 / HEAD_DIM) + EPS)
    return a * _split_dot(r, ind_t)


def _head_rms_t(a_t):
    m = a_t.shape[1]
    a3 = a_t.reshape(N_HEADS, HEAD_DIM, m)
    r = lax.rsqrt(jnp.mean(a3 * a3, axis=1, keepdims=True) + EPS)
    return (a3 * r).reshape(ATT_WIDTH, m)


def _log_sigmoid(x):
    return jnp.minimum(x, 0.0) - jnp.log1p(jnp.exp(-jnp.abs(x)))


def _in_proj_kernel(*refs, n_alias, bb, tt, pos0, transposed):
    refs = refs[n_alias:]
    (x_ref, sh_ref, sc_ref, hist_ref, win_ref, wf_ref, wft_ref,
     bfr_ref, bfc_ref, qg_ref, kg_ref, ind_ref, indt_ref, wp_ref, ps_ref,
     k32_ref, v32_ref, logf_ref, pstate_ref, kbf_ref, q_ref) = refs[:21]
    vt_ref = refs[21] if transposed else None
    o = POOL_WIDTH
    wu_ref = win_ref.at[0:o, :]
    wq_ref = win_ref.at[o:o + ATT_WIDTH, :]
    wk_ref = win_ref.at[o + ATT_WIDTH:o + 2 * ATT_WIDTH, :]
    wv_ref = win_ref.at[o + 2 * ATT_WIDTH:o + 3 * ATT_WIDTH, :]
    logft_ref, py_ref, z_ref = refs[-3:]
    ti = pl.program_id(1)
    m = bb * tt

    x = x_ref[...]
    ms = jnp.mean(x * x, axis=-1, keepdims=True)
    h = (x * lax.rsqrt(ms + EPS)) * (1.0 + sc_ref[...]) + sh_ref[...]
    hb = h.reshape(m, D_MODEL).astype(BF16)

    q_raw = _dot_nt(hb, wq_ref[...])
    k_raw = _dot_nt(hb, wk_ref[...])
    v = _dot_nt(hb, wv_ref[...])
    if transposed:
        def along_lanes(g):
            return jnp.concatenate([g] * (m // LANES), axis=1)
        k_t = _head_rms_t(k_raw.T) * along_lanes(kg_ref[...])
        q_t = _head_rms_t(q_raw.T) * along_lanes(qg_ref[...])
        v_t = v.T
        k32_ref[...] = k_t
        v32_ref[...] = v_t
        kbf_ref[...] = k_t.T.reshape(bb, tt, ATT_WIDTH).astype(BF16)
        q_ref[...] = (q_t * (HEAD_DIM ** -0.5 * LOG2E)).astype(BF16)
        vt_ref[...] = v_t.astype(BF16)
    else:
        ind = ind_ref[...]
        ind_t = indt_ref[...]
        q = _head_rms(q_raw, ind, ind_t) * qg_ref[...]
        k = _head_rms(k_raw, ind, ind_t) * kg_ref[...]
        k32_ref[...] = k.reshape(bb, tt, ATT_WIDTH)
        v32_ref[...] = v.reshape(bb, tt, ATT_WIDTH)
        kbf_ref[...] = k.reshape(bb, tt, ATT_WIDTH).astype(BF16)
        q_ref[...] = (q * HEAD_DIM ** -0.5).reshape(bb, tt, ATT_WIDTH).astype(BF16)

    fl = _dot(hb, wf_ref[...]) + bfr_ref[...]
    logf_ref[...] = _log_sigmoid(fl)[:, :N_HEADS].reshape(bb, tt, N_HEADS)
    flt = _dot_nt(wft_ref[...], hb) + bfc_ref[...]
    logft_ref[...] = _log_sigmoid(flt)

    u3 = _dot_nt(hb, wu_ref[...]).reshape(bb, tt, POOL_WIDTH)

    @pl.when(ti == 0)
    def _():
        z_ref[:, 0:HIST_ROWS, :] = hist_ref[...]

    z_ref[:, HIST_ROWS:HIST_ROWS + tt, :] = u3
    acc = u3
    sums = {}
    for i in range(1, max(POOL_WINDOWS)):
        acc = acc + z_ref[:, HIST_ROWS - i:HIST_ROWS - i + tt, :]
        if i + 1 in POOL_WINDOWS:
            sums[i + 1] = acc
    group = lax.broadcasted_iota(jnp.int32, (1, 1, POOL_WIDTH), 2) // POOL_GROUP_DIM
    win = sums[POOL_WINDOWS[-1]]
    width = jnp.full((1, 1, POOL_WIDTH), float(POOL_WINDOWS[-1]), F32)
    for g in range(len(POOL_WINDOWS) - 2, -1, -1):
        win = jnp.where(group == g, sums[POOL_WINDOWS[g]], win)
        width = jnp.where(group == g, float(POOL_WINDOWS[g]), width)
    pos = (pos0 + ti * tt + lax.broadcasted_iota(jnp.int32, (1, tt, 1), 1)).astype(F32)
    count = jnp.minimum(pos + 1.0, width)
    mixed = win / count - u3
    y = _dot(mixed.reshape(m, POOL_WIDTH).astype(BF16), wp_ref[...]) * ps_ref[...]
    py_ref[...] = y.reshape(bb, tt, POOL_WIDTH).astype(BF16)

    pstate_ref[...] = z_ref[:, tt + 1:tt + HIST_ROWS, :]
    z_ref[:, 0:HIST_ROWS, :] = z_ref[:, tt:tt + HIST_ROWS, :]


def _in_proj(x, sh, sc, hist, w, slabs, layer, *, bb, tt, pos0, transposed):
    b, t, _ = x.shape
    depth = w["depth"]
    nb, nt = b // bb, t // tt
    m = bb * tt
    n_alias = 0 if slabs is None else 4
    gain_shape = (ATT_WIDTH, LANES) if transposed else (1, ATT_WIDTH)

    def per_b(shape):
        return pl.BlockSpec(shape, lambda i, j: (i,) + (0,) * (len(shape) - 1))

    in_specs = [pl.BlockSpec(memory_space=pl.ANY)] * n_alias + [
        pl.BlockSpec((bb, tt, D_MODEL), lambda i, j: (i, j, 0)),
        per_b((bb, 1, D_MODEL)), per_b((bb, 1, D_MODEL)), per_b((bb, HIST_ROWS, POOL_WIDTH)),
        _resident_layer((w["w_in"].shape[1], D_MODEL), layer),
        _resident((D_MODEL, LANES)), _resident((HEADS_PAD, D_MODEL)),
        _resident((1, LANES)), _resident((HEADS_PAD, 1)),
        _resident(gain_shape), _resident(gain_shape),
        _resident((ATT_WIDTH, LANES)), _resident((LANES, ATT_WIDTH)),
        _resident((POOL_WIDTH, POOL_WIDTH)), _resident((1, POOL_WIDTH)),
    ]
    if transposed:
        assert bb == 1
        kv_shape = jax.ShapeDtypeStruct((depth, b, ATT_WIDTH, t), F32)
        kv_spec = pl.BlockSpec((None, None, ATT_WIDTH, tt), lambda i, j: (layer, i, 0, j))
    else:
        kv_shape = jax.ShapeDtypeStruct((depth, b, t, ATT_WIDTH), F32)
        kv_spec = pl.BlockSpec((None, bb, tt, ATT_WIDTH), lambda i, j: (layer, i, j, 0))
    slab_shapes = [
        kv_shape,
        kv_shape,
        jax.ShapeDtypeStruct((depth, b, t, N_HEADS), F32),
        jax.ShapeDtypeStruct((depth, b, POOL_HIST, POOL_WIDTH), F32),
    ]
    slab_specs = [
        kv_spec,
        kv_spec,
        pl.BlockSpec((None, bb, tt, N_HEADS), lambda i, j: (layer, i, j, 0)),
        pl.BlockSpec((None, bb, POOL_HIST, POOL_WIDTH), lambda i, j: (layer, i, 0, 0)),
    ]
    if transposed:
        q_shape = jax.ShapeDtypeStruct((b, ATT_WIDTH, t), BF16)
        q_spec = pl.BlockSpec((None, ATT_WIDTH, tt), lambda i, j: (i, 0, j))
    else:
        q_shape = jax.ShapeDtypeStruct((b, t, ATT_WIDTH), BF16)
        q_spec = pl.BlockSpec((bb, tt, ATT_WIDTH), lambda i, j: (i, j, 0))
    out_shape = slab_shapes + [
        jax.ShapeDtypeStruct((b, t, ATT_WIDTH), BF16),
        q_shape,
    ] + ([q_shape] if transposed else []) + [
        jax.ShapeDtypeStruct((nb, HEADS_PAD, nt * m), F32),
        jax.ShapeDtypeStruct((b, t, POOL_WIDTH), BF16),
    ]
    out_specs = slab_specs + [
        pl.BlockSpec((bb, tt, ATT_WIDTH), lambda i, j: (i, j, 0)),
        q_spec,
    ] + ([q_spec] if transposed else []) + [
        pl.BlockSpec((None, HEADS_PAD, m), lambda i, j: (i, 0, j)),
        pl.BlockSpec((bb, tt, POOL_WIDTH), lambda i, j: (i, j, 0)),
    ]
    args = ([] if slabs is None else list(slabs)) + [
        x, sh, sc, hist, w["w_in"], w["w_f"], w["w_ft"],
        w["bf_row"], w["bf_col"], w["qg_col" if transposed else "qg"],
        w["kg_col" if transposed else "kg"], w["ind"], w["ind_t"], w["w_pool"],
        w["pool_scale"],
    ]
    outs = pl.pallas_call(
        functools.partial(_in_proj_kernel, n_alias=n_alias, bb=bb, tt=tt, pos0=pos0,
                          transposed=transposed),
        grid=(nb, nt),
        in_specs=in_specs,
        out_specs=out_specs,
        out_shape=out_shape,
        scratch_shapes=[pltpu.VMEM((bb, tt + HIST_ROWS, POOL_WIDTH), F32)],
        input_output_aliases={i: i for i in range(n_alias)},
        compiler_params=pltpu.CompilerParams(
            dimension_semantics=("arbitrary", "arbitrary"), vmem_limit_bytes=VMEM_LIMIT),
        name="in_proj_prompt" if transposed else "in_proj_sample",
    )(*args)
    return outs[:4], outs[4:]


def _cumsum_kernel(x_ref, o_ref):
    rb, length = x_ref.shape
    c = CUMSUM_CHUNK
    tri = (lax.broadcasted_iota(jnp.int32, (c, c), 0)
           <= lax.broadcasted_iota(jnp.int32, (c, c), 1)).astype(F32)
    total = jnp.zeros((rb, 1), F32)
    for i in range(length // c):
        y = total + jnp.dot(x_ref[:, i * c:(i + 1) * c], tri, precision=lax.Precision.HIGHEST,
                            preferred_element_type=F32)
        o_ref[:, i * c:(i + 1) * c] = y
        total = y[:, c - 1:c]


def _cumsum_time(x, rows_per_block):
    length = x.shape[-1]
    r = int(np.prod(x.shape[:-1]))
    rb = rows_per_block
    out = pl.pallas_call(
        _cumsum_kernel,
        grid=(r // rb,),
        in_specs=[pl.BlockSpec((rb, length), lambda i: (i, 0))],
        out_specs=pl.BlockSpec((rb, length), lambda i: (i, 0)),
        out_shape=jax.ShapeDtypeStruct((r, length), F32),
        compiler_params=pltpu.CompilerParams(
            dimension_semantics=("arbitrary",), vmem_limit_bytes=VMEM_LIMIT),
        name="cumsum_time",
    )(x.reshape(r, length))
    return out.reshape(x.shape)


def _fill_key_forget_sums(fcol_ref, fkb_ref, hp):
    chunk = 512
    for pair in range(N_HEADS // 2):
        @pl.when(hp == pair)
        def _(pair=pair):
            def fill(c, carry):
                r0 = pl.multiple_of(c * chunk, chunk)
                for hh in range(2):
                    h = 2 * pair + hh
                    col = fcol_ref[pl.ds(r0, chunk), h:h + 1]
                    fkb_ref[hh, pl.ds(r0, chunk), :] = jnp.broadcast_to(col * LOG2E, (chunk, LANES))
                return carry
            lax.fori_loop(0, fcol_ref.shape[0] // chunk, fill, 0)


def _pair_query_columns(qt):
    row_head = lax.broadcasted_iota(jnp.int32, (2 * HEAD_DIM, 1), 0) // HEAD_DIM
    zero = jnp.zeros_like(qt)
    return jnp.concatenate(
        [jnp.where(row_head == 0, qt, zero), jnp.where(row_head == 1, qt, zero)], axis=1)


def _attn_prompt_bounded_kernel(kmax_ref, qt_ref, k_ref, vt_ref, ft_ref, fcol_ref, o_ref,
                                fkb_ref, p_ref, *, tq, tk, nsub):
    hp = pl.program_id(1)
    g = pl.program_id(2)

    @pl.when(g == 0)
    def _():
        _fill_key_forget_sums(fcol_ref, fkb_ref, hp)

    ones_rows = jnp.ones((HEADS_PAD, tk), BF16)
    key_pos = lax.broadcasted_iota(jnp.int32, (tk, tq), 0)
    query_pos = lax.broadcasted_iota(jnp.int32, (tk, tq), 1)

    def query_block(sub):
        qi = g * nsub + sub
        n_full = (qi * tq) // tk
        qt = qt_ref[:, sub * tq:(sub + 1) * tq]
        qsq = jnp.square(qt.astype(F32))
        shift = []
        for hh in range(2):
            qnorm = jnp.sqrt(
                jnp.sum(qsq[hh * HEAD_DIM:(hh + 1) * HEAD_DIM], axis=0, keepdims=True))
            fq = ft_ref[pl.ds(2 * hp + hh, 1), sub * tq:(sub + 1) * tq]
            shift.append(fq * LOG2E - qnorm * kmax_ref[0, 0])
        causal = n_full * tk + key_pos <= qi * tq + query_pos
        return n_full, _pair_query_columns(qt), shift, causal

    def weights(blk, j, masked):
        _, q2, shift, causal = blk
        k0 = pl.multiple_of(j * tk, tk)
        s2 = _dot(k_ref[pl.ds(k0, tk), :], q2)
        ps = []
        for hh in range(2):
            fk = fkb_ref[hh, pl.ds(k0, tk), :]
            a = (s2[:, hh * tq:(hh + 1) * tq] - jnp.concatenate([fk] * (tq // LANES), axis=1)
                 + shift[hh])
            if masked:
                a = jnp.where(causal, a, NEG)
            ps.append(jnp.exp2(a).astype(BF16))
        return jnp.concatenate(ps, axis=1)

    def accumulate(j, acc, p2):
        k0 = pl.multiple_of(j * tk, tk)
        out = []
        for hh in range(2):
            vb = jnp.concatenate(
                [vt_ref[pl.ds(hh * HEAD_DIM, HEAD_DIM), pl.ds(k0, tk)], ones_rows], axis=0)
            out.append(acc[hh] + _dot(vb, p2[:, hh * tq:(hh + 1) * tq]))
        return tuple(out)

    blocks = [query_block(sub) for sub in range(nsub)]
    acc0 = jnp.zeros((HEAD_DIM + HEADS_PAD, tq), F32)
    p_ref[...] = weights(blocks[0], blocks[0][0], True)
    for sub, blk in enumerate(blocks):
        n_full = blk[0]

        def step(j, acc, blk=blk, n_full=n_full):
            p_next = weights(blk, j, False)
            acc = accumulate(jnp.where(j == 0, n_full, j - 1), acc, p_ref[...])
            p_ref[...] = p_next
            return acc

        acc = lax.fori_loop(0, n_full, step, (acc0, acc0))
        last = jnp.maximum(n_full - 1, 0)
        if sub + 1 < nsub:
            nxt = blocks[sub + 1]
            p_next = weights(nxt, nxt[0], True)
            acc = accumulate(last, acc, p_ref[...])
            p_ref[...] = p_next
        else:
            acc = accumulate(last, acc, p_ref[...])
        outs = [a[0:HEAD_DIM] / a[HEAD_DIM:HEAD_DIM + 1] for a in acc]
        o_ref[sub * tq:(sub + 1) * tq, :] = jnp.concatenate(outs, axis=0).T.astype(BF16)


def _attn_prompt_kernel(qt_ref, k_ref, vt_ref, ft_ref, fcol_ref, o_ref, fkb_ref, s_ref, p_ref,
                        *, tq, tk):
    hp = pl.program_id(1)
    qi = pl.program_id(2)
    n_full = (qi * tq) // tk

    @pl.when(qi == 0)
    def _():
        _fill_key_forget_sums(fcol_ref, fkb_ref, hp)

    causal = (n_full * tk + lax.broadcasted_iota(jnp.int32, (tk, tq), 0)
              <= qi * tq + lax.broadcasted_iota(jnp.int32, (tk, tq), 1))
    ones_rows = jnp.ones((HEADS_PAD, tk), BF16)
    q2 = _pair_query_columns(qt_ref[...])
    fq = [ft_ref[pl.ds(2 * hp + hh, 1), :] * LOG2E for hh in range(2)]

    def biased_scores(j):
        k0 = pl.multiple_of(j * tk, tk)
        s2 = _dot(k_ref[pl.ds(k0, tk), :], q2)
        a, cmax = [], []
        for hh in range(2):
            fk = fkb_ref[hh, pl.ds(k0, tk), :]
            ah = s2[:, hh * tq:(hh + 1) * tq] - jnp.concatenate([fk] * (tq // LANES), axis=1)
            a.append(ah)
            cmax.append(jnp.max(ah, axis=0, keepdims=True))
        return jnp.concatenate(a, axis=1), cmax

    def weights(a2, cmax, m_run):
        ps, alphas, ms = [], [], []
        for hh in range(2):
            m_new = jnp.maximum(m_run[hh], cmax[hh] + fq[hh])
            alphas.append(jnp.exp2(m_run[hh] - m_new))
            ps.append(jnp.exp2(a2[:, hh * tq:(hh + 1) * tq] + (fq[hh] - m_new)).astype(BF16))
            ms.append(m_new)
        return ps, alphas, ms

    def accumulate(j, acc, p, alpha):
        k0 = pl.multiple_of(j * tk, tk)
        out = []
        for hh in range(2):
            vb = jnp.concatenate(
                [vt_ref[pl.ds(hh * HEAD_DIM, HEAD_DIM), pl.ds(k0, tk)], ones_rows], axis=0)
            out.append(alpha[hh] * acc[hh] + _dot(vb, p[hh]))
        return out

    a0, cmax0 = biased_scores(0)
    s_ref[...] = a0
    p_ref[...] = jnp.zeros(p_ref.shape, BF16)

    def step(j, carry):
        acc, m_run, alpha, cmax = carry
        a_cur = s_ref[...]
        p_prev = p_ref[...]
        a_next, cmax_next = biased_scores(j + 1)
        p, alpha_new, m_new = weights(a_cur, cmax, m_run)
        acc = accumulate(jnp.maximum(j - 1, 0), acc, [p_prev[:, :tq], p_prev[:, tq:]], alpha)
        s_ref[...] = a_next
        p_ref[...] = jnp.concatenate(p, axis=1)
        return tuple(acc), tuple(m_new), tuple(alpha_new), tuple(cmax_next)

    acc0 = jnp.zeros((HEAD_DIM + HEADS_PAD, tq), F32)
    m0 = jnp.full((1, tq), NEG, F32)
    one = jnp.ones((1, tq), F32)
    acc, m_run, alpha, _ = lax.fori_loop(
        0, n_full, step, ((acc0, acc0), (m0, m0), (one, one), tuple(cmax0)))
    p_prev = p_ref[...]
    a_diag = s_ref[...]
    a_diag = jnp.concatenate(
        [jnp.where(causal, a_diag[:, hh * tq:(hh + 1) * tq], NEG) for hh in range(2)], axis=1)
    cmax = [jnp.max(a_diag[:, hh * tq:(hh + 1) * tq], axis=0, keepdims=True) for hh in range(2)]
    p, alpha_new, _ = weights(a_diag, cmax, m_run)
    acc = accumulate(jnp.maximum(n_full - 1, 0), acc, [p_prev[:, :tq], p_prev[:, tq:]], alpha)
    acc = accumulate(n_full, acc, p, alpha_new)
    outs = [a[0:HEAD_DIM] / a[HEAD_DIM:HEAD_DIM + 1] for a in acc]
    o_ref[...] = jnp.concatenate(outs, axis=0).T.astype(BF16)


def _attn_prompt(qt, kbf, vt, ft, fcol, q_gain, k_gain, *, tq, tk):
    b, t, _ = kbf.shape
    pairs = N_HEADS // 2
    in_specs = [
        pl.BlockSpec((None, 2 * HEAD_DIM, tq), lambda i, p, j: (i, p, j)),
        pl.BlockSpec((None, t, 2 * HEAD_DIM), lambda i, p, j: (i, 0, p)),
        pl.BlockSpec((None, 2 * HEAD_DIM, t), lambda i, p, j: (i, p, 0)),
        pl.BlockSpec((None, HEADS_PAD, tq), lambda i, p, j: (i, 0, j)),
        pl.BlockSpec((None, t, HEADS_PAD), lambda i, p, j: (i, 0, 0)),
    ]
    rest = dict(
        out_shape=jax.ShapeDtypeStruct((b, t, ATT_WIDTH), BF16),
        compiler_params=pltpu.CompilerParams(
            dimension_semantics=("arbitrary", "arbitrary", "arbitrary"),
            vmem_limit_bytes=VMEM_LIMIT),
    )
    common = dict(
        grid=(b, pairs, t // tq),
        out_specs=pl.BlockSpec((None, tq, 2 * HEAD_DIM), lambda i, p, j: (i, j, p)),
        **rest,
    )
    key_sums = pltpu.VMEM((2, t, LANES), F32)
    weights_tile = pltpu.VMEM((tk, 2 * tq), BF16)

    def running_max(_, *args):
        return pl.pallas_call(
            functools.partial(_attn_prompt_kernel, tq=tq, tk=tk),
            in_specs=in_specs,
            scratch_shapes=[key_sums, pltpu.VMEM((tk, 2 * tq), F32), weights_tile],
            name="attn_prompt", **common)(*args)

    def bounded(kmax, *args):
        nsub = 2
        wide = nsub * tq
        return pl.pallas_call(
            functools.partial(_attn_prompt_bounded_kernel, tq=tq, tk=tk, nsub=nsub),
            grid=(b, pairs, t // wide),
            in_specs=[
                pl.BlockSpec(memory_space=pltpu.SMEM),
                pl.BlockSpec((None, 2 * HEAD_DIM, wide), lambda i, p, j: (i, p, j)),
                in_specs[1], in_specs[2],
                pl.BlockSpec((None, HEADS_PAD, wide), lambda i, p, j: (i, 0, j)),
                in_specs[4],
            ],
            out_specs=pl.BlockSpec((None, wide, 2 * HEAD_DIM), lambda i, p, j: (i, j, p)),
            scratch_shapes=[key_sums, weights_tile],
            name="attn_prompt_bounded", **rest)(kmax, *args)

    qmax = 1.01 * LOG2E * jnp.max(jnp.abs(q_gain))
    kmax = 1.01 * HEAD_DIM ** 0.5 * jnp.max(jnp.abs(k_gain))
    return lax.cond(2.0 * qmax * kmax <= BOUND_SLACK_OCTAVES, bounded, running_max,
                    kmax.reshape(1, 1).astype(F32), qt, kbf, vt, ft, fcol)


def _attn_sample_kernel(q_ref, ck_ref, cv_ref, kn_ref, vn_ref, f_ref, lrow_ref, lcol_ref, o_ref,
                        qbd_ref, m_ref, l_ref, acc_ref, fq_ref, fnew_ref, *, tk, past, tt):
    i = pl.program_id(0)
    j = pl.program_id(1)
    rows = N_HEADS * tt
    own = ((lax.broadcasted_iota(jnp.int32, (rows, ATT_WIDTH), 0) // tt)
           == (lax.broadcasted_iota(jnp.int32, (rows, ATT_WIDTH), 1) // HEAD_DIM))

    def cached_forget_sums(start, size):
        blk = f_ref[:, :, pl.ds(start, size)]
        mine = lax.broadcasted_iota(jnp.int32, (1, SUBLANES, 1), 1) == i % SUBLANES
        return jnp.sum(jnp.where(mine, blk, 0.0), axis=1)

    @pl.when(j == 0)
    def _():
        q = q_ref[...]
        qbd_ref[...] = jnp.where(own, jnp.concatenate([q] * N_HEADS, axis=0), jnp.zeros((), BF16))
        m_ref[...] = jnp.full(m_ref.shape, NEG, F32)
        l_ref[...] = jnp.zeros(l_ref.shape, F32)
        acc_ref[...] = jnp.zeros(acc_ref.shape, F32)
        f_last = cached_forget_sums(past - 1, 1)
        upper = (lax.broadcasted_iota(jnp.int32, (tt, tt), 0)
                 <= lax.broadcasted_iota(jnp.int32, (tt, tt), 1)).astype(F32)
        run = jnp.dot(lrow_ref[...], upper, precision=lax.Precision.HIGHEST,
                      preferred_element_type=F32)
        fnew_ref[...] = f_last + run[:N_HEADS]
        r_i = lax.broadcasted_iota(jnp.int32, (rows, rows), 0)
        c_i = lax.broadcasted_iota(jnp.int32, (rows, rows), 1)
        lower = ((r_i // tt == c_i // tt) & (c_i <= r_i)).astype(F32)
        run_col = jnp.dot(lower, jnp.broadcast_to(lcol_ref[...], (rows, LANES)),
                          precision=lax.Precision.HIGHEST, preferred_element_type=F32)
        f_last_rows = jnp.broadcast_to(f_last[:, None, :], (N_HEADS, tt, 1)).reshape(rows, 1)
        fq_ref[...] = f_last_rows + run_col[:, 0:1]

    fq3 = fq_ref[...].reshape(N_HEADS, tt, 1)

    def update(kb, vb, fk, mask, time_on_lanes):
        if time_on_lanes:
            n = kb.shape[1]
            s = _dot(qbd_ref[...], kb.astype(BF16))
        else:
            n = kb.shape[0]
            s = _dot_nt(qbd_ref[...], kb.astype(BF16))
        s = s.reshape(N_HEADS, tt, n) + fq3 - fk[:, None, :]
        if mask is not None:
            s = jnp.where(mask, s, NEG)
        m_old = m_ref[...].reshape(N_HEADS, tt, 1)
        m_new = jnp.maximum(m_old, jnp.max(s, axis=-1, keepdims=True))
        alpha = jnp.exp(m_old - m_new)
        p = jnp.exp(s - m_new)
        l_ref[...] = (alpha * l_ref[...].reshape(N_HEADS, tt, 1)
                      + jnp.sum(p, axis=-1, keepdims=True)).reshape(rows, 1)
        m_ref[...] = m_new.reshape(rows, 1)
        pb = p.reshape(rows, n).astype(BF16)
        pv = _dot_nt(pb, vb.astype(BF16)) if time_on_lanes else _dot(pb, vb.astype(BF16))
        acc_ref[...] = alpha.reshape(rows, 1) * acc_ref[...] + pv

    k0 = pl.multiple_of(j * tk, tk)
    update(ck_ref[...], cv_ref[...], cached_forget_sums(k0, tk), None, True)

    @pl.when(j == pl.num_programs(1) - 1)
    def _():
        mask = (lax.broadcasted_iota(jnp.int32, (1, tt, tt), 2)
                <= lax.broadcasted_iota(jnp.int32, (1, tt, tt), 1))
        update(kn_ref[...], vn_ref[...], fnew_ref[...], mask, False)
        o = jnp.where(own, acc_ref[...] / l_ref[...], 0.0).reshape(N_HEADS, tt, ATT_WIDTH)
        o_ref[...] = jnp.sum(o, axis=0).astype(BF16)


def _attn_sample(q, cache_k, cache_v, k_new, v_new, f_past, logf_row, logf_col, layer, *, tk):
    nb, tt, _ = q.shape
    past = cache_k.shape[3]
    rows = N_HEADS * tt
    return pl.pallas_call(
        functools.partial(_attn_sample_kernel, tk=tk, past=past, tt=tt),
        grid=(nb, past // tk),
        in_specs=[
            pl.BlockSpec((None, tt, ATT_WIDTH), lambda i, j: (i, 0, 0)),
            pl.BlockSpec((None, None, ATT_WIDTH, tk), lambda i, j: (layer, i, 0, j)),
            pl.BlockSpec((None, None, ATT_WIDTH, tk), lambda i, j: (layer, i, 0, j)),
            pl.BlockSpec((None, None, tt, ATT_WIDTH), lambda i, j: (layer, i, 0, 0)),
            pl.BlockSpec((None, None, tt, ATT_WIDTH), lambda i, j: (layer, i, 0, 0)),
            pl.BlockSpec((None, N_HEADS, SUBLANES, past), lambda i, j: (layer, 0, i // SUBLANES, 0)),
            pl.BlockSpec((None, HEADS_PAD, tt), lambda i, j: (i, 0, 0)),
            pl.BlockSpec((None, rows, 1), lambda i, j: (i, 0, 0)),
        ],
        out_specs=pl.BlockSpec((None, tt, ATT_WIDTH), lambda i, j: (i, 0, 0)),
        out_shape=jax.ShapeDtypeStruct((nb, tt, ATT_WIDTH), BF16),
        scratch_shapes=[
            pltpu.VMEM((rows, ATT_WIDTH), BF16),
            pltpu.VMEM((rows, 1), F32),
            pltpu.VMEM((rows, 1), F32),
            pltpu.VMEM((rows, ATT_WIDTH), F32),
            pltpu.VMEM((rows, 1), F32),
            pltpu.VMEM((N_HEADS, tt), F32),
        ],
        compiler_params=pltpu.CompilerParams(
            dimension_semantics=("arbitrary", "arbitrary"), vmem_limit_bytes=VMEM_LIMIT),
        name="attn_sample",
    )(q, cache_k, cache_v, k_new, v_new, f_past, logf_row, logf_col)


def _out_mlp_kernel(x_ref, py_ref, att_ref, g1_ref, sh_ref, sc_ref, g2_ref,
                    wo_ref, wup_ref, wdn_ref, y_ref, *, bb, tt, ff_chunk):
    m = bb * tt
    mix = (_dot(py_ref[...].reshape(m, POOL_WIDTH), wo_ref[0:POOL_WIDTH, :])
           + _dot(att_ref[...].reshape(m, ATT_WIDTH), wo_ref[POOL_WIDTH:, :]))
    x1 = x_ref[...] + g1_ref[...] * mix.reshape(bb, tt, D_MODEL)
    ms = jnp.mean(x1 * x1, axis=-1, keepdims=True)
    h = (x1 * lax.rsqrt(ms + EPS)) * (1.0 + sc_ref[...]) + sh_ref[...]
    hb = h.reshape(m, D_MODEL).astype(BF16)
    ff = jnp.zeros((m, D_MODEL), F32)
    for c in range(D_FF // ff_chunk):
        up = _dot(hb, wup_ref[:, c * ff_chunk:(c + 1) * ff_chunk])
        act = jnp.square(jnp.maximum(up, 0.0)).astype(BF16)
        ff = ff + _dot(act, wdn_ref[c * ff_chunk:(c + 1) * ff_chunk, :])
    y_ref[...] = x1 + g2_ref[...] * ff.reshape(bb, tt, D_MODEL)


def _out_mlp(x, py, att, g1, sh, sc, g2, w, layer, *, bb, tt, name):
    b, t, _ = x.shape

    def tile(width):
        return pl.BlockSpec((bb, tt, width), lambda i, j: (i, j, 0))

    def per_b():
        return pl.BlockSpec((bb, 1, D_MODEL), lambda i, j: (i, 0, 0))

    return pl.pallas_call(
        functools.partial(_out_mlp_kernel, bb=bb, tt=tt, ff_chunk=1024),
        grid=(b // bb, t // tt),
        in_specs=[
            tile(D_MODEL), tile(POOL_WIDTH), tile(ATT_WIDTH),
            per_b(), per_b(), per_b(), per_b(),
            _resident_layer((POOL_WIDTH + ATT_WIDTH, D_MODEL), layer),
            _resident_layer((D_MODEL, D_FF), layer), _resident_layer((D_FF, D_MODEL), layer),
        ],
        out_specs=tile(D_MODEL),
        out_shape=jax.ShapeDtypeStruct((b, t, D_MODEL), F32),
        compiler_params=pltpu.CompilerParams(
            dimension_semantics=("arbitrary", "arbitrary"), vmem_limit_bytes=VMEM_LIMIT),
        name=name,
    )(x, py, att, g1, sh, sc, g2, w["w_out"], w["w_up"], w["w_down"])


def _layer_weights(depth, layer, stacked, w_in, b_f, q_gain, k_gain, w_pool, pool_scale):
    w_f = w_in[layer, :, POOL_WIDTH + 3 * ATT_WIDTH:]
    head_of_col = np.arange(ATT_WIDTH) // HEAD_DIM
    ind = (head_of_col[:, None] == np.arange(LANES)[None, :]).astype(np.float32)
    w_pool_bd = jnp.zeros((POOL_WIDTH, POOL_WIDTH), F32)
    for g in range(len(POOL_WINDOWS)):
        sl = slice(g * POOL_GROUP_DIM, (g + 1) * POOL_GROUP_DIM)
        w_pool_bd = w_pool_bd.at[sl, sl].set(w_pool[layer, g])
    return {
        "depth": depth,
        **stacked,
        "w_f": jnp.pad(w_f, ((0, 0), (0, LANES - N_HEADS))).astype(BF16),
        "w_ft": jnp.pad(w_f.T, ((0, HEADS_PAD - N_HEADS), (0, 0))).astype(BF16),
        "bf_row": jnp.pad(b_f[layer], (0, LANES - N_HEADS)).reshape(1, LANES),
        "bf_col": jnp.pad(b_f[layer], (0, HEADS_PAD - N_HEADS)).reshape(HEADS_PAD, 1),
        "qg": jnp.tile(q_gain[layer], N_HEADS).reshape(1, ATT_WIDTH),
        "kg": jnp.tile(k_gain[layer], N_HEADS).reshape(1, ATT_WIDTH),
        "qg_col": jnp.broadcast_to(jnp.tile(q_gain[layer], N_HEADS)[:, None], (ATT_WIDTH, LANES)),
        "kg_col": jnp.broadcast_to(jnp.tile(k_gain[layer], N_HEADS)[:, None], (ATT_WIDTH, LANES)),
        "ind": jnp.asarray(ind, BF16),
        "ind_t": jnp.asarray(ind.T, BF16),
        "w_pool": w_pool_bd.astype(BF16),
        "pool_scale": pool_scale[layer].reshape(1, POOL_WIDTH),
    }


def _mod_parts(mod_rows):
    b = mod_rows.shape[0]
    parts = mod_rows.reshape(b, 6, 1, D_MODEL)
    return [parts[:, i] for i in range(6)]


def _time_major(slab):
    depth, b, _, t = slab.shape
    return jnp.transpose(slab.reshape(depth, b, N_HEADS, HEAD_DIM, t), (0, 1, 4, 2, 3))


def kernel(x_prompt, x_sample, c_prompt, c_sample, cache_k, cache_v, cache_logf, cache_pool,
           w_mod, b_mod, w_in, b_f, q_gain, k_gain, w_pool, pool_scale, w_out, w_up, w_down):
    depth = w_mod.shape[0]
    bp, seq, _ = x_prompt.shape
    bs, dec_seq, _ = x_sample.shape
    past = cache_k.shape[2]
    tq = 1024
    tk_p = 1024
    tm_in = 512
    tm_out = 512
    tk_cache = 2048

    n_c = bp + bs
    c_rows = -(-n_c // 8) * 8
    c_all = jnp.pad(jnp.concatenate([c_prompt, c_sample], axis=0), ((0, c_rows - n_c), (0, 0)))
    mod = _modulation(c_all, w_mod, b_mod)

    ck = jnp.transpose(cache_k, (0, 1, 3, 4, 2)).reshape(depth, bs, ATT_WIDTH, past)
    cv = jnp.transpose(cache_v, (0, 1, 3, 4, 2)).reshape(depth, bs, ATT_WIDTH, past)
    f_past = _cumsum_time(jnp.transpose(cache_logf, (0, 3, 1, 2)), rows_per_block=8 * bs)
    stacked = {"w_in": jnp.swapaxes(w_in, 1, 2).astype(BF16), "w_out": w_out.astype(BF16),
               "w_up": w_up.astype(BF16), "w_down": w_down.astype(BF16)}
    hist_p = jnp.zeros((bp, HIST_ROWS, POOL_WIDTH), F32)

    y_p, y_s = x_prompt, x_sample
    slabs_p = slabs_s = None
    for l in range(depth):
        w = _layer_weights(depth, l, stacked, w_in, b_f, q_gain, k_gain, w_pool, pool_scale)
        sh1, sc1, g1, sh2, sc2, g2 = _mod_parts(mod[l, :bp])
        slabs_p, (kbf, qt, vt, logft, py) = _in_proj(
            y_p, sh1, sc1, hist_p, w, slabs_p, l, bb=1, tt=tm_in, pos0=0, transposed=True)
        ft = _cumsum_time(logft, rows_per_block=bp * HEADS_PAD)
        att = _attn_prompt(qt, kbf, vt, ft, jnp.swapaxes(ft, 1, 2), q_gain[l], k_gain[l],
                           tq=tq, tk=tk_p)
        y_p = _out_mlp(y_p, py, att, g1, sh2, sc2, g2, w, l, bb=1, tt=tm_out, name="out_mlp_prompt")

        sh1, sc1, g1, sh2, sc2, g2 = _mod_parts(mod[l, bp:n_c])
        hist_s = jnp.pad(cache_pool[l], ((0, 0), (HIST_ROWS - POOL_HIST, 0), (0, 0)))
        slabs_s, (_, q_s, logft_s, py_s) = _in_proj(
            y_s, sh1, sc1, hist_s, w, slabs_s, l, bb=bs, tt=dec_seq, pos0=past, transposed=False)
        logf_row = jnp.swapaxes(logft_s.reshape(HEADS_PAD, bs, dec_seq), 0, 1)
        logf_col = logf_row[:, :N_HEADS].reshape(bs, N_HEADS * dec_seq, 1)
        att_s = _attn_sample(q_s, ck, cv, slabs_s[0], slabs_s[1], f_past, logf_row, logf_col, l,
                             tk=tk_cache)
        y_s = _out_mlp(y_s, py_s, att_s, g1, sh2, sc2, g2, w, l, bb=bs, tt=dec_seq,
                       name="out_mlp_sample")

    k_p, v_p, f_p, p_p = slabs_p
    k_s, v_s, f_s, p_s = slabs_s
    return (y_p, y_s,
            _time_major(k_p), _time_major(v_p), f_p, p_p,
            k_s.reshape(depth, bs, dec_seq, N_HEADS, HEAD_DIM), v_s.reshape(depth, bs, dec_seq, N_HEADS, HEAD_DIM),
            f_s, p_s)
```

```python
import functools

import numpy as np
import jax
import jax.numpy as jnp
from jax import lax
from jax.experimental import pallas as pl
from jax.experimental.pallas import tpu as pltpu

D_MODEL = 1024
POOL_WIDTH = 256
POOL_GROUP_DIM = 64
POOL_WINDOWS = (2, 4, 8, 16)
POOL_HIST = 15
HIST_ROWS = 16
ATT_WIDTH = 768
HEAD_DIM = 64
N_HEADS = 12
HEADS_PAD = 16
D_FF = 4096
EPS = 1e-6
NEG = -1e30
LOG2E = 1.4426950408889634
BOUND_SLACK_OCTAVES = 64.0

LANES = 128
SUBLANES = 8
CUMSUM_CHUNK = 256
VMEM_LIMIT = 56 * 1024 * 1024

F32 = jnp.float32
BF16 = jnp.bfloat16


def _dot(a, b):
    return jnp.dot(a, b, preferred_element_type=F32)


def _dot_nt(a, b):
    return lax.dot_general(a, b, (((1,), (1,)), ((), ())), preferred_element_type=F32)


def _split_dot(a, b):
    hi = a.astype(BF16)
    lo = (a - hi.astype(F32)).astype(BF16)
    return _dot(hi, b) + _dot(lo, b)


def _resident(shape):
    nd = len(shape)
    return pl.BlockSpec(shape, lambda *_: (0,) * nd, pipeline_mode=pl.Buffered(1))


def _resident_layer(shape, layer):
    nd = len(shape)
    return pl.BlockSpec((None,) + tuple(shape), lambda *_: (layer,) + (0,) * nd,
                        pipeline_mode=pl.Buffered(1))


def _mod_kernel(c_ref, w_ref, b_ref, o_ref):
    c = c_ref[...]
    s = c / (1.0 + jnp.exp(-c))
    o_ref[...] = _dot(s.astype(BF16), w_ref[...].astype(BF16)) + b_ref[...]


def _modulation(c_all, w_mod, b_mod):
    depth, _, n = w_mod.shape
    rows = c_all.shape[0]
    tn = 1536
    return pl.pallas_call(
        _mod_kernel,
        grid=(depth, n // tn),
        in_specs=[
            pl.BlockSpec((rows, D_MODEL), lambda l, j: (0, 0)),
            pl.BlockSpec((None, D_MODEL, tn), lambda l, j: (l, 0, j)),
            pl.BlockSpec((None, 1, tn), lambda l, j: (l, 0, j)),
        ],
        out_specs=pl.BlockSpec((None, rows, tn), lambda l, j: (l, 0, j)),
        out_shape=jax.ShapeDtypeStruct((depth, rows, n), F32),
        compiler_params=pltpu.CompilerParams(
            dimension_semantics=("arbitrary", "arbitrary"), vmem_limit_bytes=VMEM_LIMIT),
        name="modulation",
    )(c_all, w_mod, b_mod.reshape(depth, 1, n))


def _head_rms(a, ind, ind_t):
    ssum = _dot((a * a).astype(BF16), ind)
    r = lax.rsqrt(ssum * (1.0 / HEAD_DIM) + EPS)
    return a * _split_dot(r, ind_t)


def _head_rms_t(a_t):
    m = a_t.shape[1]
    a3 = a_t.reshape(N_HEADS, HEAD_DIM, m)
    r = lax.rsqrt(jnp.mean(a3 * a3, axis=1, keepdims=True) + EPS)
    return (a3 * r).reshape(ATT_WIDTH, m)


def _log_sigmoid(x):
    return jnp.minimum(x, 0.0) - jnp.log1p(jnp.exp(-jnp.abs(x)))


def _in_proj_kernel(*refs, n_alias, bb, tt, pos0, transposed):
    refs = refs[n_alias:]
    (x_ref, sh_ref, sc_ref, hist_ref, win_ref, wf_ref, wft_ref,
     bfr_ref, bfc_ref, qg_ref, kg_ref, ind_ref, indt_ref, wp_ref, ps_ref,
     k32_ref, v32_ref, logf_ref, pstate_ref, kbf_ref, q_ref) = refs[:21]
    vt_ref = refs[21] if transposed else None
    o = POOL_WIDTH
    wu_ref = win_ref.at[0:o, :]
    wq_ref = win_ref.at[o:o + ATT_WIDTH, :]
    wk_ref = win_ref.at[o + ATT_WIDTH:o + 2 * ATT_WIDTH, :]
    wv_ref = win_ref.at[o + 2 * ATT_WIDTH:o + 3 * ATT_WIDTH, :]
    logft_ref, py_ref, z_ref = refs[-3:]
    ti = pl.program_id(1)
    m = bb * tt

    x = x_ref[...]
    ms = jnp.mean(x * x, axis=-1, keepdims=True)
    h = (x * lax.rsqrt(ms + EPS)) * (1.0 + sc_ref[...]) + sh_ref[...]
    hb = h.reshape(m, D_MODEL).astype(BF16)

    q_raw = _dot_nt(hb, wq_ref[...])
    k_raw = _dot_nt(hb, wk_ref[...])
    v = _dot_nt(hb, wv_ref[...])
    if transposed:
        def along_lanes(g):
            return jnp.concatenate([g] * (m // LANES), axis=1)
        k_t = _head_rms_t(k_raw.T) * along_lanes(kg_ref[...])
        q_t = _head_rms_t(q_raw.T) * along_lanes(qg_ref[...])
        v_t = v.T
        k32_ref[...] = k_t
        v32_ref[...] = v_t
        kbf_ref[...] = k_t.T.reshape(bb, tt, ATT_WIDTH).astype(BF16)
        q_ref[...] = (q_t * (HEAD_DIM ** -0.5 * LOG2E)).astype(BF16)
        vt_ref[...] = v_t.astype(BF16)
    else:
        ind = ind_ref[...]
        ind_t = indt_ref[...]
        q = _head_rms(q_raw, ind, ind_t) * qg_ref[...]
        k = _head_rms(k_raw, ind, ind_t) * kg_ref[...]
        k32_ref[...] = k.reshape(bb, tt, ATT_WIDTH)
        v32_ref[...] = v.reshape(bb, tt, ATT_WIDTH)
        kbf_ref[...] = k.reshape(bb, tt, ATT_WIDTH).astype(BF16)
        q_ref[...] = (q * HEAD_DIM ** -0.5).reshape(bb, tt, ATT_WIDTH).astype(BF16)

    fl = _dot(hb, wf_ref[...]) + bfr_ref[...]
    logf_ref[...] = _log_sigmoid(fl)[:, :N_HEADS].reshape(bb, tt, N_HEADS)
    flt = _dot_nt(wft_ref[...], hb) + bfc_ref[...]
    logft_ref[...] = _log_sigmoid(flt)

    u3 = _dot_nt(hb, wu_ref[...]).reshape(bb, tt, POOL_WIDTH)

    @pl.when(ti == 0)
    def _():
        z_ref[:, 0:HIST_ROWS, :] = hist_ref[...]

    z_ref[:, HIST_ROWS:HIST_ROWS + tt, :] = u3
    acc = u3
    sums = {}
    for i in range(1, max(POOL_WINDOWS)):
        acc = acc + z_ref[:, HIST_ROWS - i:HIST_ROWS - i + tt, :]
        if i + 1 in POOL_WINDOWS:
            sums[i + 1] = acc
    group = lax.broadcasted_iota(jnp.int32, (1, 1, POOL_WIDTH), 2) // POOL_GROUP_DIM
    win = sums[POOL_WINDOWS[-1]]
    width = jnp.full((1, 1, POOL_WIDTH), float(POOL_WINDOWS[-1]), F32)
    for g in range(len(POOL_WINDOWS) - 2, -1, -1):
        win = jnp.where(group == g, sums[POOL_WINDOWS[g]], win)
        width = jnp.where(group == g, float(POOL_WINDOWS[g]), width)
    pos = (pos0 + ti * tt + lax.broadcasted_iota(jnp.int32, (1, tt, 1), 1)).astype(F32)
    count = jnp.minimum(pos + 1.0, width)
    mixed = win / count - u3
    y = _dot(mixed.reshape(m, POOL_WIDTH).astype(BF16), wp_ref[...]) * ps_ref[...]
    py_ref[...] = y.reshape(bb, tt, POOL_WIDTH).astype(BF16)

    pstate_ref[...] = z_ref[:, tt + 1:tt + HIST_ROWS, :]
    z_ref[:, 0:HIST_ROWS, :] = z_ref[:, tt:tt + HIST_ROWS, :]


def _in_proj(x, sh, sc, hist, w, slabs, layer, *, bb, tt, pos0, transposed):
    b, t, _ = x.shape
    depth = w["depth"]
    nb, nt = b // bb, t // tt
    m = bb * tt
    n_alias = 0 if slabs is None else 4
    gain_shape = (ATT_WIDTH, LANES) if transposed else (1, ATT_WIDTH)

    def per_b(shape):
        return pl.BlockSpec(shape, lambda i, j: (i,) + (0,) * (len(shape) - 1))

    in_specs = [pl.BlockSpec(memory_space=pl.ANY)] * n_alias + [
        pl.BlockSpec((bb, tt, D_MODEL), lambda i, j: (i, j, 0)),
        per_b((bb, 1, D_MODEL)), per_b((bb, 1, D_MODEL)), per_b((bb, HIST_ROWS, POOL_WIDTH)),
        _resident_layer((w["w_in"].shape[1], D_MODEL), layer),
        _resident((D_MODEL, LANES)), _resident((HEADS_PAD, D_MODEL)),
        _resident((1, LANES)), _resident((HEADS_PAD, 1)),
        _resident(gain_shape), _resident(gain_shape),
        _resident((ATT_WIDTH, LANES)), _resident((LANES, ATT_WIDTH)),
        _resident((POOL_WIDTH, POOL_WIDTH)), _resident((1, POOL_WIDTH)),
    ]
    if transposed:
        assert bb == 1
        kv_shape = jax.ShapeDtypeStruct((depth, b, ATT_WIDTH, t), F32)
        kv_spec = pl.BlockSpec((None, None, ATT_WIDTH, tt), lambda i, j: (layer, i, 0, j))
    else:
        kv_shape = jax.ShapeDtypeStruct((depth, b, t, ATT_WIDTH), F32)
        kv_spec = pl.BlockSpec((None, bb, tt, ATT_WIDTH), lambda i, j: (layer, i, j, 0))
    slab_shapes = [
        kv_shape,
        kv_shape,
        jax.ShapeDtypeStruct((depth, b, t, N_HEADS), F32),
        jax.ShapeDtypeStruct((depth, b, POOL_HIST, POOL_WIDTH), F32),
    ]
    slab_specs = [
        kv_spec,
        kv_spec,
        pl.BlockSpec((None, bb, tt, N_HEADS), lambda i, j: (layer, i, j, 0)),
        pl.BlockSpec((None, bb, POOL_HIST, POOL_WIDTH), lambda i, j: (layer, i, 0, 0)),
    ]
    if transposed:
        q_shape = jax.ShapeDtypeStruct((b, ATT_WIDTH, t), BF16)
        q_spec = pl.BlockSpec((None, ATT_WIDTH, tt), lambda i, j: (i, 0, j))
    else:
        q_shape = jax.ShapeDtypeStruct((b, t, ATT_WIDTH), BF16)
        q_spec = pl.BlockSpec((bb, tt, ATT_WIDTH), lambda i, j: (i, j, 0))
    out_shape = slab_shapes + [
        jax.ShapeDtypeStruct((b, t, ATT_WIDTH), BF16),
        q_shape,
    ] + ([q_shape] if transposed else []) + [
        jax.ShapeDtypeStruct((nb, HEADS_PAD, nt * m), F32),
        jax.ShapeDtypeStruct((b, t, POOL_WIDTH), BF16),
    ]
    out_specs = slab_specs + [
        pl.BlockSpec((bb, tt, ATT_WIDTH), lambda i, j: (i, j, 0)),
        q_spec,
    ] + ([q_spec] if transposed else []) + [
        pl.BlockSpec((None, HEADS_PAD, m), lambda i, j: (i, 0, j)),
        pl.BlockSpec((bb, tt, POOL_WIDTH), lambda i, j: (i, j, 0)),
    ]
    args = ([] if slabs is None else list(slabs)) + [
        x, sh, sc, hist, w["w_in"], w["w_f"], w["w_ft"],
        w["bf_row"], w["bf_col"], w["qg_col" if transposed else "qg"],
        w["kg_col" if transposed else "kg"], w["ind"], w["ind_t"], w["w_pool"],
        w["pool_scale"],
    ]
    outs = pl.pallas_call(
        functools.partial(_in_proj_kernel, n_alias=n_alias, bb=bb, tt=tt, pos0=pos0,
                          transposed=transposed),
        grid=(nb, nt),
        in_specs=in_specs,
        out_specs=out_specs,
        out_shape=out_shape,
        scratch_shapes=[pltpu.VMEM((bb, tt + HIST_ROWS, POOL_WIDTH), F32)],
        input_output_aliases={i: i for i in range(n_alias)},
        compiler_params=pltpu.CompilerParams(
            dimension_semantics=("arbitrary", "arbitrary"), vmem_limit_bytes=VMEM_LIMIT),
        name="in_proj_prompt" if transposed else "in_proj_sample",
    )(*args)
    return outs[:4], outs[4:]


def _cumsum_kernel(x_ref, o_ref):
    rb, length = x_ref.shape
    c = CUMSUM_CHUNK
    tri = (lax.broadcasted_iota(jnp.int32, (c, c), 0)
           <= lax.broadcasted_iota(jnp.int32, (c, c), 1)).astype(F32)
    total = jnp.zeros((rb, 1), F32)
    for i in range(length // c):
        y = total + jnp.dot(x_ref[:, i * c:(i + 1) * c], tri, precision=lax.Precision.HIGHEST,
                            preferred_element_type=F32)
        o_ref[:, i * c:(i + 1) * c] = y
        total = y[:, c - 1:c]


def _cumsum_time(x, rows_per_block):
    length = x.shape[-1]
    r = int(np.prod(x.shape[:-1]))
    rb = rows_per_block
    out = pl.pallas_call(
        _cumsum_kernel,
        grid=(r // rb,),
        in_specs=[pl.BlockSpec((rb, length), lambda i: (i, 0))],
        out_specs=pl.BlockSpec((rb, length), lambda i: (i, 0)),
        out_shape=jax.ShapeDtypeStruct((r, length), F32),
        compiler_params=pltpu.CompilerParams(
            dimension_semantics=("arbitrary",), vmem_limit_bytes=VMEM_LIMIT),
        name="cumsum_time",
    )(x.reshape(r, length))
    return out.reshape(x.shape)


def _fill_key_forget_sums(fcol_ref, fkb_ref, hp):
    chunk = 512
    for pair in range(N_HEADS // 2):
        @pl.when(hp == pair)
        def _(pair=pair):
            def fill(c, carry):
                r0 = pl.multiple_of(c * chunk, chunk)
                for hh in range(2):
                    h = 2 * pair + hh
                    col = fcol_ref[pl.ds(r0, chunk), h:h + 1]
                    fkb_ref[hh, pl.ds(r0, chunk), :] = jnp.broadcast_to(col * LOG2E, (chunk, LANES))
                return carry
            lax.fori_loop(0, fcol_ref.shape[0] // chunk, fill, 0)


def _pair_query_columns(qt):
    row_head = lax.broadcasted_iota(jnp.int32, (2 * HEAD_DIM, 1), 0) // HEAD_DIM
    zero = jnp.zeros_like(qt)
    return jnp.concatenate(
        [jnp.where(row_head == 0, qt, zero), jnp.where(row_head == 1, qt, zero)], axis=1)


def _attn_prompt_bounded_kernel(kmax_ref, qt_ref, k_ref, vt_ref, ft_ref, fcol_ref, o_ref,
                                fkb_ref, p_ref, *, tq, tk):
    hp = pl.program_id(1)
    qi = pl.program_id(2)
    n_full = (qi * tq) // tk

    @pl.when(qi == 0)
    def _():
        _fill_key_forget_sums(fcol_ref, fkb_ref, hp)

    causal = (n_full * tk + lax.broadcasted_iota(jnp.int32, (tk, tq), 0)
              <= qi * tq + lax.broadcasted_iota(jnp.int32, (tk, tq), 1))
    ones_rows = jnp.ones((HEADS_PAD, tk), BF16)
    qt = qt_ref[...]
    q2 = _pair_query_columns(qt)
    qsq = jnp.square(qt.astype(F32))
    shift = []
    for hh in range(2):
        qnorm = jnp.sqrt(jnp.sum(qsq[hh * HEAD_DIM:(hh + 1) * HEAD_DIM], axis=0, keepdims=True))
        shift.append(ft_ref[pl.ds(2 * hp + hh, 1), :] * LOG2E - qnorm * kmax_ref[0, 0])

    def weights(j, masked):
        k0 = pl.multiple_of(j * tk, tk)
        if masked and tq == tk:
            return diagonal_weights(k0)
        s2 = _dot(k_ref[pl.ds(k0, tk), :], q2)
        ps = []
        for hh in range(2):
            fk = fkb_ref[hh, pl.ds(k0, tk), :]
            a = (s2[:, hh * tq:(hh + 1) * tq] - jnp.concatenate([fk] * (tq // LANES), axis=1)
                 + shift[hh])
            if masked:
                a = jnp.where(causal, a, NEG)
            ps.append(jnp.exp2(a).astype(BF16))
        return jnp.concatenate(ps, axis=1)

    def diagonal_weights(k0):
        half = tk // 2
        ps = [[], []]
        for part in range(2):
            r0 = part * half
            c0 = part * half
            width = tq - c0
            q_cols = jnp.concatenate(
                [q2[:, hh * tq + c0:(hh + 1) * tq] for hh in range(2)], axis=1)
            rows = pl.ds(pl.multiple_of(k0 + r0, half), half)
            s2 = _dot(k_ref[rows, :], q_cols)
            for hh in range(2):
                fk = fkb_ref[hh, rows, :]
                a = (s2[:, hh * width:(hh + 1) * width]
                     - jnp.concatenate([fk] * (width // LANES), axis=1) + shift[hh][:, c0:])
                a = jnp.where(causal[r0:r0 + half, c0:], a, NEG)
                p = jnp.exp2(a).astype(BF16)
                if c0:
                    p = jnp.concatenate([jnp.zeros((half, c0), BF16), p], axis=1)
                ps[hh].append(p)
        return jnp.concatenate([jnp.concatenate(ps[0], axis=0), jnp.concatenate(ps[1], axis=0)],
                               axis=1)

    def accumulate(j, acc, p2):
        k0 = pl.multiple_of(j * tk, tk)
        out = []
        for hh in range(2):
            vb = jnp.concatenate(
                [vt_ref[pl.ds(hh * HEAD_DIM, HEAD_DIM), pl.ds(k0, tk)], ones_rows], axis=0)
            out.append(acc[hh] + _dot(vb, p2[:, hh * tq:(hh + 1) * tq]))
        return tuple(out)

    p_ref[...] = weights(n_full, True)

    def step(j, acc):
        p_next = weights(j, False)
        acc = accumulate(jnp.where(j == 0, n_full, j - 1), acc, p_ref[...])
        p_ref[...] = p_next
        return acc

    acc0 = jnp.zeros((HEAD_DIM + HEADS_PAD, tq), F32)
    acc = lax.fori_loop(0, n_full, step, (acc0, acc0))
    acc = accumulate(jnp.maximum(n_full - 1, 0), acc, p_ref[...])
    outs = [a[0:HEAD_DIM] / a[HEAD_DIM:HEAD_DIM + 1] for a in acc]
    o_ref[...] = jnp.concatenate(outs, axis=0).T.astype(BF16)


def _attn_prompt_kernel(qt_ref, k_ref, vt_ref, ft_ref, fcol_ref, o_ref, fkb_ref, s_ref, p_ref,
                        *, tq, tk):
    hp = pl.program_id(1)
    qi = pl.program_id(2)
    n_full = (qi * tq) // tk

    @pl.when(qi == 0)
    def _():
        _fill_key_forget_sums(fcol_ref, fkb_ref, hp)

    causal = (n_full * tk + lax.broadcasted_iota(jnp.int32, (tk, tq), 0)
              <= qi * tq + lax.broadcasted_iota(jnp.int32, (tk, tq), 1))
    ones_rows = jnp.ones((HEADS_PAD, tk), BF16)
    q2 = _pair_query_columns(qt_ref[...])
    fq = [ft_ref[pl.ds(2 * hp + hh, 1), :] * LOG2E for hh in range(2)]

    def biased_scores(j):
        k0 = pl.multiple_of(j * tk, tk)
        s2 = _dot(k_ref[pl.ds(k0, tk), :], q2)
        a, cmax = [], []
        for hh in range(2):
            fk = fkb_ref[hh, pl.ds(k0, tk), :]
            ah = s2[:, hh * tq:(hh + 1) * tq] - jnp.concatenate([fk] * (tq // LANES), axis=1)
            a.append(ah)
            cmax.append(jnp.max(ah, axis=0, keepdims=True))
        return jnp.concatenate(a, axis=1), cmax

    def weights(a2, cmax, m_run):
        ps, alphas, ms = [], [], []
        for hh in range(2):
            m_new = jnp.maximum(m_run[hh], cmax[hh] + fq[hh])
            alphas.append(jnp.exp2(m_run[hh] - m_new))
            ps.append(jnp.exp2(a2[:, hh * tq:(hh + 1) * tq] + (fq[hh] - m_new)).astype(BF16))
            ms.append(m_new)
        return ps, alphas, ms

    def accumulate(j, acc, p, alpha):
        k0 = pl.multiple_of(j * tk, tk)
        out = []
        for hh in range(2):
            vb = jnp.concatenate(
                [vt_ref[pl.ds(hh * HEAD_DIM, HEAD_DIM), pl.ds(k0, tk)], ones_rows], axis=0)
            out.append(alpha[hh] * acc[hh] + _dot(vb, p[hh]))
        return out

    a0, cmax0 = biased_scores(0)
    s_ref[...] = a0
    p_ref[...] = jnp.zeros(p_ref.shape, BF16)

    def step(j, carry):
        acc, m_run, alpha, cmax = carry
        a_cur = s_ref[...]
        p_prev = p_ref[...]
        a_next, cmax_next = biased_scores(j + 1)
        p, alpha_new, m_new = weights(a_cur, cmax, m_run)
        acc = accumulate(jnp.maximum(j - 1, 0), acc, [p_prev[:, :tq], p_prev[:, tq:]], alpha)
        s_ref[...] = a_next
        p_ref[...] = jnp.concatenate(p, axis=1)
        return tuple(acc), tuple(m_new), tuple(alpha_new), tuple(cmax_next)

    acc0 = jnp.zeros((HEAD_DIM + HEADS_PAD, tq), F32)
    m0 = jnp.full((1, tq), NEG, F32)
    one = jnp.ones((1, tq), F32)
    acc, m_run, alpha, _ = lax.fori_loop(
        0, n_full, step, ((acc0, acc0), (m0, m0), (one, one), tuple(cmax0)))
    p_prev = p_ref[...]
    a_diag = s_ref[...]
    a_diag = jnp.concatenate(
        [jnp.where(causal, a_diag[:, hh * tq:(hh + 1) * tq], NEG) for hh in range(2)], axis=1)
    cmax = [jnp.max(a_diag[:, hh * tq:(hh + 1) * tq], axis=0, keepdims=True) for hh in range(2)]
    p, alpha_new, _ = weights(a_diag, cmax, m_run)
    acc = accumulate(jnp.maximum(n_full - 1, 0), acc, [p_prev[:, :tq], p_prev[:, tq:]], alpha)
    acc = accumulate(n_full, acc, p, alpha_new)
    outs = [a[0:HEAD_DIM] / a[HEAD_DIM:HEAD_DIM + 1] for a in acc]
    o_ref[...] = jnp.concatenate(outs, axis=0).T.astype(BF16)


def _attn_prompt(qt, kbf, vt, ft, fcol, q_gain, k_gain, *, tq, tk):
    b, t, _ = kbf.shape
    pairs = N_HEADS // 2
    in_specs = [
        pl.BlockSpec((None, 2 * HEAD_DIM, tq), lambda i, p, j: (i, p, j)),
        pl.BlockSpec((None, t, 2 * HEAD_DIM), lambda i, p, j: (i, 0, p)),
        pl.BlockSpec((None, 2 * HEAD_DIM, t), lambda i, p, j: (i, p, 0)),
        pl.BlockSpec((None, HEADS_PAD, tq), lambda i, p, j: (i, 0, j)),
        pl.BlockSpec((None, t, HEADS_PAD), lambda i, p, j: (i, 0, 0)),
    ]
    common = dict(
        grid=(b, pairs, t // tq),
        out_specs=pl.BlockSpec((None, tq, 2 * HEAD_DIM), lambda i, p, j: (i, j, p)),
        out_shape=jax.ShapeDtypeStruct((b, t, ATT_WIDTH), BF16),
        compiler_params=pltpu.CompilerParams(
            dimension_semantics=("arbitrary", "arbitrary", "arbitrary"),
            vmem_limit_bytes=VMEM_LIMIT),
    )
    key_sums = pltpu.VMEM((2, t, LANES), F32)
    weights_tile = pltpu.VMEM((tk, 2 * tq), BF16)

    def running_max(_, *args):
        return pl.pallas_call(
            functools.partial(_attn_prompt_kernel, tq=tq, tk=tk),
            in_specs=in_specs,
            scratch_shapes=[key_sums, pltpu.VMEM((tk, 2 * tq), F32), weights_tile],
            name="attn_prompt", **common)(*args)

    def bounded(kmax, *args):
        return pl.pallas_call(
            functools.partial(_attn_prompt_bounded_kernel, tq=tq, tk=tk),
            in_specs=[pl.BlockSpec(memory_space=pltpu.SMEM)] + in_specs,
            scratch_shapes=[key_sums, weights_tile],
            name="attn_prompt_bounded", **common)(kmax, *args)

    qmax = 1.01 * LOG2E * jnp.max(jnp.abs(q_gain))
    kmax = 1.01 * HEAD_DIM ** 0.5 * jnp.max(jnp.abs(k_gain))
    return lax.cond(2.0 * qmax * kmax <= BOUND_SLACK_OCTAVES, bounded, running_max,
                    kmax.reshape(1, 1).astype(F32), qt, kbf, vt, ft, fcol)


def _attn_sample_kernel(q_ref, ck_ref, cv_ref, kn_ref, vn_ref, f_ref, lrow_ref, lcol_ref, o_ref,
                        qbd_ref, m_ref, l_ref, acc_ref, fq_ref, fnew_ref, *, tk, past, tt):
    i = pl.program_id(0)
    j = pl.program_id(1)
    rows = N_HEADS * tt
    own = ((lax.broadcasted_iota(jnp.int32, (rows, ATT_WIDTH), 0) // tt)
           == (lax.broadcasted_iota(jnp.int32, (rows, ATT_WIDTH), 1) // HEAD_DIM))

    def cached_forget_sums(start, size):
        blk = f_ref[:, :, pl.ds(start, size)]
        mine = lax.broadcasted_iota(jnp.int32, (1, SUBLANES, 1), 1) == i % SUBLANES
        return jnp.sum(jnp.where(mine, blk, 0.0), axis=1)

    @pl.when(j == 0)
    def _():
        q = q_ref[...]
        qbd_ref[...] = jnp.where(own, jnp.concatenate([q] * N_HEADS, axis=0), jnp.zeros((), BF16))
        m_ref[...] = jnp.full(m_ref.shape, NEG, F32)
        l_ref[...] = jnp.zeros(l_ref.shape, F32)
        acc_ref[...] = jnp.zeros(acc_ref.shape, F32)
        f_last = cached_forget_sums(past - 1, 1)
        upper = (lax.broadcasted_iota(jnp.int32, (tt, tt), 0)
                 <= lax.broadcasted_iota(jnp.int32, (tt, tt), 1)).astype(F32)
        run = jnp.dot(lrow_ref[...], upper, precision=lax.Precision.HIGHEST,
                      preferred_element_type=F32)
        fnew_ref[...] = f_last + run[:N_HEADS]
        r_i = lax.broadcasted_iota(jnp.int32, (rows, rows), 0)
        c_i = lax.broadcasted_iota(jnp.int32, (rows, rows), 1)
        lower = ((r_i // tt == c_i // tt) & (c_i <= r_i)).astype(F32)
        run_col = jnp.dot(lower, jnp.broadcast_to(lcol_ref[...], (rows, LANES)),
                          precision=lax.Precision.HIGHEST, preferred_element_type=F32)
        f_last_rows = jnp.broadcast_to(f_last[:, None, :], (N_HEADS, tt, 1)).reshape(rows, 1)
        fq_ref[...] = f_last_rows + run_col[:, 0:1]

    fq3 = fq_ref[...].reshape(N_HEADS, tt, 1)

    def update(kb, vb, fk, mask, time_on_lanes):
        if time_on_lanes:
            n = kb.shape[1]
            s = _dot(qbd_ref[...], kb.astype(BF16))
        else:
            n = kb.shape[0]
            s = _dot_nt(qbd_ref[...], kb.astype(BF16))
        s = s.reshape(N_HEADS, tt, n) + fq3 - fk[:, None, :]
        if mask is not None:
            s = jnp.where(mask, s, NEG)
        m_old = m_ref[...].reshape(N_HEADS, tt, 1)
        m_new = jnp.maximum(m_old, jnp.max(s, axis=-1, keepdims=True))
        alpha = jnp.exp(m_old - m_new)
        p = jnp.exp(s - m_new)
        l_ref[...] = (alpha * l_ref[...].reshape(N_HEADS, tt, 1)
                      + jnp.sum(p, axis=-1, keepdims=True)).reshape(rows, 1)
        m_ref[...] = m_new.reshape(rows, 1)
        pb = p.reshape(rows, n).astype(BF16)
        pv = _dot_nt(pb, vb.astype(BF16)) if time_on_lanes else _dot(pb, vb.astype(BF16))
        acc_ref[...] = alpha.reshape(rows, 1) * acc_ref[...] + pv

    k0 = pl.multiple_of(j * tk, tk)
    update(ck_ref[...], cv_ref[...], cached_forget_sums(k0, tk), None, True)

    @pl.when(j == pl.num_programs(1) - 1)
    def _():
        mask = (lax.broadcasted_iota(jnp.int32, (1, tt, tt), 2)
                <= lax.broadcasted_iota(jnp.int32, (1, tt, tt), 1))
        update(kn_ref[...], vn_ref[...], fnew_ref[...], mask, False)
        o = jnp.where(own, acc_ref[...] / l_ref[...], 0.0).reshape(N_HEADS, tt, ATT_WIDTH)
        o_ref[...] = jnp.sum(o, axis=0).astype(BF16)


def _attn_sample(q, cache_k, cache_v, k_new, v_new, f_past, logf_row, logf_col, layer, *, tk):
    nb, tt, _ = q.shape
    past = cache_k.shape[3]
    rows = N_HEADS * tt
    return pl.pallas_call(
        functools.partial(_attn_sample_kernel, tk=tk, past=past, tt=tt),
        grid=(nb, past // tk),
        in_specs=[
            pl.BlockSpec((None, tt, ATT_WIDTH), lambda i, j: (i, 0, 0)),
            pl.BlockSpec((None, None, ATT_WIDTH, tk), lambda i, j: (layer, i, 0, j)),
            pl.BlockSpec((None, None, ATT_WIDTH, tk), lambda i, j: (layer, i, 0, j)),
            pl.BlockSpec((None, None, tt, ATT_WIDTH), lambda i, j: (layer, i, 0, 0)),
            pl.BlockSpec((None, None, tt, ATT_WIDTH), lambda i, j: (layer, i, 0, 0)),
            pl.BlockSpec((None, N_HEADS, SUBLANES, past), lambda i, j: (layer, 0, i // SUBLANES, 0)),
            pl.BlockSpec((None, HEADS_PAD, tt), lambda i, j: (i, 0, 0)),
            pl.BlockSpec((None, rows, 1), lambda i, j: (i, 0, 0)),
        ],
        out_specs=pl.BlockSpec((None, tt, ATT_WIDTH), lambda i, j: (i, 0, 0)),
        out_shape=jax.ShapeDtypeStruct((nb, tt, ATT_WIDTH), BF16),
        scratch_shapes=[
            pltpu.VMEM((rows, ATT_WIDTH), BF16),
            pltpu.VMEM((rows, 1), F32),
            pltpu.VMEM((rows, 1), F32),
            pltpu.VMEM((rows, ATT_WIDTH), F32),
            pltpu.VMEM((rows, 1), F32),
            pltpu.VMEM((N_HEADS, tt), F32),
        ],
        compiler_params=pltpu.CompilerParams(
            dimension_semantics=("arbitrary", "arbitrary"), vmem_limit_bytes=VMEM_LIMIT),
        name="attn_sample",
    )(q, cache_k, cache_v, k_new, v_new, f_past, logf_row, logf_col)


def _out_mlp_kernel(x_ref, py_ref, att_ref, g1_ref, sh_ref, sc_ref, g2_ref,
                    wo_ref, wup_ref, wdn_ref, y_ref, *, bb, tt, ff_chunk):
    m = bb * tt
    mix = (_dot(py_ref[...].reshape(m, POOL_WIDTH), wo_ref[0:POOL_WIDTH, :])
           + _dot(att_ref[...].reshape(m, ATT_WIDTH), wo_ref[POOL_WIDTH:, :]))
    x1 = x_ref[...] + g1_ref[...] * mix.reshape(bb, tt, D_MODEL)
    ms = jnp.mean(x1 * x1, axis=-1, keepdims=True)
    h = (x1 * lax.rsqrt(ms + EPS)) * (1.0 + sc_ref[...]) + sh_ref[...]
    hb = h.reshape(m, D_MODEL).astype(BF16)
    ff = jnp.zeros((m, D_MODEL), F32)
    for c in range(D_FF // ff_chunk):
        up = _dot(hb, wup_ref[:, c * ff_chunk:(c + 1) * ff_chunk])
        act = jnp.square(jnp.maximum(up, 0.0)).astype(BF16)
        ff = ff + _dot(act, wdn_ref[c * ff_chunk:(c + 1) * ff_chunk, :])
    y_ref[...] = x1 + g2_ref[...] * ff.reshape(bb, tt, D_MODEL)


def _out_mlp(x, py, att, g1, sh, sc, g2, w, layer, *, bb, tt, name):
    b, t, _ = x.shape

    def tile(width):
        return pl.BlockSpec((bb, tt, width), lambda i, j: (i, j, 0))

    def per_b():
        return pl.BlockSpec((bb, 1, D_MODEL), lambda i, j: (i, 0, 0))

    return pl.pallas_call(
        functools.partial(_out_mlp_kernel, bb=bb, tt=tt, ff_chunk=1024),
        grid=(b // bb, t // tt),
        in_specs=[
            tile(D_MODEL), tile(POOL_WIDTH), tile(ATT_WIDTH),
            per_b(), per_b(), per_b(), per_b(),
            _resident_layer((POOL_WIDTH + ATT_WIDTH, D_MODEL), layer),
            _resident_layer((D_MODEL, D_FF), layer), _resident_layer((D_FF, D_MODEL), layer),
        ],
        out_specs=tile(D_MODEL),
        out_shape=jax.ShapeDtypeStruct((b, t, D_MODEL), F32),
        compiler_params=pltpu.CompilerParams(
            dimension_semantics=("arbitrary", "arbitrary"), vmem_limit_bytes=VMEM_LIMIT),
        name=name,
    )(x, py, att, g1, sh, sc, g2, w["w_out"], w["w_up"], w["w_down"])


def _layer_weights(depth, layer, stacked, w_in, b_f, q_gain, k_gain, w_pool, pool_scale):
    w_f = w_in[layer, :, POOL_WIDTH + 3 * ATT_WIDTH:]
    head_of_col = np.arange(ATT_WIDTH) // HEAD_DIM
    ind = (head_of_col[:, None] == np.arange(LANES)[None, :]).astype(np.float32)
    w_pool_bd = jnp.zeros((POOL_WIDTH, POOL_WIDTH), F32)
    for g in range(len(POOL_WINDOWS)):
        sl = slice(g * POOL_GROUP_DIM, (g + 1) * POOL_GROUP_DIM)
        w_pool_bd = w_pool_bd.at[sl, sl].set(w_pool[layer, g])
    return {
        "depth": depth,
        **stacked,
        "w_f": jnp.pad(w_f, ((0, 0), (0, LANES - N_HEADS))).astype(BF16),
        "w_ft": jnp.pad(w_f.T, ((0, HEADS_PAD - N_HEADS), (0, 0))).astype(BF16),
        "bf_row": jnp.pad(b_f[layer], (0, LANES - N_HEADS)).reshape(1, LANES),
        "bf_col": jnp.pad(b_f[layer], (0, HEADS_PAD - N_HEADS)).reshape(HEADS_PAD, 1),
        "qg": jnp.tile(q_gain[layer], N_HEADS).reshape(1, ATT_WIDTH),
        "kg": jnp.tile(k_gain[layer], N_HEADS).reshape(1, ATT_WIDTH),
        "qg_col": jnp.broadcast_to(jnp.tile(q_gain[layer], N_HEADS)[:, None], (ATT_WIDTH, LANES)),
        "kg_col": jnp.broadcast_to(jnp.tile(k_gain[layer], N_HEADS)[:, None], (ATT_WIDTH, LANES)),
        "ind": jnp.asarray(ind, BF16),
        "ind_t": jnp.asarray(ind.T, BF16),
        "w_pool": w_pool_bd.astype(BF16),
        "pool_scale": pool_scale[layer].reshape(1, POOL_WIDTH),
    }


def _mod_parts(mod_rows):
    b = mod_rows.shape[0]
    parts = mod_rows.reshape(b, 6, 1, D_MODEL)
    return [parts[:, i] for i in range(6)]


def _time_major(slab):
    depth, b, _, t = slab.shape
    return jnp.transpose(slab.reshape(depth, b, N_HEADS, HEAD_DIM, t), (0, 1, 4, 2, 3))


def kernel(x_prompt, x_sample, c_prompt, c_sample, cache_k, cache_v, cache_logf, cache_pool,
           w_mod, b_mod, w_in, b_f, q_gain, k_gain, w_pool, pool_scale, w_out, w_up, w_down):
    depth = w_mod.shape[0]
    bp, seq, _ = x_prompt.shape
    bs, dec_seq, _ = x_sample.shape
    past = cache_k.shape[2]
    tq = 1024
    tk_p = 1024
    tm_in = 512
    tm_out = 512
    tk_cache = 2048

    n_c = bp + bs
    c_rows = -(-n_c // 8) * 8
    c_all = jnp.pad(jnp.concatenate([c_prompt, c_sample], axis=0), ((0, c_rows - n_c), (0, 0)))
    mod = _modulation(c_all, w_mod, b_mod)

    ck = jnp.transpose(cache_k, (0, 1, 3, 4, 2)).reshape(depth, bs, ATT_WIDTH, past)
    cv = jnp.transpose(cache_v, (0, 1, 3, 4, 2)).reshape(depth, bs, ATT_WIDTH, past)
    f_past = _cumsum_time(jnp.transpose(cache_logf, (0, 3, 1, 2)), rows_per_block=8 * bs)
    stacked = {"w_in": jnp.swapaxes(w_in, 1, 2).astype(BF16), "w_out": w_out.astype(BF16),
               "w_up": w_up.astype(BF16), "w_down": w_down.astype(BF16)}
    hist_p = jnp.zeros((bp, HIST_ROWS, POOL_WIDTH), F32)

    y_p, y_s = x_prompt, x_sample
    slabs_p = slabs_s = None
    for l in range(depth):
        w = _layer_weights(depth, l, stacked, w_in, b_f, q_gain, k_gain, w_pool, pool_scale)
        sh1, sc1, g1, sh2, sc2, g2 = _mod_parts(mod[l, :bp])
        slabs_p, (kbf, qt, vt, logft, py) = _in_proj(
            y_p, sh1, sc1, hist_p, w, slabs_p, l, bb=1, tt=tm_in, pos0=0, transposed=True)
        ft = _cumsum_time(logft, rows_per_block=bp * HEADS_PAD)
        att = _attn_prompt(qt, kbf, vt, ft, jnp.swapaxes(ft, 1, 2), q_gain[l], k_gain[l],
                           tq=tq, tk=tk_p)
        y_p = _out_mlp(y_p, py, att, g1, sh2, sc2, g2, w, l, bb=1, tt=tm_out, name="out_mlp_prompt")

        sh1, sc1, g1, sh2, sc2, g2 = _mod_parts(mod[l, bp:n_c])
        hist_s = jnp.pad(cache_pool[l], ((0, 0), (HIST_ROWS - POOL_HIST, 0), (0, 0)))
        slabs_s, (_, q_s, logft_s, py_s) = _in_proj(
            y_s, sh1, sc1, hist_s, w, slabs_s, l, bb=bs, tt=dec_seq, pos0=past, transposed=False)
        logf_row = jnp.swapaxes(logft_s.reshape(HEADS_PAD, bs, dec_seq), 0, 1)
        logf_col = logf_row[:, :N_HEADS].reshape(bs, N_HEADS * dec_seq, 1)
        att_s = _attn_sample(q_s, ck, cv, slabs_s[0], slabs_s[1], f_past, logf_row, logf_col, l,
                             tk=tk_cache)
        y_s = _out_mlp(y_s, py_s, att_s, g1, sh2, sc2, g2, w, l, bb=bs, tt=dec_seq,
                       name="out_mlp_sample")

    k_p, v_p, f_p, p_p = slabs_p
    k_s, v_s, f_s, p_s = slabs_s
    return (y_p, y_s,
            _time_major(k_p), _time_major(v_p), f_p, p_p,
            k_s.reshape(depth, bs, dec_seq, N_HEADS, HEAD_DIM), v_s.reshape(depth, bs, dec_seq, N_HEADS, HEAD_DIM),
            f_s, p_s)
```
